```python
import jax, jax.numpy as jnp
from jax import lax
import numpy as np

D_MODEL = 2048
BATCH = 2
SEQ = 8192
DEPTH = 4

GRID_W = 64
CTX_LEN = 256
HEAD_DIM = 128
NA_HEADS = 8
NA_WIDTH = NA_HEADS * HEAD_DIM
NA_WIN_ROWS = 8
NA_WIN_COLS = 16
SG_GROUPS = 8
SG_DIM = 128
SG_WIDTH = SG_GROUPS * SG_DIM
SG_CHUNK = 128
EVEN_IN_W = 3 * NA_WIDTH + 2 * SG_WIDTH
EVEN_OUT_W = NA_WIDTH + SG_WIDTH
GQA_HEADS = D_MODEL // HEAD_DIM
GQA_KV_HEADS = 4
GQA_GROUP = GQA_HEADS // GQA_KV_HEADS
GQA_Q_W = GQA_HEADS * HEAD_DIM
GQA_KV_W = GQA_KV_HEADS * HEAD_DIM
Q_BLOCK = 128
ROPE_THETA = 10000.0
ROPE_AXIS_DIM = HEAD_DIM // 2
MOE_GROUPS = 4
MOE_EXPERTS_PER_GROUP = 8
MOE_EXPERTS = MOE_GROUPS * MOE_EXPERTS_PER_GROUP
MOE_TOP_K = 2
MOE_HIDDEN = 512
MOE_BLOCK = 128
N_EVEN = (DEPTH + 1) // 2
N_ODD = DEPTH // 2
RMS_EPS = 1e-6
LN_EPS = 1e-5

kernel_name = "hybrid_na_sgu_gqa_hmoe_dit"

F32 = jnp.float32


def rms_norm(x, g):
    xf = x.astype(F32)
    y = xf * lax.rsqrt(jnp.mean(xf * xf, axis=-1, keepdims=True) + RMS_EPS)
    return (y * g.astype(F32)).astype(x.dtype)


def layer_norm(x, g, b):
    xf = x.astype(F32)
    mu = jnp.mean(xf, axis=-1, keepdims=True)
    xc = xf - mu
    var = jnp.mean(xc * xc, axis=-1, keepdims=True)
    return (xc * lax.rsqrt(var + LN_EPS) * g.astype(F32) + b.astype(F32)).astype(x.dtype)


def axial_rope(n_tokens):
    t = jnp.arange(n_tokens, dtype=jnp.int32)
    row = (t // GRID_W).astype(F32)
    col = (t % GRID_W).astype(F32)
    inv_freq = ROPE_THETA ** (-jnp.arange(ROPE_AXIS_DIM // 2, dtype=F32) * 2.0 / ROPE_AXIS_DIM)
    ang = jnp.concatenate([row[:, None] * inv_freq, col[:, None] * inv_freq], axis=-1)
    return jnp.cos(ang), jnp.sin(ang)


def apply_rope(x, cos, sin):
    xf = x.astype(F32).reshape(*x.shape[:-1], HEAD_DIM // 2, 2)
    x0, x1 = xf[..., 0], xf[..., 1]
    cs, sn = cos[None, :, None, :], sin[None, :, None, :]
    out = jnp.stack([x0 * cs - x1 * sn, x0 * sn + x1 * cs], axis=-1)
    return out.reshape(x.shape).astype(x.dtype)


def dense_attention(q, k, v):
    s = jnp.einsum('bqkgd,bskd->bkgqs', q, k).astype(F32) * (HEAD_DIM ** -0.5)
    p = jax.nn.softmax(s, axis=-1).astype(v.dtype)
    o = jnp.einsum('bkgqs,bskd->bqkgd', p, v)
    return o.reshape(q.shape[0], q.shape[1], -1)


def blocked_attention(q, k_all, v_all):
    B, S = q.shape[:2]
    nq = S // Q_BLOCK
    qb = q.reshape(B, nq, Q_BLOCK, GQA_KV_HEADS, GQA_GROUP, HEAD_DIM).transpose(1, 0, 2, 3, 4, 5)

    def block(q_blk):
        s = jnp.einsum('bqkgd,bskd->bkgqs', q_blk, k_all).astype(F32) * (HEAD_DIM ** -0.5)
        p = jax.nn.softmax(s, axis=-1).astype(v_all.dtype)
        return jnp.einsum('bkgqs,bskd->bqkgd', p, v_all)

    o = lax.map(block, qb)
    return o.transpose(1, 0, 2, 3, 4, 5).reshape(B, S, -1)


def neighbourhood_attention(q, k, v, kc, vc, rpb, n_rows):
    B, S, H, dh = q.shape
    win_r = min(NA_WIN_ROWS, n_rows)
    n_nb = win_r * NA_WIN_COLS
    rows = jnp.arange(n_rows, dtype=jnp.int32)
    cols = jnp.arange(GRID_W, dtype=jnp.int32)
    key_rows = jnp.clip(rows - win_r // 2, 0, n_rows - win_r)[:, None] + jnp.arange(win_r, dtype=jnp.int32)
    key_cols = jnp.clip(cols - NA_WIN_COLS // 2, 0, GRID_W - NA_WIN_COLS)[:, None] + jnp.arange(NA_WIN_COLS, dtype=jnp.int32)
    off_rows = key_rows - rows[:, None] + (NA_WIN_ROWS - 1)
    off_cols = key_cols - cols[:, None] + (NA_WIN_COLS - 1)
    scale = dh ** -0.5
    q_rows = q.reshape(B, n_rows, GRID_W, H, dh).transpose(1, 0, 2, 3, 4)

    def row_block(args):
        q_blk, kr, orr = args
        idx = (kr[None, :, None] * GRID_W + key_cols[:, None, :]).reshape(GRID_W, n_nb)
        k_nb = k[:, idx]
        v_nb = v[:, idx]
        bias = rpb[:, orr[None, :, None], off_cols[:, None, :]].reshape(H, GRID_W, n_nb)
        s_nb = jnp.einsum('bqhd,bqkhd->bhqk', q_blk, k_nb).astype(F32) * scale + bias.astype(F32)[None]
        s_ctx = jnp.einsum('bqhd,bkhd->bhqk', q_blk, kc).astype(F32) * scale
        p = jax.nn.softmax(jnp.concatenate([s_nb, s_ctx], axis=-1), axis=-1).astype(v.dtype)
        return (jnp.einsum('bhqk,bqkhd->bqhd', p[..., :n_nb], v_nb)
                + jnp.einsum('bhqk,bkhd->bqhd', p[..., n_nb:], vc))

    o = lax.map(row_block, (q_rows, key_rows, off_rows))
    return o.transpose(1, 0, 2, 3, 4).reshape(B, S, H * dh)


def spatial_gating(u, z, w_s, b_s, ln_g, ln_b):
    B, T, G, C = u.shape
    z = layer_norm(jax.nn.gelu(z), ln_g, ln_b)
    zb = z.reshape(B, T // SG_CHUNK, SG_CHUNK, G, C)
    mixed = jnp.einsum('gpq,bnqgc->bnpgc', w_s, zb) + b_s.T[:, :, None]
    return (jax.nn.gelu(u) * mixed.reshape(B, T, G, C)).reshape(B, T, G * C)


def na_sgu_mixer(hx, hc, w_in, w_out, rpb, w_s, b_s, ln_g, ln_b, n_rows, with_ctx_out):
    B, S, _ = hx.shape
    L = hc.shape[1]
    cuts = [NA_WIDTH, 2 * NA_WIDTH, 3 * NA_WIDTH, 3 * NA_WIDTH + SG_WIDTH]
    heads = lambda t: t.reshape(t.shape[0], t.shape[1], NA_HEADS, HEAD_DIM)
    groups = lambda t: t.reshape(t.shape[0], t.shape[1], SG_GROUPS, SG_DIM)
    q, k, v, u, z = jnp.split(hx @ w_in, cuts, axis=-1)
    if with_ctx_out:
        qc, kc, vc, uc, zc = jnp.split(hc @ w_in, cuts, axis=-1)
    else:
        kc, vc = jnp.split(hc @ w_in[:, NA_WIDTH:3 * NA_WIDTH], 2, axis=-1)
    kc, vc = heads(kc), heads(vc)
    ax = neighbourhood_attention(heads(q), heads(k), heads(v), kc, vc, rpb, n_rows)
    gx = spatial_gating(groups(u), groups(z), w_s, b_s, ln_g, ln_b)
    yx = jnp.concatenate([ax, gx], axis=-1) @ w_out
    if not with_ctx_out:
        return yx, None
    ac = dense_attention(heads(qc)[:, :, :, None, :], kc, vc)
    gc = spatial_gating(groups(uc), groups(zc), w_s, b_s, ln_g, ln_b)
    yc = jnp.concatenate([ac, gc], axis=-1) @ w_out
    return yx, yc


def gqa_mixer(hx, hc, w_in, w_out, q_gain, k_gain, cos, sin, with_ctx_out):
    B, S, _ = hx.shape
    L = hc.shape[1]
    cuts = [GQA_Q_W, GQA_Q_W + GQA_KV_W]
    q, k, v = jnp.split(hx @ w_in, cuts, axis=-1)
    q = apply_rope(rms_norm(q.reshape(B, S, GQA_HEADS, HEAD_DIM), q_gain), cos, sin)
    k = apply_rope(rms_norm(k.reshape(B, S, GQA_KV_HEADS, HEAD_DIM), k_gain), cos, sin)
    v = v.reshape(B, S, GQA_KV_HEADS, HEAD_DIM)
    if with_ctx_out:
        qc, kc, vc = jnp.split(hc @ w_in, cuts, axis=-1)
    else:
        kc, vc = jnp.split(hc @ w_in[:, GQA_Q_W:], 2, axis=-1)
    kc = rms_norm(kc.reshape(B, L, GQA_KV_HEADS, HEAD_DIM), k_gain)
    vc = vc.reshape(B, L, GQA_KV_HEADS, HEAD_DIM)
    k_all = jnp.concatenate([k, kc], axis=1)
    v_all = jnp.concatenate([v, vc], axis=1)
    q5 = q.reshape(B, S, GQA_KV_HEADS, GQA_GROUP, HEAD_DIM)
    yx = blocked_attention(q5, k_all, v_all) @ w_out
    if not with_ctx_out:
        return yx, None
    qc = rms_norm(qc.reshape(B, L, GQA_HEADS, HEAD_DIM), q_gain).reshape(B, L, GQA_KV_HEADS, GQA_GROUP, HEAD_DIM)
    yc = dense_attention(qc, kc, vc) @ w_out
    return yx, yc


def hierarchical_moe(h, w_group, b_group, w_expert, b_expert, w1, w3, w2):
    n, d = h.shape
    pg = jax.nn.softmax((h @ w_group).astype(F32) + b_group.astype(F32), axis=-1)
    g_val, g_idx = lax.top_k(pg, 1)
    le = ((h @ w_expert).astype(F32) + b_expert.astype(F32)).reshape(n, MOE_GROUPS, MOE_EXPERTS_PER_GROUP)
    le_sel = jnp.take_along_axis(le, g_idx[:, :, None], axis=1)[:, 0]
    e_val, e_idx = lax.top_k(le_sel, MOE_TOP_K)
    gate = jax.nn.softmax(e_val, axis=-1) * g_val
    expert = g_idx * MOE_EXPERTS_PER_GROUP + e_idx

    a = n * MOE_TOP_K
    e_flat = expert.reshape(-1)
    w_flat = gate.reshape(-1)
    tok = jnp.arange(a, dtype=jnp.int32) // MOE_TOP_K
    order = jnp.argsort(e_flat)
    e_sorted = e_flat[order]
    counts = jnp.bincount(e_flat, length=MOE_EXPERTS)
    start = jnp.cumsum(counts) - counts
    padded = ((counts + MOE_BLOCK - 1) // MOE_BLOCK) * MOE_BLOCK
    pad_end = jnp.cumsum(padded)
    pad_start = pad_end - padded
    pos = pad_start[e_sorted] + (jnp.arange(a, dtype=jnp.int32) - start[e_sorted])
    nb = -(-a // MOE_BLOCK) + MOE_EXPERTS
    p = nb * MOE_BLOCK
    buf_tok = jnp.full((p,), n, dtype=jnp.int32).at[pos].set(tok[order])
    buf_w = jnp.zeros((p,), h.dtype).at[pos].set(w_flat[order].astype(h.dtype))
    blk_e = jnp.minimum(jnp.searchsorted(pad_end, jnp.arange(nb, dtype=jnp.int32) * MOE_BLOCK, side='right'),
                        MOE_EXPERTS - 1)
    h_pad = jnp.concatenate([h, jnp.zeros((1, d), h.dtype)], axis=0)
    xb = h_pad[buf_tok].reshape(nb, MOE_BLOCK, d)

    def expert_block(args):
        xe, e = args
        return (jax.nn.silu(xe @ w1[e]) * (xe @ w3[e])) @ w2[e]

    yb = lax.map(expert_block, (xb, blk_e)).reshape(p, d)
    out = jnp.zeros((n + 1, d), h.dtype).at[buf_tok].add(yb * buf_w[:, None])
    return out[:n]


def setup_inputs(seed: int = 0) -> dict:
    key = jax.random.key(seed)
    ks = jax.random.split(key, 27)
    D = D_MODEL

    def nrm(k, shape, s):
        return jax.random.normal(k, shape, F32) * s

    return {
        "x": nrm(ks[0], (BATCH, SEQ, D), 1.0),
        "c": nrm(ks[1], (BATCH, D), 1.0),
        "ctx": nrm(ks[2], (BATCH, CTX_LEN, D), 1.0),
        "c_ctx": nrm(ks[3], (D,), 1.0),
        "mod_w": nrm(ks[4], (DEPTH, D, 6 * D), 0.5 * D ** -0.5),
        "mod_b": nrm(ks[5], (DEPTH, 6 * D), 0.02),
        "norm_mix_g": 1.0 + nrm(ks[6], (DEPTH, D), 0.1),
        "norm_ffn_g": 1.0 + nrm(ks[7], (DEPTH, D), 0.1),
        "norm_final_g": 1.0 + nrm(ks[8], (D,), 0.1),
        "na_sg_w_in": nrm(ks[9], (N_EVEN, D, EVEN_IN_W), D ** -0.5),
        "na_sg_w_out": nrm(ks[10], (N_EVEN, EVEN_OUT_W, D), EVEN_OUT_W ** -0.5),
        "na_rpb": nrm(ks[11], (N_EVEN, NA_HEADS, 2 * NA_WIN_ROWS - 1, 2 * NA_WIN_COLS - 1), 0.2),
        "sg_w_s": nrm(ks[12], (N_EVEN, SG_GROUPS, SG_CHUNK, SG_CHUNK), SG_CHUNK ** -0.5),
        "sg_b_s": nrm(ks[13], (N_EVEN, SG_GROUPS, SG_CHUNK), 0.02),
        "sg_ln_g": 1.0 + nrm(ks[14], (N_EVEN, SG_GROUPS, SG_DIM), 0.1),
        "sg_ln_b": nrm(ks[15], (N_EVEN, SG_GROUPS, SG_DIM), 0.02),
        "gqa_w_in": nrm(ks[16], (N_ODD, D, GQA_Q_W + 2 * GQA_KV_W), D ** -0.5),
        "gqa_w_out": nrm(ks[17], (N_ODD, GQA_Q_W, D), GQA_Q_W ** -0.5),
        "gqa_q_gain": 1.0 + nrm(ks[18], (N_ODD, HEAD_DIM), 0.1),
        "gqa_k_gain": 1.0 + nrm(ks[19], (N_ODD, HEAD_DIM), 0.1),
        "moe_w_group": nrm(ks[20], (DEPTH, D, MOE_GROUPS), D ** -0.5),
        "moe_b_group": nrm(ks[21], (DEPTH, MOE_GROUPS), 0.01),
        "moe_w_expert": nrm(ks[22], (DEPTH, D, MOE_EXPERTS), D ** -0.5),
        "moe_b_expert": nrm(ks[23], (DEPTH, MOE_EXPERTS), 0.01),
        "moe_w1": nrm(ks[24], (DEPTH, MOE_EXPERTS, D, MOE_HIDDEN), D ** -0.5),
        "moe_w3": nrm(ks[25], (DEPTH, MOE_EXPERTS, D, MOE_HIDDEN), D ** -0.5),
        "moe_w2": nrm(ks[26], (DEPTH, MOE_EXPERTS, MOE_HIDDEN, D), MOE_HIDDEN ** -0.5),
    }


def reference(x, c, ctx, c_ctx, mod_w, mod_b, norm_mix_g, norm_ffn_g, norm_final_g,
              na_sg_w_in, na_sg_w_out, na_rpb, sg_w_s, sg_b_s, sg_ln_g, sg_ln_b,
              gqa_w_in, gqa_w_out, gqa_q_gain, gqa_k_gain,
              moe_w_group, moe_b_group, moe_w_expert, moe_b_expert, moe_w1, moe_w3, moe_w2):
    B, S, D = x.shape
    L = ctx.shape[1]
    n_rows = S // GRID_W
    cos, sin = axial_rope(S)
    cond_x = jax.nn.silu(c)
    cond_c = jax.nn.silu(c_ctx)
    h_ctx = ctx
    for layer in range(DEPTH):
        last = layer == DEPTH - 1
        mod_x = cond_x @ mod_w[layer] + mod_b[layer]
        sh1, sc1, ga1, sh2, sc2, ga2 = jnp.split(mod_x[:, None, :], 6, axis=-1)
        n_cmod = 2 if last else 6
        mc = jnp.split(cond_c @ mod_w[layer][:, :n_cmod * D] + mod_b[layer][:n_cmod * D], n_cmod)
        hx = rms_norm(x, norm_mix_g[layer]) * (1 + sc1) + sh1
        hc = rms_norm(h_ctx, norm_mix_g[layer]) * (1 + mc[1]) + mc[0]
        i = layer // 2
        if layer % 2 == 0:
            yx, yc = na_sgu_mixer(hx, hc, na_sg_w_in[i], na_sg_w_out[i], na_rpb[i], sg_w_s[i], sg_b_s[i],
                                  sg_ln_g[i], sg_ln_b[i], n_rows, not last)
        else:
            yx, yc = gqa_mixer(hx, hc, gqa_w_in[i], gqa_w_out[i], gqa_q_gain[i], gqa_k_gain[i],
                               cos, sin, not last)
        x = x + ga1 * yx
        tokens = (rms_norm(x, norm_ffn_g[layer]) * (1 + sc2) + sh2).reshape(B * S, D)
        if not last:
            h_ctx = h_ctx + mc[2] * yc
            fc = rms_norm(h_ctx, norm_ffn_g[layer]) * (1 + mc[4]) + mc[3]
            tokens = jnp.concatenate([tokens, fc.reshape(B * L, D)], axis=0)
        f = hierarchical_moe(tokens, moe_w_group[layer], moe_b_group[layer], moe_w_expert[layer],
                             moe_b_expert[layer], moe_w1[layer], moe_w3[layer], moe_w2[layer])
        x = x + ga2 * f[:B * S].reshape(B, S, D)
        if not last:
            h_ctx = h_ctx + mc[5] * f[B * S:].reshape(B, L, D)
    return rms_norm(x, norm_final_g)
```

```python
import functools

import jax
import jax.numpy as jnp
from jax import lax
from jax.experimental import pallas as pl
from jax.experimental.pallas import tpu as pltpu

F32 = jnp.float32
BF16 = jnp.bfloat16

GRID_W = 64
HEAD_DIM = 128
NA_HEADS = 8
NA_WIDTH = NA_HEADS * HEAD_DIM
NA_WIN_ROWS = 8
NA_WIN_COLS = 16
SG_GROUPS = 8
SG_DIM = 128
SG_WIDTH = SG_GROUPS * SG_DIM
SG_CHUNK = 128
GQA_KV_HEADS = 4
ROPE_THETA = 10000.0
ROPE_AXIS_DIM = HEAD_DIM // 2
MOE_GROUPS = 4
MOE_EXPERTS_PER_GROUP = 8
MOE_EXPERTS = MOE_GROUPS * MOE_EXPERTS_PER_GROUP
MOE_TOP_K = 2
RMS_EPS = 1e-6
LN_EPS = 1e-5

LOG2E = 1.4426950408889634
MASK_VALUE = -1e30
ATTN_SCALE = HEAD_DIM ** -0.5
LANES = 128
ROUTER_W = LANES
ROW_TILE = 512
MOE_ROWS = 256
VMEM_LIMIT = 48 * 1024 * 1024


def _params(sem):
    return pltpu.CompilerParams(dimension_semantics=sem, vmem_limit_bytes=VMEM_LIMIT)


def _sigmoid(x):
    return 1.0 / (1.0 + jnp.exp(-x))


def _gelu_tanh(x):
    cdf = 0.5 * (1.0 + jnp.tanh(0.7978845608028654 * (x + 0.044715 * (x * x * x))))
    return x * cdf


def _rms_modulate(x, g, sc, sh):
    r = lax.rsqrt(jnp.mean(x * x, axis=-1, keepdims=True) + RMS_EPS)
    return (x * r * g) * (1.0 + sc) + sh


def _mod_kernel(c_ref, w_ref, b_ref, o_ref):
    c = c_ref[...]
    cs = (c * _sigmoid(c)).astype(BF16)
    o_ref[0] = jnp.dot(cs, w_ref[0].astype(BF16), preferred_element_type=F32) + b_ref[0]


def _modulation(cond, mod_w, mod_b):
    depth, d, n = mod_w.shape
    tn = 1024
    return pl.pallas_call(
        _mod_kernel,
        grid=(depth, n // tn),
        in_specs=[pl.BlockSpec((8, d), lambda l, j: (0, 0)),
                  pl.BlockSpec((1, d, tn), lambda l, j: (l, 0, j)),
                  pl.BlockSpec((1, 1, tn), lambda l, j: (l, 0, j))],
        out_specs=pl.BlockSpec((1, 8, tn), lambda l, j: (l, 0, j)),
        out_shape=jax.ShapeDtypeStruct((depth, 8, n), F32),
        compiler_params=_params(("parallel", "parallel")),
        name="adaln_modulation",
    )(cond, mod_w, mod_b.reshape(depth, 1, n))


def _seg_fn(tiles_per_batch, n_batch):
    return lambda i: jnp.minimum(i // tiles_per_batch, n_batch)


def _in_prologue(x_ref, g_ref, sc_ref, sh_ref, hx_ref):
    hx_ref[...] = _rms_modulate(x_ref[...], g_ref[...], sc_ref[...], sh_ref[...]).astype(BF16)


def _in_even_kernel(x_ref, g_ref, sc_ref, sh_ref, w_ref, cs_ref, qkv_ref, uz_ref, hx_ref, *, n_qkv_tiles):
    j = pl.program_id(1)

    @pl.when(j == 0)
    def _():
        _in_prologue(x_ref, g_ref, sc_ref, sh_ref, hx_ref)

    acc = jnp.dot(hx_ref[...], w_ref[...], preferred_element_type=F32)

    @pl.when(j < n_qkv_tiles)
    def _():
        qkv_ref[...] = (acc * cs_ref[...]).astype(BF16)

    @pl.when(j >= n_qkv_tiles)
    def _():
        uz_ref[...] = acc


def _in_odd_kernel(x_ref, g_ref, sc_ref, sh_ref, w_ref, gain_ref, cos_ref, sin_ref, o_ref, hx_ref, *, n_rope_tiles):
    j = pl.program_id(1)

    @pl.when(j == 0)
    def _():
        _in_prologue(x_ref, g_ref, sc_ref, sh_ref, hx_ref)

    acc = jnp.dot(hx_ref[...], w_ref[...], preferred_element_type=F32)
    tm, tn = acc.shape

    @pl.when(j < n_rope_tiles)
    def _():
        cos = cos_ref[...]
        sin = sin_ref[...]
        even_lane = (lax.broadcasted_iota(jnp.int32, (tm, HEAD_DIM), 1) % 2) == 0
        for h in range(tn // HEAD_DIM):
            cols = slice(h * HEAD_DIM, (h + 1) * HEAD_DIM)
            y = acc[:, cols]
            r = lax.rsqrt(jnp.mean(y * y, axis=-1, keepdims=True) + RMS_EPS)
            yn = y * r * gain_ref[:, cols]
            sw = jnp.where(even_lane, pltpu.roll(yn, HEAD_DIM - 1, 1), pltpu.roll(yn, 1, 1))
            o_ref[:, cols] = (yn * cos + sw * sin).astype(BF16)

    @pl.when(j >= n_rope_tiles)
    def _():
        o_ref[...] = acc.astype(BF16)


def _common_in_specs(tm, d, seg):
    return [pl.BlockSpec((tm, d), lambda i, j: (i, 0)),
            pl.BlockSpec((1, d), lambda i, j: (0, 0)),
            pl.BlockSpec((None, None, 1, d), lambda i, j: (seg(i), 1, 0, 0)),
            pl.BlockSpec((None, None, 1, d), lambda i, j: (seg(i), 0, 0, 0))]


def _in_proj_even(xs, g, mods, w, colscale, seg, tm):
    nt, d = xs.shape
    n = w.shape[1]
    tn = 512
    n_qkv = 3 * NA_WIDTH
    nq = n_qkv // tn
    return pl.pallas_call(
        functools.partial(_in_even_kernel, n_qkv_tiles=nq),
        grid=(nt // tm, n // tn),
        in_specs=_common_in_specs(tm, d, seg) + [
            pl.BlockSpec((d, tn), lambda i, j: (0, j)),
            pl.BlockSpec((1, tn), lambda i, j: (0, j))],
        out_specs=[pl.BlockSpec((tm, tn), lambda i, j: (i, jnp.minimum(j, nq - 1))),
                   pl.BlockSpec((tm, tn), lambda i, j: (i, jnp.maximum(j - nq, 0)))],
        out_shape=[jax.ShapeDtypeStruct((nt, n_qkv), BF16),
                   jax.ShapeDtypeStruct((nt, n - n_qkv), F32)],
        scratch_shapes=[pltpu.VMEM((tm, d), BF16)],
        compiler_params=_params(("parallel", "arbitrary")),
        name="in_proj_even",
    )(xs, g.reshape(1, d), mods, mods, w, colscale)


def _in_proj_odd(xs, g, mods, w, gain, cosd, sins, seg, tm, tiles_per_batch, n_lat_tiles, n_rope_cols):
    nt, d = xs.shape
    n = w.shape[1]
    tn = 512
    rope_blk = lambda i, j: (jnp.where(i < n_lat_tiles, i % tiles_per_batch, tiles_per_batch), 0)
    return pl.pallas_call(
        functools.partial(_in_odd_kernel, n_rope_tiles=n_rope_cols // tn),
        grid=(nt // tm, n // tn),
        in_specs=_common_in_specs(tm, d, seg) + [
            pl.BlockSpec((d, tn), lambda i, j: (0, j)),
            pl.BlockSpec((1, tn), lambda i, j: (0, j)),
            pl.BlockSpec((tm, HEAD_DIM), rope_blk),
            pl.BlockSpec((tm, HEAD_DIM), rope_blk)],
        out_specs=pl.BlockSpec((tm, tn), lambda i, j: (i, j)),
        out_shape=jax.ShapeDtypeStruct((nt, n), BF16),
        scratch_shapes=[pltpu.VMEM((tm, d), BF16)],
        compiler_params=_params(("parallel", "arbitrary")),
        name="in_proj_odd",
    )(xs, g.reshape(1, d), mods, mods, w, gain, cosd, sins)


def _na_kernel(q_ref, k_ref, v_ref, kc_ref, vc_ref, bias_ref, o_ref, *, n_rows, rows_per_step, n_lat_steps):
    r = pl.program_id(2)
    win = NA_WIN_ROWS * GRID_W
    dn = (((1,), (1,)), ((), ()))
    kc = kc_ref[...]
    vc = vc_ref[...]

    @pl.when(r < n_lat_steps)
    def _():
        for t in range(rows_per_step):
            row = r * rows_per_step + t
            kr0 = jnp.clip(row - NA_WIN_ROWS // 2, 0, n_rows - NA_WIN_ROWS)
            start = pl.multiple_of(kr0 * GRID_W, GRID_W)
            q = q_ref[t * GRID_W:(t + 1) * GRID_W, :]
            kw = k_ref[pl.ds(start, win), :]
            vw = v_ref[pl.ds(start, win), :]
            s_nb = lax.dot_general(q, kw, dn, preferred_element_type=F32) + bias_ref[row - kr0]
            s_cx = lax.dot_general(q, kc, dn, preferred_element_type=F32)
            m = jnp.maximum(s_nb.max(-1, keepdims=True), s_cx.max(-1, keepdims=True))
            p_nb = jnp.exp2(s_nb - m)
            p_cx = jnp.exp2(s_cx - m)
            l = p_nb.sum(-1, keepdims=True) + p_cx.sum(-1, keepdims=True)
            o = (jnp.dot(p_nb.astype(BF16), vw, preferred_element_type=F32)
                 + jnp.dot(p_cx.astype(BF16), vc, preferred_element_type=F32))
            o_ref[t * GRID_W:(t + 1) * GRID_W, :] = (o / l).astype(BF16)

    @pl.when(r == n_lat_steps)
    def _():
        s = lax.dot_general(q_ref[...], kc, dn, preferred_element_type=F32)
        p = jnp.exp2(s - s.max(-1, keepdims=True))
        l = p.sum(-1, keepdims=True)
        o = jnp.dot(p.astype(BF16), vc, preferred_element_type=F32)
        o_ref[...] = (o / l).astype(BF16)


def _na_bias_table(rpb):
    n_heads = rpb.shape[0]
    s = jnp.arange(NA_WIN_ROWS)[:, None]
    j = jnp.arange(NA_WIN_ROWS)[None, :]
    off_r = j - s + (NA_WIN_ROWS - 1)
    qc = jnp.arange(GRID_W)[:, None]
    kc = jnp.arange(GRID_W)[None, :]
    kc0 = jnp.clip(qc - NA_WIN_COLS // 2, 0, GRID_W - NA_WIN_COLS)
    valid = (kc >= kc0) & (kc < kc0 + NA_WIN_COLS)
    off_c = jnp.clip(kc - qc + (NA_WIN_COLS - 1), 0, 2 * NA_WIN_COLS - 2)
    tbl = rpb[:, off_r[:, :, None, None], off_c[None, None, :, :]].astype(F32)
    tbl = jnp.where(valid[None, None, None], tbl * LOG2E, MASK_VALUE)
    return tbl.transpose(1, 0, 3, 2, 4).reshape(NA_WIN_ROWS, n_heads, GRID_W, NA_WIN_ROWS * GRID_W)


def _na_attention(qkv, bias_tbl, n_batch, seq, ctx_len):
    nt = qkv.shape[0]
    n_rows = seq // GRID_W
    blk = ctx_len
    rows_per_step = blk // GRID_W
    n_lat_steps = seq // blk
    ctx_blk0 = n_batch * seq // blk
    h_k, h_v = NA_HEADS, 2 * NA_HEADS

    def q_idx(b, h, r):
        return (jnp.where(r < n_lat_steps, b * n_lat_steps + r, ctx_blk0 + b), h)

    win = NA_WIN_ROWS * GRID_W
    return pl.pallas_call(
        functools.partial(_na_kernel, n_rows=n_rows, rows_per_step=rows_per_step, n_lat_steps=n_lat_steps),
        grid=(n_batch, NA_HEADS, n_lat_steps + 1),
        in_specs=[pl.BlockSpec((blk, HEAD_DIM), q_idx),
                  pl.BlockSpec((seq, HEAD_DIM), lambda b, h, r: (b, h_k + h)),
                  pl.BlockSpec((seq, HEAD_DIM), lambda b, h, r: (b, h_v + h)),
                  pl.BlockSpec((blk, HEAD_DIM), lambda b, h, r: (ctx_blk0 + b, h_k + h)),
                  pl.BlockSpec((blk, HEAD_DIM), lambda b, h, r: (ctx_blk0 + b, h_v + h)),
                  pl.BlockSpec((NA_WIN_ROWS, None, GRID_W, win), lambda b, h, r: (0, h, 0, 0))],
        out_specs=pl.BlockSpec((blk, HEAD_DIM), q_idx),
        out_shape=jax.ShapeDtypeStruct((nt, NA_WIDTH), BF16),
        compiler_params=_params(("parallel", "parallel", "arbitrary")),
        name="neighbourhood_attention",
    )(qkv, qkv, qkv, qkv, qkv, bias_tbl)


def _sg_kernel(uz_ref, ws_ref, bs_ref, lg_ref, lb_ref, o_ref, *, n_chunks):
    for g in range(SG_GROUPS):
        w = ws_ref[g].astype(BF16)
        for n in range(n_chunks):
            rows = slice(n * SG_CHUNK, (n + 1) * SG_CHUNK)
            u = uz_ref[rows, g * SG_DIM:(g + 1) * SG_DIM]
            z = uz_ref[rows, SG_WIDTH + g * SG_DIM:SG_WIDTH + (g + 1) * SG_DIM]
            zz = _gelu_tanh(z)
            mu = jnp.mean(zz, axis=-1, keepdims=True)
            xc = zz - mu
            var = jnp.mean(xc * xc, axis=-1, keepdims=True)
            zn = xc * lax.rsqrt(var + LN_EPS) * lg_ref[g] + lb_ref[g]
            mixed = jnp.dot(w, zn.astype(BF16), preferred_element_type=F32) + bs_ref[g]
            o_ref[rows, g * SG_DIM:(g + 1) * SG_DIM] = (_gelu_tanh(u) * mixed).astype(BF16)


def _spatial_gating(uz, w_s, b_s, ln_g, ln_b, tm):
    nt = uz.shape[0]
    full3 = lambda i: (0, 0, 0)
    return pl.pallas_call(
        functools.partial(_sg_kernel, n_chunks=tm // SG_CHUNK),
        grid=(nt // tm,),
        in_specs=[pl.BlockSpec((tm, 2 * SG_WIDTH), lambda i: (i, 0)),
                  pl.BlockSpec((SG_GROUPS, SG_CHUNK, SG_CHUNK), full3),
                  pl.BlockSpec((SG_GROUPS, SG_CHUNK, 1), full3),
                  pl.BlockSpec((SG_GROUPS, 1, SG_DIM), full3),
                  pl.BlockSpec((SG_GROUPS, 1, SG_DIM), full3)],
        out_specs=pl.BlockSpec((tm, SG_WIDTH), lambda i: (i, 0)),
        out_shape=jax.ShapeDtypeStruct((nt, SG_WIDTH), BF16),
        compiler_params=_params(("parallel",)),
        name="spatial_gating",
    )(uz, w_s, b_s.reshape(SG_GROUPS, SG_CHUNK, 1), ln_g.reshape(SG_GROUPS, 1, SG_DIM),
      ln_b.reshape(SG_GROUPS, 1, SG_DIM))


def _flash_kernel(q_ref, k_ref, v_ref, kc_ref, vc_ref, o_ref, q_s, *, tq, tk, seq, group, n_lat_q):
    qi = pl.program_id(2)
    for g in range(group):
        q_s[g * tq:(g + 1) * tq, :] = q_ref[:, g * HEAD_DIM:(g + 1) * HEAD_DIM]
    q = q_s[...]
    m_rows = group * tq
    dn = (((1,), (1,)), ((), ()))

    def step(k, v, carry):
        m, l, acc = carry
        s = lax.dot_general(q, k, dn, preferred_element_type=F32)
        m_new = jnp.maximum(m, s.max(-1, keepdims=True))
        alpha = jnp.exp2(m - m_new)
        p = jnp.exp2(s - m_new)
        l = alpha * l + p.sum(-1, keepdims=True)
        acc = alpha * acc + jnp.dot(p.astype(BF16), v, preferred_element_type=F32)
        return m_new, l, acc

    init = (jnp.full((m_rows, 1), MASK_VALUE, F32), jnp.zeros((m_rows, 1), F32),
            jnp.zeros((m_rows, HEAD_DIM), F32))

    def finish(carry):
        _, l, acc = carry
        o = acc / l
        for g in range(group):
            o_ref[:, g * HEAD_DIM:(g + 1) * HEAD_DIM] = o[g * tq:(g + 1) * tq].astype(BF16)

    @pl.when(qi < n_lat_q)
    def _():
        def body(t, carry):
            start = pl.multiple_of(t * tk, tk)
            return step(k_ref[pl.ds(start, tk), :], v_ref[pl.ds(start, tk), :], carry)

        carry = lax.fori_loop(0, seq // tk, body, init)
        finish(step(kc_ref[...], vc_ref[...], carry))

    @pl.when(qi == n_lat_q)
    def _():
        finish(step(kc_ref[...], vc_ref[...], init))


def _flash_attention(qkv, n_batch, seq, ctx_len, n_q_heads):
    nt = qkv.shape[0]
    group = n_q_heads // GQA_KV_HEADS
    tq = ctx_len
    tk = min(1024, seq)
    n_lat_q = seq // tq
    ctx_blk0 = n_batch * seq // tq
    hk0 = n_q_heads
    hv0 = n_q_heads + GQA_KV_HEADS
    gw = group * HEAD_DIM

    def q_idx(b, kh, qi):
        return (jnp.where(qi < n_lat_q, b * n_lat_q + qi, ctx_blk0 + b), kh)

    return pl.pallas_call(
        functools.partial(_flash_kernel, tq=tq, tk=tk, seq=seq, group=group, n_lat_q=n_lat_q),
        grid=(n_batch, GQA_KV_HEADS, n_lat_q + 1),
        in_specs=[pl.BlockSpec((tq, gw), q_idx),
                  pl.BlockSpec((seq, HEAD_DIM), lambda b, kh, qi: (b, hk0 + kh)),
                  pl.BlockSpec((seq, HEAD_DIM), lambda b, kh, qi: (b, hv0 + kh)),
                  pl.BlockSpec((tq, HEAD_DIM), lambda b, kh, qi: (ctx_blk0 + b, hk0 + kh)),
                  pl.BlockSpec((tq, HEAD_DIM), lambda b, kh, qi: (ctx_blk0 + b, hv0 + kh))],
        out_specs=pl.BlockSpec((tq, gw), q_idx),
        out_shape=jax.ShapeDtypeStruct((nt, n_q_heads * HEAD_DIM), BF16),
        scratch_shapes=[pltpu.VMEM((group * tq, HEAD_DIM), BF16)],
        compiler_params=_params(("parallel", "parallel", "arbitrary")),
        name="gqa_flash_attention",
    )(qkv, qkv, qkv, qkv, qkv)


def _out_kernel(*refs, n_lhs):
    lhs_refs = refs[:n_lhs]
    (w_ref, x_ref, ga_ref, g2_ref, sc_ref, sh_ref, wr_ref, br_ref, xo_ref, tok_ref, lg_ref) = refs[n_lhs:]
    acc = None
    k0 = 0
    for a_ref in lhs_refs:
        kp = a_ref.shape[1]
        part = jnp.dot(a_ref[...], w_ref[k0:k0 + kp, :], preferred_element_type=F32)
        acc = part if acc is None else acc + part
        k0 += kp
    xn = x_ref[...] + ga_ref[...] * acc
    xo_ref[...] = xn
    tok = _rms_modulate(xn, g2_ref[...], sc_ref[...], sh_ref[...]).astype(BF16)
    tok_ref[...] = tok
    lg_ref[...] = jnp.dot(tok, wr_ref[...], preferred_element_type=F32) + br_ref[...]


def _out_proj(lhs, w, xs, mods, g2, w_router, b_router, seg, tm):
    nt, d = xs.shape
    k = w.shape[0]
    modspec = lambda which: pl.BlockSpec((None, None, 1, d), lambda i: (seg(i), which, 0, 0))
    return pl.pallas_call(
        functools.partial(_out_kernel, n_lhs=len(lhs)),
        grid=(nt // tm,),
        in_specs=[pl.BlockSpec((tm, a.shape[1]), lambda i: (i, 0)) for a in lhs] + [
            pl.BlockSpec((k, d), lambda i: (0, 0)),
            pl.BlockSpec((tm, d), lambda i: (i, 0)),
            modspec(2),
            pl.BlockSpec((1, d), lambda i: (0, 0)),
            modspec(4),
            modspec(3),
            pl.BlockSpec((d, ROUTER_W), lambda i: (0, 0)),
            pl.BlockSpec((1, ROUTER_W), lambda i: (0, 0))],
        out_specs=[pl.BlockSpec((tm, d), lambda i: (i, 0)),
                   pl.BlockSpec((tm, d), lambda i: (i, 0)),
                   pl.BlockSpec((tm, ROUTER_W), lambda i: (i, 0))],
        out_shape=[jax.ShapeDtypeStruct((nt, d), F32),
                   jax.ShapeDtypeStruct((nt, d), BF16),
                   jax.ShapeDtypeStruct((nt, ROUTER_W), F32)],
        compiler_params=_params(("parallel",)),
        name="out_proj_residual_router",
    )(*lhs, w, xs, mods, g2.reshape(1, d), mods, mods, w_router, b_router)


def _expert_kernel(blk_e_ref, n_used_ref, xs_ref, w1_ref, w3_ref, w2_ref, o_ref):
    b = pl.program_id(0)

    @pl.when(b < n_used_ref[0])
    def _():
        x = xs_ref[...]
        h1 = jnp.dot(x, w1_ref[...].astype(BF16), preferred_element_type=F32)
        h3 = jnp.dot(x, w3_ref[...].astype(BF16), preferred_element_type=F32)
        a = (h1 * _sigmoid(h1) * h3).astype(BF16)
        o_ref[...] = jnp.dot(a, w2_ref[...].astype(BF16), preferred_element_type=F32)

    @pl.when(b >= n_used_ref[0])
    def _():
        o_ref[...] = jnp.zeros_like(o_ref)


def _experts(xs, blk_e, n_used, w1, w3, w2, layer):
    p, d = xs.shape
    hid = w1.shape[-1]
    nb = p // MOE_ROWS
    grid_spec = pltpu.PrefetchScalarGridSpec(
        num_scalar_prefetch=2,
        grid=(nb,),
        in_specs=[pl.BlockSpec((MOE_ROWS, d), lambda b, be, nu: (jnp.minimum(b, nu[0] - 1), 0)),
                  pl.BlockSpec((None, None, d, hid), lambda b, be, nu: (layer, be[b], 0, 0)),
                  pl.BlockSpec((None, None, d, hid), lambda b, be, nu: (layer, be[b], 0, 0)),
                  pl.BlockSpec((None, None, hid, d), lambda b, be, nu: (layer, be[b], 0, 0))],
        out_specs=pl.BlockSpec((MOE_ROWS, d), lambda b, be, nu: (b, 0)),
    )
    return pl.pallas_call(
        _expert_kernel,
        grid_spec=grid_spec,
        out_shape=jax.ShapeDtypeStruct((p, d), F32),
        compiler_params=_params(("arbitrary",)),
        name="moe_experts",
    )(blk_e, n_used, xs, w1, w3, w2)


def _route(logits):
    n = logits.shape[0]
    pg = jax.nn.softmax(logits[:, :MOE_GROUPS], axis=-1)
    g_val, g_idx = lax.top_k(pg, 1)
    le = logits[:, MOE_GROUPS:MOE_GROUPS + MOE_EXPERTS].reshape(n, MOE_GROUPS, MOE_EXPERTS_PER_GROUP)
    le_sel = jnp.take_along_axis(le, g_idx[:, :, None], axis=1)[:, 0]
    e_val, e_idx = lax.top_k(le_sel, MOE_TOP_K)
    gate = jax.nn.softmax(e_val, axis=-1) * g_val
    expert = g_idx * MOE_EXPERTS_PER_GROUP + e_idx
    return gate, expert.astype(jnp.int32)


def _dispatch_plan(expert):
    n = expert.shape[0]
    a = n * MOE_TOP_K
    e_flat = expert.reshape(-1)
    onehot = (e_flat[:, None] == jnp.arange(MOE_EXPERTS, dtype=jnp.int32)[None, :]).astype(jnp.int32)
    csum = jnp.cumsum(onehot, axis=0)
    rank = jnp.take_along_axis(csum, e_flat[:, None], axis=1)[:, 0] - 1
    counts = csum[-1]
    padded = ((counts + MOE_ROWS - 1) // MOE_ROWS) * MOE_ROWS
    pad_end = jnp.cumsum(padded)
    pad_start = pad_end - padded
    pos = pad_start[e_flat] + rank
    nb = -(-a // MOE_ROWS) + MOE_EXPERTS
    buf_tok = jnp.zeros((nb * MOE_ROWS,), jnp.int32).at[pos].set(
        jnp.arange(a, dtype=jnp.int32) // MOE_TOP_K, unique_indices=True)
    blk_e = jnp.minimum(jnp.searchsorted(pad_end, jnp.arange(nb, dtype=jnp.int32) * MOE_ROWS, side='right'),
                        MOE_EXPERTS - 1).astype(jnp.int32)
    n_used = (pad_end[-1:] // MOE_ROWS).astype(jnp.int32)
    return pos.reshape(n, MOE_TOP_K), buf_tok, blk_e, n_used


def _combine_kernel(x_ref, y0_ref, y1_ref, gt_ref, ga_ref, o_ref):
    gt = gt_ref[...]
    f = gt[:, 0:1] * y0_ref[...] + gt[:, 1:2] * y1_ref[...]
    o_ref[...] = x_ref[...] + ga_ref[...] * f


def _combine_final_kernel(x_ref, y0_ref, y1_ref, gt_ref, ga_ref, gf_ref, o_ref):
    gt = gt_ref[...]
    f = gt[:, 0:1] * y0_ref[...] + gt[:, 1:2] * y1_ref[...]
    xn = x_ref[...] + ga_ref[...] * f
    r = lax.rsqrt(jnp.mean(xn * xn, axis=-1, keepdims=True) + RMS_EPS)
    o_ref[...] = xn * r * gf_ref[...]


def _combine(xs, y0, y1, gate, mods, seg, tm, final_g=None, n_out_rows=None):
    nt, d = xs.shape
    rows = nt if n_out_rows is None else n_out_rows
    row_blk = lambda i: (i, 0)
    in_specs = [pl.BlockSpec((tm, d), row_blk), pl.BlockSpec((tm, d), row_blk), pl.BlockSpec((tm, d), row_blk),
                pl.BlockSpec((tm, MOE_TOP_K), row_blk),
                pl.BlockSpec((None, None, 1, d), lambda i: (seg(i), 5, 0, 0))]
    args = [xs, y0, y1, gate, mods]
    kern = _combine_kernel
    if final_g is not None:
        in_specs.append(pl.BlockSpec((1, d), lambda i: (0, 0)))
        args.append(final_g.reshape(1, d))
        kern = _combine_final_kernel
    return pl.pallas_call(
        kern,
        grid=(rows // tm,),
        in_specs=in_specs,
        out_specs=pl.BlockSpec((tm, d), row_blk),
        out_shape=jax.ShapeDtypeStruct((rows, d), F32),
        compiler_params=_params(("parallel",)),
        name="moe_combine_residual",
    )(*args)


def _rope_tables(seq, pad_rows):
    t = jnp.arange(seq, dtype=jnp.int32)
    row = (t // GRID_W).astype(F32)
    col = (t % GRID_W).astype(F32)
    inv_freq = ROPE_THETA ** (-jnp.arange(ROPE_AXIS_DIM // 2, dtype=F32) * 2.0 / ROPE_AXIS_DIM)
    ang = jnp.concatenate([row[:, None] * inv_freq, col[:, None] * inv_freq], axis=-1)
    cos, sin = jnp.cos(ang), jnp.sin(ang)
    cosd = jnp.repeat(cos, 2, axis=-1)
    sins = jnp.stack([-sin, sin], axis=-1).reshape(seq, HEAD_DIM)
    cosd = jnp.concatenate([cosd, jnp.ones((pad_rows, HEAD_DIM), F32)], axis=0)
    sins = jnp.concatenate([sins, jnp.zeros((pad_rows, HEAD_DIM), F32)], axis=0)
    return cosd, sins


def kernel(x, c, ctx, c_ctx, mod_w, mod_b, norm_mix_g, norm_ffn_g, norm_final_g, na_sg_w_in, na_sg_w_out, na_rpb, sg_w_s, sg_b_s, sg_ln_g, sg_ln_b, gqa_w_in, gqa_w_out, gqa_q_gain, gqa_k_gain, moe_w_group, moe_b_group, moe_w_expert, moe_b_expert, moe_w1, moe_w3, moe_w2):
    n_batch, seq, d = x.shape
    ctx_len = ctx.shape[1]
    depth = mod_w.shape[0]
    n_lat = n_batch * seq
    tm = min(ROW_TILE, ctx_len * n_batch)
    assert seq % tm == 0 and (n_batch * ctx_len) % tm == 0 and seq % ctx_len == 0
    assert ctx_len % GRID_W == 0 and n_batch + 1 <= 8
    tiles_per_batch = seq // tm
    seg = _seg_fn(tiles_per_batch, n_batch)
    n_q_heads = d // HEAD_DIM

    xs = jnp.concatenate([x.reshape(n_lat, d), ctx.reshape(n_batch * ctx_len, d)], axis=0)
    cond = jnp.zeros((8, d), F32).at[:n_batch].set(c).at[n_batch].set(c_ctx)
    mod_all = _modulation(cond, mod_w, mod_b)
    cosd, sins = _rope_tables(seq, tm)

    q_w = n_q_heads * HEAD_DIM
    kv_w = GQA_KV_HEADS * HEAD_DIM
    even_colscale = jnp.concatenate([jnp.full((NA_WIDTH,), ATTN_SCALE * LOG2E, F32),
                                     jnp.ones((2 * NA_WIDTH + 2 * SG_WIDTH,), F32)])[None, :]
    out = None
    for layer in range(depth):
        last = layer == depth - 1
        mods = mod_all[layer, :n_batch + 1].reshape(n_batch + 1, 6, 1, d)
        i = layer // 2
        if layer % 2 == 0:
            qkv, uz = _in_proj_even(xs, norm_mix_g[layer], mods, na_sg_w_in[i].astype(BF16), even_colscale, seg, tm)
            att = _na_attention(qkv, _na_bias_table(na_rpb[i]), n_batch, seq, ctx_len)
            gat = _spatial_gating(uz, sg_w_s[i], sg_b_s[i], sg_ln_g[i], sg_ln_b[i], tm)
            lhs, w_out = [att, gat], na_sg_w_out[i].astype(BF16)
        else:
            gain = jnp.concatenate([jnp.tile(gqa_q_gain[i] * (ATTN_SCALE * LOG2E), n_q_heads),
                                    jnp.tile(gqa_k_gain[i], GQA_KV_HEADS),
                                    jnp.ones((kv_w,), F32)])[None, :]
            qkv = _in_proj_odd(xs, norm_mix_g[layer], mods, gqa_w_in[i].astype(BF16), gain, cosd, sins,
                               seg, tm, tiles_per_batch, n_lat // tm, q_w + kv_w)
            att = _flash_attention(qkv, n_batch, seq, ctx_len, n_q_heads)
            lhs, w_out = [att], gqa_w_out[i].astype(BF16)

        pad = ROUTER_W - MOE_GROUPS - MOE_EXPERTS
        w_router = jnp.concatenate([moe_w_group[layer], moe_w_expert[layer], jnp.zeros((d, pad), F32)],
                                   axis=1).astype(BF16)
        b_router = jnp.concatenate([moe_b_group[layer], moe_b_expert[layer], jnp.zeros((pad,), F32)])[None, :]
        xs, tok, logits = _out_proj(lhs, w_out, xs, mods, norm_ffn_g[layer], w_router, b_router, seg, tm)

        gate, expert = _route(logits)
        pos, buf_tok, blk_e, n_used = _dispatch_plan(expert)
        yb = _experts(tok[buf_tok], blk_e, n_used, moe_w1, moe_w3, moe_w2, layer)
        y0 = yb[pos[:, 0]]
        y1 = yb[pos[:, 1]]
        if last:
            out = _combine(xs, y0, y1, gate, mods, seg, tm, final_g=norm_final_g, n_out_rows=n_lat)
        else:
            xs = _combine(xs, y0, y1, gate, mods, seg, tm)
    return out.reshape(n_batch, seq, d)
```

```python
import functools

import jax
import jax.numpy as jnp
from jax import lax
from jax.experimental import pallas as pl
from jax.experimental.pallas import tpu as pltpu

F32 = jnp.float32
BF16 = jnp.bfloat16

GRID_W = 64
HEAD_DIM = 128
NA_HEADS = 8
NA_WIDTH = NA_HEADS * HEAD_DIM
NA_WIN_ROWS = 8
NA_WIN_COLS = 16
SG_GROUPS = 8
SG_DIM = 128
SG_WIDTH = SG_GROUPS * SG_DIM
SG_CHUNK = 128
GQA_KV_HEADS = 4
ROPE_THETA = 10000.0
ROPE_AXIS_DIM = HEAD_DIM // 2
MOE_GROUPS = 4
MOE_EXPERTS_PER_GROUP = 8
MOE_EXPERTS = MOE_GROUPS * MOE_EXPERTS_PER_GROUP
MOE_TOP_K = 2
RMS_EPS = 1e-6
LN_EPS = 1e-5

LOG2E = 1.4426950408889634
MASK_VALUE = -1e30
ATTN_SCALE = HEAD_DIM ** -0.5
LANES = 128
ROUTER_W = LANES
ROW_TILE = 512
MOE_ROWS = 256
ROUTE_TILE = 512
PLAN_LANES = 256
VMEM_LIMIT = 48 * 1024 * 1024


def _params(sem):
    return pltpu.CompilerParams(dimension_semantics=sem, vmem_limit_bytes=VMEM_LIMIT)


def _sigmoid(x):
    return 1.0 / (1.0 + jnp.exp(-x))


def _gelu_tanh(x):
    cdf = 0.5 * (1.0 + jnp.tanh(0.7978845608028654 * (x + 0.044715 * (x * x * x))))
    return x * cdf


def _rms_modulate(x, g, sc, sh):
    r = lax.rsqrt(jnp.mean(x * x, axis=-1, keepdims=True) + RMS_EPS)
    return (x * r * g) * (1.0 + sc) + sh


def _mod_kernel(c_ref, w_ref, b_ref, o_ref):
    c = c_ref[...]
    cs = (c * _sigmoid(c)).astype(BF16)
    o_ref[0] = jnp.dot(cs, w_ref[0].astype(BF16), preferred_element_type=F32) + b_ref[0]


def _modulation(cond, mod_w, mod_b):
    depth, d, n = mod_w.shape
    tn = 1024
    return pl.pallas_call(
        _mod_kernel,
        grid=(depth, n // tn),
        in_specs=[pl.BlockSpec((8, d), lambda l, j: (0, 0)),
                  pl.BlockSpec((1, d, tn), lambda l, j: (l, 0, j)),
                  pl.BlockSpec((1, 1, tn), lambda l, j: (l, 0, j))],
        out_specs=pl.BlockSpec((1, 8, tn), lambda l, j: (l, 0, j)),
        out_shape=jax.ShapeDtypeStruct((depth, 8, n), F32),
        compiler_params=_params(("parallel", "parallel")),
        name="adaln_modulation",
    )(cond, mod_w, mod_b.reshape(depth, 1, n))


def _seg_fn(tiles_per_batch, n_batch):
    return lambda i: jnp.minimum(i // tiles_per_batch, n_batch)


def _in_prologue(x_ref, g_ref, sc_ref, sh_ref, hx_ref):
    hx_ref[...] = _rms_modulate(x_ref[...], g_ref[...], sc_ref[...], sh_ref[...]).astype(BF16)


def _in_even_kernel(x_ref, g_ref, sc_ref, sh_ref, w_ref, cs_ref, qkv_ref, uz_ref, hx_ref, *, n_qkv_tiles):
    j = pl.program_id(1)

    @pl.when(j == 0)
    def _():
        _in_prologue(x_ref, g_ref, sc_ref, sh_ref, hx_ref)

    acc = jnp.dot(hx_ref[...], w_ref[...], preferred_element_type=F32)

    @pl.when(j < n_qkv_tiles)
    def _():
        qkv_ref[...] = (acc * cs_ref[...]).astype(BF16)

    @pl.when(j >= n_qkv_tiles)
    def _():
        uz_ref[...] = acc


def _in_odd_kernel(x_ref, g_ref, sc_ref, sh_ref, w_ref, gain_ref, cos_ref, sin_ref, o_ref, hx_ref, *, n_rope_tiles):
    j = pl.program_id(1)

    @pl.when(j == 0)
    def _():
        _in_prologue(x_ref, g_ref, sc_ref, sh_ref, hx_ref)

    acc = jnp.dot(hx_ref[...], w_ref[...], preferred_element_type=F32)
    tm, tn = acc.shape

    @pl.when(j < n_rope_tiles)
    def _():
        cos = cos_ref[...]
        sin = sin_ref[...]
        even_lane = (lax.broadcasted_iota(jnp.int32, (tm, HEAD_DIM), 1) % 2) == 0
        for h in range(tn // HEAD_DIM):
            cols = slice(h * HEAD_DIM, (h + 1) * HEAD_DIM)
            y = acc[:, cols]
            r = lax.rsqrt(jnp.mean(y * y, axis=-1, keepdims=True) + RMS_EPS)
            yn = y * r * gain_ref[:, cols]
            sw = jnp.where(even_lane, pltpu.roll(yn, HEAD_DIM - 1, 1), pltpu.roll(yn, 1, 1))
            o_ref[:, cols] = (yn * cos + sw * sin).astype(BF16)

    @pl.when(j >= n_rope_tiles)
    def _():
        o_ref[...] = acc.astype(BF16)


def _common_in_specs(tm, d, seg):
    return [pl.BlockSpec((tm, d), lambda i, j: (i, 0)),
            pl.BlockSpec((1, d), lambda i, j: (0, 0)),
            pl.BlockSpec((None, None, 1, d), lambda i, j: (seg(i), 1, 0, 0)),
            pl.BlockSpec((None, None, 1, d), lambda i, j: (seg(i), 0, 0, 0))]


def _in_proj_even(xs, g, mods, w, colscale, seg, tm):
    nt, d = xs.shape
    n = w.shape[1]
    tn = 512
    n_qkv = 3 * NA_WIDTH
    nq = n_qkv // tn
    return pl.pallas_call(
        functools.partial(_in_even_kernel, n_qkv_tiles=nq),
        grid=(nt // tm, n // tn),
        in_specs=_common_in_specs(tm, d, seg) + [
            pl.BlockSpec((d, tn), lambda i, j: (0, j)),
            pl.BlockSpec((1, tn), lambda i, j: (0, j))],
        out_specs=[pl.BlockSpec((tm, tn), lambda i, j: (i, jnp.minimum(j, nq - 1))),
                   pl.BlockSpec((tm, tn), lambda i, j: (i, jnp.maximum(j - nq, 0)))],
        out_shape=[jax.ShapeDtypeStruct((nt, n_qkv), BF16),
                   jax.ShapeDtypeStruct((nt, n - n_qkv), F32)],
        scratch_shapes=[pltpu.VMEM((tm, d), BF16)],
        compiler_params=_params(("parallel", "arbitrary")),
        name="in_proj_even",
    )(xs, g.reshape(1, d), mods, mods, w, colscale)


def _in_proj_odd(xs, g, mods, w, gain, cosd, sins, seg, tm, tiles_per_batch, n_lat_tiles, n_rope_cols):
    nt, d = xs.shape
    n = w.shape[1]
    tn = 512
    rope_blk = lambda i, j: (jnp.where(i < n_lat_tiles, i % tiles_per_batch, tiles_per_batch), 0)
    return pl.pallas_call(
        functools.partial(_in_odd_kernel, n_rope_tiles=n_rope_cols // tn),
        grid=(nt // tm, n // tn),
        in_specs=_common_in_specs(tm, d, seg) + [
            pl.BlockSpec((d, tn), lambda i, j: (0, j)),
            pl.BlockSpec((1, tn), lambda i, j: (0, j)),
            pl.BlockSpec((tm, HEAD_DIM), rope_blk),
            pl.BlockSpec((tm, HEAD_DIM), rope_blk)],
        out_specs=pl.BlockSpec((tm, tn), lambda i, j: (i, j)),
        out_shape=jax.ShapeDtypeStruct((nt, n), BF16),
        scratch_shapes=[pltpu.VMEM((tm, d), BF16)],
        compiler_params=_params(("parallel", "arbitrary")),
        name="in_proj_odd",
    )(xs, g.reshape(1, d), mods, mods, w, gain, cosd, sins)


def _na_kernel(q_ref, k_ref, v_ref, kc_ref, vc_ref, bias_ref, o_ref, *, n_rows, rows_per_step, n_lat_steps):
    r = pl.program_id(2)
    win = NA_WIN_ROWS * GRID_W
    dn = (((1,), (1,)), ((), ()))
    kc = kc_ref[...]
    vc = vc_ref[...]

    @pl.when(r < n_lat_steps)
    def _():
        for t in range(rows_per_step):
            row = r * rows_per_step + t
            kr0 = jnp.clip(row - NA_WIN_ROWS // 2, 0, n_rows - NA_WIN_ROWS)
            start = pl.multiple_of(kr0 * GRID_W, GRID_W)
            q = q_ref[t * GRID_W:(t + 1) * GRID_W, :]
            kw = k_ref[pl.ds(start, win), :]
            vw = v_ref[pl.ds(start, win), :]
            s_nb = lax.dot_general(q, kw, dn, preferred_element_type=F32) + bias_ref[row - kr0]
            s_cx = lax.dot_general(q, kc, dn, preferred_element_type=F32)
            m = jnp.maximum(s_nb.max(-1, keepdims=True), s_cx.max(-1, keepdims=True))
            p_nb = jnp.exp2(s_nb - m)
            p_cx = jnp.exp2(s_cx - m)
            l = p_nb.sum(-1, keepdims=True) + p_cx.sum(-1, keepdims=True)
            o = (jnp.dot(p_nb.astype(BF16), vw, preferred_element_type=F32)
                 + jnp.dot(p_cx.astype(BF16), vc, preferred_element_type=F32))
            o_ref[t * GRID_W:(t + 1) * GRID_W, :] = (o / l).astype(BF16)

    @pl.when(r == n_lat_steps)
    def _():
        s = lax.dot_general(q_ref[...], kc, dn, preferred_element_type=F32)
        p = jnp.exp2(s - s.max(-1, keepdims=True))
        l = p.sum(-1, keepdims=True)
        o = jnp.dot(p.astype(BF16), vc, preferred_element_type=F32)
        o_ref[...] = (o / l).astype(BF16)


def _na_bias_table(rpb):
    n_heads = rpb.shape[0]
    qc = jnp.arange(GRID_W)[:, None]
    kc = jnp.arange(GRID_W)[None, :]
    kc0 = jnp.clip(qc - NA_WIN_COLS // 2, 0, GRID_W - NA_WIN_COLS)
    valid = (kc >= kc0) & (kc < kc0 + NA_WIN_COLS)
    off_c = kc - qc + (NA_WIN_COLS - 1)
    col = jnp.zeros((n_heads, 2 * NA_WIN_ROWS - 1, GRID_W, GRID_W), F32)
    for o in range(2 * NA_WIN_COLS - 1):
        col = col + jnp.where((off_c == o)[None, None], rpb[:, :, o, None, None].astype(F32), 0.0)
    col = jnp.where(valid[None, None], col * LOG2E, MASK_VALUE)
    tbl = jnp.stack([col[:, NA_WIN_ROWS - 1 - s:2 * NA_WIN_ROWS - 1 - s] for s in range(NA_WIN_ROWS)])
    return tbl.transpose(0, 1, 3, 2, 4).reshape(NA_WIN_ROWS, n_heads, GRID_W, NA_WIN_ROWS * GRID_W)


def _na_attention(qkv, bias_tbl, n_batch, seq, ctx_len):
    nt = qkv.shape[0]
    n_rows = seq // GRID_W
    blk = ctx_len
    rows_per_step = blk // GRID_W
    n_lat_steps = seq // blk
    ctx_blk0 = n_batch * seq // blk
    h_k, h_v = NA_HEADS, 2 * NA_HEADS

    def q_idx(b, h, r):
        return (jnp.where(r < n_lat_steps, b * n_lat_steps + r, ctx_blk0 + b), h)

    win = NA_WIN_ROWS * GRID_W
    return pl.pallas_call(
        functools.partial(_na_kernel, n_rows=n_rows, rows_per_step=rows_per_step, n_lat_steps=n_lat_steps),
        grid=(n_batch, NA_HEADS, n_lat_steps + 1),
        in_specs=[pl.BlockSpec((blk, HEAD_DIM), q_idx),
                  pl.BlockSpec((seq, HEAD_DIM), lambda b, h, r: (b, h_k + h)),
                  pl.BlockSpec((seq, HEAD_DIM), lambda b, h, r: (b, h_v + h)),
                  pl.BlockSpec((blk, HEAD_DIM), lambda b, h, r: (ctx_blk0 + b, h_k + h)),
                  pl.BlockSpec((blk, HEAD_DIM), lambda b, h, r: (ctx_blk0 + b, h_v + h)),
                  pl.BlockSpec((NA_WIN_ROWS, None, GRID_W, win), lambda b, h, r: (0, h, 0, 0))],
        out_specs=pl.BlockSpec((blk, HEAD_DIM), q_idx),
        out_shape=jax.ShapeDtypeStruct((nt, NA_WIDTH), BF16),
        compiler_params=_params(("parallel", "parallel", "arbitrary")),
        name="neighbourhood_attention",
    )(qkv, qkv, qkv, qkv, qkv, bias_tbl)


def _sg_kernel(uz_ref, ws_ref, bs_ref, lg_ref, lb_ref, o_ref, *, n_chunks):
    for g in range(SG_GROUPS):
        w = ws_ref[g].astype(BF16)
        for n in range(n_chunks):
            rows = slice(n * SG_CHUNK, (n + 1) * SG_CHUNK)
            u = uz_ref[rows, g * SG_DIM:(g + 1) * SG_DIM]
            z = uz_ref[rows, SG_WIDTH + g * SG_DIM:SG_WIDTH + (g + 1) * SG_DIM]
            zz = _gelu_tanh(z)
            mu = jnp.mean(zz, axis=-1, keepdims=True)
            xc = zz - mu
            var = jnp.mean(xc * xc, axis=-1, keepdims=True)
            zn = xc * lax.rsqrt(var + LN_EPS) * lg_ref[g] + lb_ref[g]
            mixed = jnp.dot(w, zn.astype(BF16), preferred_element_type=F32) + bs_ref[g]
            o_ref[rows, g * SG_DIM:(g + 1) * SG_DIM] = (_gelu_tanh(u) * mixed).astype(BF16)


def _spatial_gating(uz, w_s, b_s, ln_g, ln_b, tm):
    nt = uz.shape[0]
    full3 = lambda i: (0, 0, 0)
    return pl.pallas_call(
        functools.partial(_sg_kernel, n_chunks=tm // SG_CHUNK),
        grid=(nt // tm,),
        in_specs=[pl.BlockSpec((tm, 2 * SG_WIDTH), lambda i: (i, 0)),
                  pl.BlockSpec((SG_GROUPS, SG_CHUNK, SG_CHUNK), full3),
                  pl.BlockSpec((SG_GROUPS, SG_CHUNK, 1), full3),
                  pl.BlockSpec((SG_GROUPS, 1, SG_DIM), full3),
                  pl.BlockSpec((SG_GROUPS, 1, SG_DIM), full3)],
        out_specs=pl.BlockSpec((tm, SG_WIDTH), lambda i: (i, 0)),
        out_shape=jax.ShapeDtypeStruct((nt, SG_WIDTH), BF16),
        compiler_params=_params(("parallel",)),
        name="spatial_gating",
    )(uz, w_s, b_s.reshape(SG_GROUPS, SG_CHUNK, 1), ln_g.reshape(SG_GROUPS, 1, SG_DIM),
      ln_b.reshape(SG_GROUPS, 1, SG_DIM))


def _flash_kernel(q_ref, k_ref, v_ref, kc_ref, vc_ref, o_ref, q_s, *, tq, tk, seq, group, n_lat_q):
    qi = pl.program_id(2)
    for g in range(group):
        q_s[g * tq:(g + 1) * tq, :] = q_ref[:, g * HEAD_DIM:(g + 1) * HEAD_DIM]
    q = q_s[...]
    m_rows = group * tq
    dn = (((1,), (1,)), ((), ()))

    def step(k, v, carry):
        m, l, acc = carry
        s = lax.dot_general(q, k, dn, preferred_element_type=F32)
        m_new = jnp.maximum(m, s.max(-1, keepdims=True))
        alpha = jnp.exp2(m - m_new)
        p = jnp.exp2(s - m_new)
        l = alpha * l + p.sum(-1, keepdims=True)
        acc = alpha * acc + jnp.dot(p.astype(BF16), v, preferred_element_type=F32)
        return m_new, l, acc

    init = (jnp.full((m_rows, 1), MASK_VALUE, F32), jnp.zeros((m_rows, 1), F32),
            jnp.zeros((m_rows, HEAD_DIM), F32))

    def finish(carry):
        _, l, acc = carry
        o = acc / l
        for g in range(group):
            o_ref[:, g * HEAD_DIM:(g + 1) * HEAD_DIM] = o[g * tq:(g + 1) * tq].astype(BF16)

    @pl.when(qi < n_lat_q)
    def _():
        def body(t, carry):
            start = pl.multiple_of(t * tk, tk)
            return step(k_ref[pl.ds(start, tk), :], v_ref[pl.ds(start, tk), :], carry)

        carry = lax.fori_loop(0, seq // tk, body, init)
        finish(step(kc_ref[...], vc_ref[...], carry))

    @pl.when(qi == n_lat_q)
    def _():
        finish(step(kc_ref[...], vc_ref[...], init))


def _flash_attention(qkv, n_batch, seq, ctx_len, n_q_heads):
    nt = qkv.shape[0]
    group = n_q_heads // GQA_KV_HEADS
    tq = ctx_len
    tk = min(1024, seq)
    n_lat_q = seq // tq
    ctx_blk0 = n_batch * seq // tq
    hk0 = n_q_heads
    hv0 = n_q_heads + GQA_KV_HEADS
    gw = group * HEAD_DIM

    def q_idx(b, kh, qi):
        return (jnp.where(qi < n_lat_q, b * n_lat_q + qi, ctx_blk0 + b), kh)

    return pl.pallas_call(
        functools.partial(_flash_kernel, tq=tq, tk=tk, seq=seq, group=group, n_lat_q=n_lat_q),
        grid=(n_batch, GQA_KV_HEADS, n_lat_q + 1),
        in_specs=[pl.BlockSpec((tq, gw), q_idx),
                  pl.BlockSpec((seq, HEAD_DIM), lambda b, kh, qi: (b, hk0 + kh)),
                  pl.BlockSpec((seq, HEAD_DIM), lambda b, kh, qi: (b, hv0 + kh)),
                  pl.BlockSpec((tq, HEAD_DIM), lambda b, kh, qi: (ctx_blk0 + b, hk0 + kh)),
                  pl.BlockSpec((tq, HEAD_DIM), lambda b, kh, qi: (ctx_blk0 + b, hv0 + kh))],
        out_specs=pl.BlockSpec((tq, gw), q_idx),
        out_shape=jax.ShapeDtypeStruct((nt, n_q_heads * HEAD_DIM), BF16),
        scratch_shapes=[pltpu.VMEM((group * tq, HEAD_DIM), BF16)],
        compiler_params=_params(("parallel", "parallel", "arbitrary")),
        name="gqa_flash_attention",
    )(qkv, qkv, qkv, qkv, qkv)


def _out_kernel(*refs, n_lhs):
    lhs_refs = refs[:n_lhs]
    (w_ref, x_ref, ga_ref, g2_ref, sc_ref, sh_ref, wr_ref, br_ref, xo_ref, tok_ref, lg_ref) = refs[n_lhs:]
    acc = None
    k0 = 0
    for a_ref in lhs_refs:
        kp = a_ref.shape[1]
        part = jnp.dot(a_ref[...], w_ref[k0:k0 + kp, :], preferred_element_type=F32)
        acc = part if acc is None else acc + part
        k0 += kp
    xn = x_ref[...] + ga_ref[...] * acc
    xo_ref[...] = xn
    tok = _rms_modulate(xn, g2_ref[...], sc_ref[...], sh_ref[...])
    tok_ref[...] = tok
    lg_ref[...] = lax.dot_general(wr_ref[...], tok.astype(BF16), (((1,), (1,)), ((), ())),
                                  preferred_element_type=F32) + br_ref[...]


def _out_proj(lhs, w, xs, mods, g2, w_router, b_router, seg, tm):
    nt, d = xs.shape
    k = w.shape[0]
    modspec = lambda which: pl.BlockSpec((None, None, 1, d), lambda i: (seg(i), which, 0, 0))
    return pl.pallas_call(
        functools.partial(_out_kernel, n_lhs=len(lhs)),
        grid=(nt // tm,),
        in_specs=[pl.BlockSpec((tm, a.shape[1]), lambda i: (i, 0)) for a in lhs] + [
            pl.BlockSpec((k, d), lambda i: (0, 0)),
            pl.BlockSpec((tm, d), lambda i: (i, 0)),
            modspec(2),
            pl.BlockSpec((1, d), lambda i: (0, 0)),
            modspec(4),
            modspec(3),
            pl.BlockSpec((ROUTER_W, d), lambda i: (0, 0)),
            pl.BlockSpec((ROUTER_W, 1), lambda i: (0, 0))],
        out_specs=[pl.BlockSpec((tm, d), lambda i: (i, 0)),
                   pl.BlockSpec((tm, d), lambda i: (i, 0)),
                   pl.BlockSpec((ROUTER_W, tm), lambda i: (0, i))],
        out_shape=[jax.ShapeDtypeStruct((nt, d), F32),
                   jax.ShapeDtypeStruct((nt, d), F32),
                   jax.ShapeDtypeStruct((ROUTER_W, nt), F32)],
        compiler_params=_params(("parallel",)),
        name="out_proj_residual_router",
    )(*lhs, w, xs, mods, g2.reshape(1, d), mods, mods, w_router, b_router)


def _expert_kernel(blk_e_ref, n_used_ref, xs_ref, w1_ref, w3_ref, w2_ref, o_ref):
    b = pl.program_id(0)

    @pl.when(b < n_used_ref[0])
    def _():
        x = xs_ref[...].astype(BF16)
        h1 = jnp.dot(x, w1_ref[...].astype(BF16), preferred_element_type=F32)
        h3 = jnp.dot(x, w3_ref[...].astype(BF16), preferred_element_type=F32)
        a = (h1 * _sigmoid(h1) * h3).astype(BF16)
        o_ref[...] = jnp.dot(a, w2_ref[...].astype(BF16), preferred_element_type=F32)

    @pl.when(b >= n_used_ref[0])
    def _():
        o_ref[...] = jnp.zeros_like(o_ref)


def _experts(xs, blk_e, n_used, w1, w3, w2, layer):
    p, d = xs.shape
    hid = w1.shape[-1]
    nb = p // MOE_ROWS
    grid_spec = pltpu.PrefetchScalarGridSpec(
        num_scalar_prefetch=2,
        grid=(nb,),
        in_specs=[pl.BlockSpec((MOE_ROWS, d), lambda b, be, nu: (jnp.minimum(b, nu[0] - 1), 0)),
                  pl.BlockSpec((None, None, d, hid), lambda b, be, nu: (layer, be[b], 0, 0)),
                  pl.BlockSpec((None, None, d, hid), lambda b, be, nu: (layer, be[b], 0, 0)),
                  pl.BlockSpec((None, None, hid, d), lambda b, be, nu: (layer, be[b], 0, 0))],
        out_specs=pl.BlockSpec((MOE_ROWS, d), lambda b, be, nu: (b, 0)),
    )
    return pl.pallas_call(
        _expert_kernel,
        grid_spec=grid_spec,
        out_shape=jax.ShapeDtypeStruct((p, d), F32),
        compiler_params=_params(("arbitrary",)),
        name="moe_experts",
    )(blk_e, n_used, xs, w1, w3, w2)


def _first_index(vals, target):
    idx = jnp.full(target.shape, len(vals) - 1, jnp.int32)
    for i in range(len(vals) - 2, -1, -1):
        idx = jnp.where(vals[i] == target, i, idx)
    return idx


def _router_kernel(lg_ref, gate_ref, pos_ref, blk_ref, nused_ref, e_s, rank_s, *, n_tiles):
    tw = ROUTE_TILE
    tri = jnp.where(lax.broadcasted_iota(jnp.int32, (tw, tw), 0) < lax.broadcasted_iota(jnp.int32, (tw, tw), 1),
                    1.0, 0.0).astype(BF16)
    eid = lax.broadcasted_iota(jnp.int32, (MOE_EXPERTS, tw), 0)
    epg = MOE_EXPERTS_PER_GROUP

    def pass1(c, run):
        sl = pl.ds(pl.multiple_of(c * tw, tw), tw)
        g = [lg_ref[i:i + 1, sl] for i in range(MOE_GROUPS)]
        gm = functools.reduce(jnp.maximum, g)
        gidx = _first_index(g, gm)
        gval = 1.0 / functools.reduce(lambda a, b: a + b, [jnp.exp(gi - gm) for gi in g])
        le = []
        for e in range(epg):
            sel = lg_ref[MOE_GROUPS + (MOE_GROUPS - 1) * epg + e:MOE_GROUPS + (MOE_GROUPS - 1) * epg + e + 1, sl]
            for gg in range(MOE_GROUPS - 2, -1, -1):
                sel = jnp.where(gidx == gg, lg_ref[MOE_GROUPS + gg * epg + e:MOE_GROUPS + gg * epg + e + 1, sl], sel)
            le.append(sel)
        m1 = functools.reduce(jnp.maximum, le)
        i1 = _first_index(le, m1)
        le2 = [jnp.where(i1 == e, -jnp.inf, le[e]) for e in range(epg)]
        m2 = functools.reduce(jnp.maximum, le2)
        i2 = _first_index(le2, m2)
        t = jnp.exp(m2 - m1)
        inv = 1.0 / (1.0 + t)
        gate_ref[0:1, sl] = inv * gval
        gate_ref[1:2, sl] = t * inv * gval
        e0 = gidx * epg + i1
        e1 = gidx * epg + i2
        oh0 = eid == e0
        oh1 = eid == e1
        oh0f = jnp.where(oh0, 1.0, 0.0)
        oh1f = jnp.where(oh1, 1.0, 0.0)
        pre0 = jnp.dot(oh0f.astype(BF16), tri, preferred_element_type=F32)
        pre1 = jnp.dot(oh1f.astype(BF16), tri, preferred_element_type=F32)
        c0 = oh0f.sum(axis=1, keepdims=True)
        c1 = oh1f.sum(axis=1, keepdims=True)
        rank_s[0:1, sl] = jnp.where(oh0, run + pre0, 0.0).sum(axis=0, keepdims=True)
        rank_s[1:2, sl] = jnp.where(oh1, run + c0 + pre1, 0.0).sum(axis=0, keepdims=True)
        e_s[0:1, sl] = e0
        e_s[1:2, sl] = e1
        return run + c0 + c1

    counts = lax.fori_loop(0, n_tiles, pass1, jnp.zeros((MOE_EXPERTS, 1), F32))
    blocks = jnp.floor((counts + (MOE_ROWS - 1)) * (1.0 / MOE_ROWS))
    lincl = jnp.where(lax.broadcasted_iota(jnp.int32, (MOE_EXPERTS, MOE_EXPERTS), 1)
                      <= lax.broadcasted_iota(jnp.int32, (MOE_EXPERTS, MOE_EXPERTS), 0), 1.0, 0.0).astype(BF16)
    end_blocks = jnp.dot(lincl, jnp.broadcast_to(blocks, (MOE_EXPERTS, LANES)).astype(BF16),
                         preferred_element_type=F32)[:, 0:1]
    start_rows = (end_blocks - blocks) * MOE_ROWS

    def pass2(c, carry):
        sl = pl.ds(pl.multiple_of(c * tw, tw), tw)
        for k in range(MOE_TOP_K):
            base = jnp.where(eid == e_s[k:k + 1, sl], start_rows, 0.0).sum(axis=0, keepdims=True)
            pos_ref[k:k + 1, sl] = (base + rank_s[k:k + 1, sl]).astype(jnp.int32)
        return carry

    lax.fori_loop(0, n_tiles, pass2, 0)
    bl = lax.broadcasted_iota(jnp.int32, (MOE_EXPERTS, PLAN_LANES), 1).astype(F32)
    blk = jnp.where(end_blocks <= bl, 1.0, 0.0).sum(axis=0, keepdims=True)
    blk_ref[...] = jnp.minimum(blk, MOE_EXPERTS - 1.0).astype(jnp.int32)
    nused_ref[...] = jnp.broadcast_to(end_blocks[MOE_EXPERTS - 1:MOE_EXPERTS, :], (1, LANES)).astype(jnp.int32)


def _router(lg_t):
    n = lg_t.shape[1]
    assert n % ROUTE_TILE == 0
    return pl.pallas_call(
        functools.partial(_router_kernel, n_tiles=n // ROUTE_TILE),
        out_shape=[jax.ShapeDtypeStruct((MOE_TOP_K, n), F32),
                   jax.ShapeDtypeStruct((MOE_TOP_K, n), jnp.int32),
                   jax.ShapeDtypeStruct((1, PLAN_LANES), jnp.int32),
                   jax.ShapeDtypeStruct((1, LANES), jnp.int32)],
        scratch_shapes=[pltpu.VMEM((MOE_TOP_K, n), jnp.int32), pltpu.VMEM((MOE_TOP_K, n), F32)],
        compiler_params=pltpu.CompilerParams(vmem_limit_bytes=VMEM_LIMIT),
        name="moe_router_plan",
    )(lg_t)


def _dispatch_kernel(p0_ref, p1_ref, tok_ref, init_ref, xs_ref, sem, *, tm):
    del init_ref
    base = pl.program_id(0) * tm

    def copy(r, p_ref):
        return pltpu.make_async_copy(tok_ref.at[pl.ds(r, 1)], xs_ref.at[pl.ds(p_ref[base + r], 1)], sem)

    def start(r, carry):
        copy(r, p0_ref).start()
        copy(r, p1_ref).start()
        return carry

    def wait(r, carry):
        copy(r, p0_ref).wait()
        copy(r, p1_ref).wait()
        return carry

    lax.fori_loop(0, tm, start, 0, unroll=8)
    lax.fori_loop(0, tm, wait, 0, unroll=8)


def _dispatch(tok, pos0, pos1, n_slots, tm):
    nt, d = tok.shape
    grid_spec = pltpu.PrefetchScalarGridSpec(
        num_scalar_prefetch=2,
        grid=(nt // tm,),
        in_specs=[pl.BlockSpec((tm, d), lambda i, p0, p1: (i, 0)),
                  pl.BlockSpec(memory_space=pl.ANY)],
        out_specs=pl.BlockSpec(memory_space=pl.ANY),
        scratch_shapes=[pltpu.SemaphoreType.DMA(())],
    )
    return pl.pallas_call(
        functools.partial(_dispatch_kernel, tm=tm),
        grid_spec=grid_spec,
        out_shape=jax.ShapeDtypeStruct((n_slots, d), tok.dtype),
        input_output_aliases={3: 0},
        compiler_params=_params(("arbitrary",)),
        name="moe_dispatch",
    )(pos0, pos1, tok, jnp.zeros((n_slots, d), tok.dtype))


def _combine_kernel(p0_ref, p1_ref, x_ref, yb_ref, gt_ref, ga_ref, *rest, tm, final):
    if final:
        gf_ref, o_ref, y0_s, y1_s, sem = rest
    else:
        o_ref, y0_s, y1_s, sem = rest
    base = pl.program_id(0) * tm

    def copy(r, p_ref, y_s):
        return pltpu.make_async_copy(yb_ref.at[pl.ds(p_ref[base + r], 1)], y_s.at[pl.ds(r, 1)], sem)

    def start(r, carry):
        copy(r, p0_ref, y0_s).start()
        copy(r, p1_ref, y1_s).start()
        return carry

    def wait(r, carry):
        copy(r, p0_ref, y0_s).wait()
        copy(r, p1_ref, y1_s).wait()
        return carry

    lax.fori_loop(0, tm, start, 0, unroll=8)
    lax.fori_loop(0, tm, wait, 0, unroll=8)
    gt = gt_ref[...]
    f = gt[:, 0:1] * y0_s[...] + gt[:, 1:2] * y1_s[...]
    xn = x_ref[...] + ga_ref[...] * f
    if final:
        r = lax.rsqrt(jnp.mean(xn * xn, axis=-1, keepdims=True) + RMS_EPS)
        xn = xn * r * gf_ref[...]
    o_ref[...] = xn


def _combine(xs, yb, pos0, pos1, gate, mods, seg, tm, final_g=None, n_out_rows=None):
    nt, d = xs.shape
    rows = nt if n_out_rows is None else n_out_rows
    row_blk = lambda i, p0, p1: (i, 0)
    in_specs = [pl.BlockSpec((tm, d), row_blk),
                pl.BlockSpec(memory_space=pl.ANY),
                pl.BlockSpec((tm, MOE_TOP_K), row_blk),
                pl.BlockSpec((None, None, 1, d), lambda i, p0, p1: (seg(i), 5, 0, 0))]
    args = [xs, yb, gate, mods]
    if final_g is not None:
        in_specs.append(pl.BlockSpec((1, d), lambda i, p0, p1: (0, 0)))
        args.append(final_g.reshape(1, d))
    grid_spec = pltpu.PrefetchScalarGridSpec(
        num_scalar_prefetch=2,
        grid=(rows // tm,),
        in_specs=in_specs,
        out_specs=pl.BlockSpec((tm, d), row_blk),
        scratch_shapes=[pltpu.VMEM((tm, d), F32), pltpu.VMEM((tm, d), F32), pltpu.SemaphoreType.DMA(())],
    )
    return pl.pallas_call(
        functools.partial(_combine_kernel, tm=tm, final=final_g is not None),
        grid_spec=grid_spec,
        out_shape=jax.ShapeDtypeStruct((rows, d), F32),
        compiler_params=_params(("arbitrary",)),
        name="moe_combine_residual",
    )(pos0, pos1, *args)


def _rope_tables(seq, pad_rows):
    t = jnp.arange(seq, dtype=jnp.int32)
    row = (t // GRID_W).astype(F32)
    col = (t % GRID_W).astype(F32)
    inv_freq = ROPE_THETA ** (-jnp.arange(ROPE_AXIS_DIM // 2, dtype=F32) * 2.0 / ROPE_AXIS_DIM)
    ang = jnp.concatenate([row[:, None] * inv_freq, col[:, None] * inv_freq], axis=-1)
    cos, sin = jnp.cos(ang), jnp.sin(ang)
    cosd = jnp.repeat(cos, 2, axis=-1)
    sins = jnp.stack([-sin, sin], axis=-1).reshape(seq, HEAD_DIM)
    cosd = jnp.concatenate([cosd, jnp.ones((pad_rows, HEAD_DIM), F32)], axis=0)
    sins = jnp.concatenate([sins, jnp.zeros((pad_rows, HEAD_DIM), F32)], axis=0)
    return cosd, sins


def kernel(x, c, ctx, c_ctx, mod_w, mod_b, norm_mix_g, norm_ffn_g, norm_final_g, na_sg_w_in, na_sg_w_out, na_rpb, sg_w_s, sg_b_s, sg_ln_g, sg_ln_b, gqa_w_in, gqa_w_out, gqa_q_gain, gqa_k_gain, moe_w_group, moe_b_group, moe_w_expert, moe_b_expert, moe_w1, moe_w3, moe_w2):
    n_batch, seq, d = x.shape
    ctx_len = ctx.shape[1]
    depth = mod_w.shape[0]
    n_lat = n_batch * seq
    tm = min(ROW_TILE, ctx_len * n_batch)
    assert seq % tm == 0 and (n_batch * ctx_len) % tm == 0 and seq % ctx_len == 0
    assert ctx_len % GRID_W == 0 and n_batch + 1 <= 8
    tiles_per_batch = seq // tm
    seg = _seg_fn(tiles_per_batch, n_batch)
    n_q_heads = d // HEAD_DIM
    n_tok = n_lat + n_batch * ctx_len
    n_blocks = -(-n_tok * MOE_TOP_K // MOE_ROWS) + MOE_EXPERTS
    assert n_blocks <= PLAN_LANES and n_tok % ROUTE_TILE == 0

    xs = jnp.concatenate([x.reshape(n_lat, d), ctx.reshape(n_batch * ctx_len, d)], axis=0)
    cond = jnp.zeros((8, d), F32).at[:n_batch].set(c).at[n_batch].set(c_ctx)
    mod_all = _modulation(cond, mod_w, mod_b)
    cosd, sins = _rope_tables(seq, tm)

    q_w = n_q_heads * HEAD_DIM
    kv_w = GQA_KV_HEADS * HEAD_DIM
    even_colscale = jnp.concatenate([jnp.full((NA_WIDTH,), ATTN_SCALE * LOG2E, F32),
                                     jnp.ones((2 * NA_WIDTH + 2 * SG_WIDTH,), F32)])[None, :]
    out = None
    for layer in range(depth):
        last = layer == depth - 1
        mods = mod_all[layer, :n_batch + 1].reshape(n_batch + 1, 6, 1, d)
        i = layer // 2
        if layer % 2 == 0:
            qkv, uz = _in_proj_even(xs, norm_mix_g[layer], mods, na_sg_w_in[i].astype(BF16), even_colscale, seg, tm)
            att = _na_attention(qkv, _na_bias_table(na_rpb[i]), n_batch, seq, ctx_len)
            gat = _spatial_gating(uz, sg_w_s[i], sg_b_s[i], sg_ln_g[i], sg_ln_b[i], tm)
            lhs, w_out = [att, gat], na_sg_w_out[i].astype(BF16)
        else:
            gain = jnp.concatenate([jnp.tile(gqa_q_gain[i] * (ATTN_SCALE * LOG2E), n_q_heads),
                                    jnp.tile(gqa_k_gain[i], GQA_KV_HEADS),
                                    jnp.ones((kv_w,), F32)])[None, :]
            qkv = _in_proj_odd(xs, norm_mix_g[layer], mods, gqa_w_in[i].astype(BF16), gain, cosd, sins,
                               seg, tm, tiles_per_batch, n_lat // tm, q_w + kv_w)
            att = _flash_attention(qkv, n_batch, seq, ctx_len, n_q_heads)
            lhs, w_out = [att], gqa_w_out[i].astype(BF16)

        pad = ROUTER_W - MOE_GROUPS - MOE_EXPERTS
        w_router = jnp.concatenate([moe_w_group[layer], moe_w_expert[layer], jnp.zeros((d, pad), F32)],
                                   axis=1).T.astype(BF16)
        b_router = jnp.concatenate([moe_b_group[layer], moe_b_expert[layer], jnp.zeros((pad,), F32)])[:, None]
        xs, tok, lg_t = _out_proj(lhs, w_out, xs, mods, norm_ffn_g[layer], w_router, b_router, seg, tm)

        gate, pos, blk, nused = _router(lg_t)
        pos0, pos1 = pos[0], pos[1]
        slots = _dispatch(tok, pos0, pos1, n_blocks * MOE_ROWS, tm)
        yb = _experts(slots, blk[0, :n_blocks], nused[0, :1], moe_w1, moe_w3, moe_w2, layer)
        if last:
            out = _combine(xs, yb, pos0, pos1, gate.T, mods, seg, tm, final_g=norm_final_g, n_out_rows=n_lat)
        else:
            xs = _combine(xs, yb, pos0, pos1, gate.T, mods, seg, tm)
    return out.reshape(n_batch, seq, d)
```

```python
import functools

import jax
import jax.numpy as jnp
from jax import lax
from jax.experimental import pallas as pl
from jax.experimental.pallas import tpu as pltpu

F32 = jnp.float32
BF16 = jnp.bfloat16

GRID_W = 64
HEAD_DIM = 128
NA_HEADS = 8
NA_WIDTH = NA_HEADS * HEAD_DIM
NA_WIN_ROWS = 8
NA_WIN_COLS = 16
NA_BLOCK_ROWS = 4
NA_SPAN_ROWS = 12
SG_GROUPS = 8
SG_DIM = 128
SG_WIDTH = SG_GROUPS * SG_DIM
SG_CHUNK = 128
GQA_KV_HEADS = 4
ROPE_THETA = 10000.0
ROPE_AXIS_DIM = HEAD_DIM // 2
MOE_GROUPS = 4
MOE_EXPERTS_PER_GROUP = 8
MOE_EXPERTS = MOE_GROUPS * MOE_EXPERTS_PER_GROUP
MOE_TOP_K = 2
RMS_EPS = 1e-6
LN_EPS = 1e-5

LOG2E = 1.4426950408889634
MASK_VALUE = -1e30
ATTN_SCALE = HEAD_DIM ** -0.5
LANES = 128
MXU_N = 256
ROUTER_W = LANES
ROW_TILE = 512
MOE_ROWS = 256
ROUTE_TILE = 512
PLAN_LANES = 256
VMEM_LIMIT = 48 * 1024 * 1024
VMEM_LIMIT_BIG = 56 * 1024 * 1024


def _params(sem):
    return pltpu.CompilerParams(dimension_semantics=sem, vmem_limit_bytes=VMEM_LIMIT)


def _sigmoid(x):
    return 1.0 / (1.0 + jnp.exp(-x))


def _gelu_tanh(x):
    cdf = 0.5 * (1.0 + jnp.tanh(0.7978845608028654 * (x + 0.044715 * (x * x * x))))
    return x * cdf


def _rms_modulate(x, g, sc, sh):
    r = lax.rsqrt(jnp.mean(x * x, axis=-1, keepdims=True) + RMS_EPS)
    return (x * r * g) * (1.0 + sc) + sh


def _mod_kernel(c_ref, w_ref, b_ref, o_ref):
    c = c_ref[...]
    cs = (c * _sigmoid(c)).astype(BF16)
    o_ref[0] = jnp.dot(cs, w_ref[0].astype(BF16), preferred_element_type=F32) + b_ref[0]


def _modulation(cond, mod_w, mod_b):
    depth, d, n = mod_w.shape
    tn = 1024
    return pl.pallas_call(
        _mod_kernel,
        grid=(depth, n // tn),
        in_specs=[pl.BlockSpec((8, d), lambda l, j: (0, 0)),
                  pl.BlockSpec((1, d, tn), lambda l, j: (l, 0, j)),
                  pl.BlockSpec((1, 1, tn), lambda l, j: (l, 0, j))],
        out_specs=pl.BlockSpec((1, 8, tn), lambda l, j: (l, 0, j)),
        out_shape=jax.ShapeDtypeStruct((depth, 8, n), F32),
        compiler_params=_params(("parallel", "parallel")),
        name="adaln_modulation",
    )(cond, mod_w, mod_b.reshape(depth, 1, n))


def _seg_fn(tiles_per_batch, n_batch):
    return lambda i: jnp.minimum(i // tiles_per_batch, n_batch)


def _in_prologue(x_ref, g_ref, sc_ref, sh_ref, hx_ref):
    hx_ref[...] = _rms_modulate(x_ref[...], g_ref[...], sc_ref[...], sh_ref[...]).astype(BF16)


def _in_even_kernel(x_ref, g_ref, sc_ref, sh_ref, w_ref, wvt_ref, qk_ref, vt_ref, uz_ref, hx_ref, *, tn, q_width):
    _in_prologue(x_ref, g_ref, sc_ref, sh_ref, hx_ref)
    n_qk = qk_ref.shape[1]
    for j in range(n_qk // tn):
        cols = slice(j * tn, (j + 1) * tn)
        acc = jnp.dot(hx_ref[...], w_ref[:, cols], preferred_element_type=F32)
        if j * tn < q_width:
            acc = acc * (ATTN_SCALE * LOG2E)
        qk_ref[:, cols] = acc.astype(BF16)
    for j in range(uz_ref.shape[1] // tn):
        uz_ref[:, j * tn:(j + 1) * tn] = jnp.dot(hx_ref[...], w_ref[:, n_qk + j * tn:n_qk + (j + 1) * tn],
                                                 preferred_element_type=F32)
    for j in range(vt_ref.shape[0] // tn):
        vt_ref[j * tn:(j + 1) * tn, :] = lax.dot_general(
            wvt_ref[j * tn:(j + 1) * tn, :], hx_ref[...], (((1,), (1,)), ((), ())),
            preferred_element_type=F32).astype(BF16)


def _in_odd_kernel(x_ref, g_ref, sc_ref, sh_ref, w_ref, wvt_ref, cq_ref, sq_ref, ck_ref, sk_ref,
                   qk_ref, vt_ref, hx_ref, *, q_width):
    _in_prologue(x_ref, g_ref, sc_ref, sh_ref, hx_ref)
    for hp in range(qk_ref.shape[1] // MXU_N):
        c_ref, s_ref = (cq_ref, sq_ref) if hp * MXU_N < q_width else (ck_ref, sk_ref)
        y2 = jnp.dot(hx_ref[...], w_ref[:, hp * MXU_N:(hp + 1) * MXU_N], preferred_element_type=F32)
        for h in range(MXU_N // HEAD_DIM):
            y = y2[:, h * HEAD_DIM:(h + 1) * HEAD_DIM]
            cols = slice(hp * MXU_N + h * HEAD_DIM, hp * MXU_N + (h + 1) * HEAD_DIM)
            r = lax.rsqrt(jnp.mean(y * y, axis=-1, keepdims=True) + RMS_EPS)
            sw = pltpu.roll(y, HEAD_DIM // 2, 1)
            qk_ref[:, cols] = ((y * c_ref[...] + sw * s_ref[...]) * r).astype(BF16)
    vt_ref[...] = lax.dot_general(wvt_ref[...], hx_ref[...], (((1,), (1,)), ((), ())),
                                  preferred_element_type=F32).astype(BF16)


def _common_in_specs(tm, d, seg):
    return [pl.BlockSpec((tm, d), lambda i, *_: (i, 0)),
            pl.BlockSpec((1, d), lambda i, *_: (0, 0)),
            pl.BlockSpec((None, None, 1, d), lambda i, *_: (seg(i), 1, 0, 0)),
            pl.BlockSpec((None, None, 1, d), lambda i, *_: (seg(i), 0, 0, 0))]


def _in_proj_even(xs, g, mods, w, w_vt, seg, tm):
    nt, d = xs.shape
    n = w.shape[1]
    n_qk = 2 * NA_WIDTH
    v_w = w_vt.shape[0]
    resident = dict(pipeline_mode=pl.Buffered(1))
    return pl.pallas_call(
        functools.partial(_in_even_kernel, tn=512, q_width=NA_WIDTH),
        grid=(nt // tm,),
        in_specs=_common_in_specs(tm, d, seg) + [
            pl.BlockSpec((d, n), lambda i: (0, 0), **resident),
            pl.BlockSpec((v_w, d), lambda i: (0, 0), **resident)],
        out_specs=[pl.BlockSpec((tm, n_qk), lambda i: (i, 0)),
                   pl.BlockSpec((v_w, tm), lambda i: (0, i)),
                   pl.BlockSpec((tm, n - n_qk), lambda i: (i, 0))],
        out_shape=[jax.ShapeDtypeStruct((nt, n_qk), BF16),
                   jax.ShapeDtypeStruct((v_w, nt), BF16),
                   jax.ShapeDtypeStruct((nt, n - n_qk), F32)],
        scratch_shapes=[pltpu.VMEM((tm, d), BF16)],
        compiler_params=pltpu.CompilerParams(dimension_semantics=("parallel",),
                                             vmem_limit_bytes=VMEM_LIMIT_BIG),
        name="in_proj_even",
    )(xs, g.reshape(1, d), mods, mods, w, w_vt)


def _in_proj_odd(xs, g, mods, w_qk, w_vt, ctab, stab, seg, tm, tiles_per_batch, n_lat_tiles, q_width):
    nt, d = xs.shape
    n_qk = w_qk.shape[1]
    kv_w = w_vt.shape[0]
    rope_row = lambda i: jnp.where(i < n_lat_tiles, i % tiles_per_batch, tiles_per_batch)
    tab_spec = lambda which: pl.BlockSpec((None, tm, HEAD_DIM), lambda i: (which, rope_row(i), 0))
    resident = dict(pipeline_mode=pl.Buffered(1))
    return pl.pallas_call(
        functools.partial(_in_odd_kernel, q_width=q_width),
        grid=(nt // tm,),
        in_specs=_common_in_specs(tm, d, seg) + [
            pl.BlockSpec((d, n_qk), lambda i: (0, 0), **resident),
            pl.BlockSpec((kv_w, d), lambda i: (0, 0), **resident),
            tab_spec(0), tab_spec(0), tab_spec(1), tab_spec(1)],
        out_specs=[pl.BlockSpec((tm, n_qk), lambda i: (i, 0)),
                   pl.BlockSpec((kv_w, tm), lambda i: (0, i))],
        out_shape=[jax.ShapeDtypeStruct((nt, n_qk), BF16),
                   jax.ShapeDtypeStruct((kv_w, nt), BF16)],
        scratch_shapes=[pltpu.VMEM((tm, d), BF16)],
        compiler_params=_params(("parallel",)),
        name="in_proj_odd",
    )(xs, g.reshape(1, d), mods, mods, w_qk, w_vt, ctab, stab, ctab, stab)


def _na_kernel(q_ref, k_ref, vt_ref, qc_ref, kc_ref, vtc_ref, bias_ref, o_ref, oc_ref, *, n_rows, blocks_per_step):
    c = pl.program_id(2)
    bq = NA_BLOCK_ROWS * GRID_W
    span = NA_SPAN_ROWS * GRID_W
    n_blocks = n_rows // NA_BLOCK_ROWS
    dn = (((1,), (1,)), ((), ()))
    kc = kc_ref[...]
    vtc = vtc_ref[...]

    for j in range(blocks_per_step):
        blk = c * blocks_per_step + j
        kr_base = jnp.clip(blk * NA_BLOCK_ROWS - NA_WIN_ROWS // 2, 0, n_rows - NA_SPAN_ROWS)
        start = pl.multiple_of(kr_base * GRID_W, NA_BLOCK_ROWS * GRID_W)
        kind = jnp.where(blk == 0, 0, jnp.where(blk == n_blocks - 1, 2, 1))
        q = q_ref[j * bq:(j + 1) * bq, :]
        s_nb = lax.dot_general(k_ref[pl.ds(start, span), :], q, dn, preferred_element_type=F32) + bias_ref[kind]
        s_cx = lax.dot_general(kc, q, dn, preferred_element_type=F32)
        m = jnp.maximum(s_nb.max(axis=0, keepdims=True), s_cx.max(axis=0, keepdims=True))
        p_nb = jnp.exp2(s_nb - m)
        p_cx = jnp.exp2(s_cx - m)
        l = p_nb.sum(axis=0, keepdims=True) + p_cx.sum(axis=0, keepdims=True)
        o = (jnp.dot(vt_ref[:, pl.ds(start, span)], p_nb.astype(BF16), preferred_element_type=F32)
             + jnp.dot(vtc, p_cx.astype(BF16), preferred_element_type=F32))
        o_ref[j * bq:(j + 1) * bq, :] = (o / l).T.astype(BF16)

    @pl.when(c == pl.num_programs(2) - 1)
    def _():
        s = lax.dot_general(kc, qc_ref[...], dn, preferred_element_type=F32)
        p = jnp.exp2(s - s.max(axis=0, keepdims=True))
        l = p.sum(axis=0, keepdims=True)
        o = jnp.dot(vtc, p.astype(BF16), preferred_element_type=F32)
        oc_ref[...] = (o / l).T.astype(BF16)


def _na_bias_table(rpb, n_rows):
    n_heads = rpb.shape[0]
    qc = jnp.arange(GRID_W)[:, None]
    kc = jnp.arange(GRID_W)[None, :]
    kc0 = jnp.clip(qc - NA_WIN_COLS // 2, 0, GRID_W - NA_WIN_COLS)
    valid = (kc >= kc0) & (kc < kc0 + NA_WIN_COLS)
    off_c = kc - qc + (NA_WIN_COLS - 1)
    col = jnp.zeros((n_heads, 2 * NA_WIN_ROWS - 1, GRID_W, GRID_W), F32)
    for o in range(2 * NA_WIN_COLS - 1):
        col = col + jnp.where((off_c == o)[None, None], rpb[:, :, o, None, None].astype(F32), 0.0)
    col = jnp.where(valid[None, None], col * LOG2E, MASK_VALUE).transpose(0, 1, 3, 2)
    u = jnp.arange(NA_SPAN_ROWS)[:, None]
    t = jnp.arange(NA_BLOCK_ROWS)[None, :]
    tables = []
    for r0, kr_base in ((0, 0), (NA_BLOCK_ROWS, 0), (n_rows - NA_BLOCK_ROWS, n_rows - NA_SPAN_ROWS)):
        q_row = r0 + t
        win0 = jnp.clip(q_row - NA_WIN_ROWS // 2, 0, n_rows - NA_WIN_ROWS)
        k_row = kr_base + u
        in_win = (k_row >= win0) & (k_row < win0 + NA_WIN_ROWS)
        off_r = k_row - q_row + (NA_WIN_ROWS - 1)
        tb = jnp.zeros((n_heads, NA_SPAN_ROWS, GRID_W, NA_BLOCK_ROWS, GRID_W), F32)
        for o in range(2 * NA_WIN_ROWS - 1):
            sel = in_win & (off_r == o)
            tb = tb + jnp.where(sel[None, :, None, :, None], col[:, o][:, None, :, None, :], 0.0)
        tb = jnp.where(in_win[None, :, None, :, None], tb, MASK_VALUE)
        tables.append(tb.reshape(n_heads, NA_SPAN_ROWS * GRID_W, NA_BLOCK_ROWS * GRID_W))
    return jnp.stack(tables)


def _na_attention(qk, vt, bias_tbl, n_batch, seq, ctx_len):
    n_rows = seq // GRID_W
    bq = NA_BLOCK_ROWS * GRID_W
    span = NA_SPAN_ROWS * GRID_W
    step_rows = min(2048, seq)
    assert ctx_len == bq and n_rows % NA_BLOCK_ROWS == 0 and n_rows >= NA_SPAN_ROWS + NA_BLOCK_ROWS
    steps = seq // step_rows
    ctx_blk0 = n_batch * seq // ctx_len
    h_k = NA_HEADS
    return pl.pallas_call(
        functools.partial(_na_kernel, n_rows=n_rows, blocks_per_step=step_rows // bq),
        grid=(n_batch, NA_HEADS, steps),
        in_specs=[pl.BlockSpec((step_rows, HEAD_DIM), lambda b, h, c: (b * steps + c, h)),
                  pl.BlockSpec((seq, HEAD_DIM), lambda b, h, c: (b, h_k + h)),
                  pl.BlockSpec((HEAD_DIM, seq), lambda b, h, c: (h, b)),
                  pl.BlockSpec((ctx_len, HEAD_DIM), lambda b, h, c: (ctx_blk0 + b, h)),
                  pl.BlockSpec((ctx_len, HEAD_DIM), lambda b, h, c: (ctx_blk0 + b, h_k + h)),
                  pl.BlockSpec((HEAD_DIM, ctx_len), lambda b, h, c: (h, ctx_blk0 + b)),
                  pl.BlockSpec((3, None, span, bq), lambda b, h, c: (0, h, 0, 0))],
        out_specs=[pl.BlockSpec((step_rows, HEAD_DIM), lambda b, h, c: (b * steps + c, h)),
                   pl.BlockSpec((ctx_len, HEAD_DIM), lambda b, h, c: (b, h))],
        out_shape=[jax.ShapeDtypeStruct((n_batch * seq, NA_WIDTH), BF16),
                   jax.ShapeDtypeStruct((n_batch * ctx_len, NA_WIDTH), BF16)],
        compiler_params=_params(("parallel", "parallel", "arbitrary")),
        name="neighbourhood_attention",
    )(qk, qk, vt, qk, qk, vt, bias_tbl)


def _sg_kernel(uz_ref, ws_ref, bs_ref, lg_ref, lb_ref, o_ref, *, n_chunks):
    for g in range(SG_GROUPS):
        w = ws_ref[g].astype(BF16)
        for n in range(n_chunks):
            rows = slice(n * SG_CHUNK, (n + 1) * SG_CHUNK)
            u = uz_ref[rows, g * SG_DIM:(g + 1) * SG_DIM]
            z = uz_ref[rows, SG_WIDTH + g * SG_DIM:SG_WIDTH + (g + 1) * SG_DIM]
            zz = _gelu_tanh(z)
            mu = jnp.mean(zz, axis=-1, keepdims=True)
            xc = zz - mu
            var = jnp.mean(xc * xc, axis=-1, keepdims=True)
            zn = xc * lax.rsqrt(var + LN_EPS) * lg_ref[g] + lb_ref[g]
            mixed = jnp.dot(w, zn.astype(BF16), preferred_element_type=F32) + bs_ref[g]
            o_ref[rows, g * SG_DIM:(g + 1) * SG_DIM] = (_gelu_tanh(u) * mixed).astype(BF16)


def _spatial_gating(uz, w_s, b_s, ln_g, ln_b, tm):
    nt = uz.shape[0]
    full3 = lambda i: (0, 0, 0)
    return pl.pallas_call(
        functools.partial(_sg_kernel, n_chunks=tm // SG_CHUNK),
        grid=(nt // tm,),
        in_specs=[pl.BlockSpec((tm, 2 * SG_WIDTH), lambda i: (i, 0)),
                  pl.BlockSpec((SG_GROUPS, SG_CHUNK, SG_CHUNK), full3),
                  pl.BlockSpec((SG_GROUPS, SG_CHUNK, 1), full3),
                  pl.BlockSpec((SG_GROUPS, 1, SG_DIM), full3),
                  pl.BlockSpec((SG_GROUPS, 1, SG_DIM), full3)],
        out_specs=pl.BlockSpec((tm, SG_WIDTH), lambda i: (i, 0)),
        out_shape=jax.ShapeDtypeStruct((nt, SG_WIDTH), BF16),
        compiler_params=_params(("parallel",)),
        name="spatial_gating",
    )(uz, w_s, b_s.reshape(SG_GROUPS, SG_CHUNK, 1), ln_g.reshape(SG_GROUPS, 1, SG_DIM),
      ln_b.reshape(SG_GROUPS, 1, SG_DIM))


def _flash_kernel(q_ref, k_ref, vt_ref, kc_ref, vtc_ref, o_ref, q_s, *, tq, tk, seq, group, n_lat_q):
    qi = pl.program_id(2)
    for g in range(group):
        q_s[g * tq:(g + 1) * tq, :] = q_ref[:, g * HEAD_DIM:(g + 1) * HEAD_DIM]
    q = q_s[...]
    m_rows = group * tq
    dn = (((1,), (1,)), ((), ()))

    def step(k, vt, carry):
        m, l, acc = carry
        s = lax.dot_general(k, q, dn, preferred_element_type=F32)
        m_new = jnp.maximum(m, s.max(axis=0, keepdims=True))
        alpha = jnp.exp2(m - m_new)
        p = jnp.exp2(s - m_new)
        l = alpha * l + p.sum(axis=0, keepdims=True)
        acc = alpha * acc + jnp.dot(vt, p.astype(BF16), preferred_element_type=F32)
        return m_new, l, acc

    init = (jnp.full((1, m_rows), MASK_VALUE, F32), jnp.zeros((1, m_rows), F32),
            jnp.zeros((HEAD_DIM, m_rows), F32))

    def finish(carry):
        _, l, acc = carry
        o = acc / l
        for g in range(group):
            o_ref[:, g * HEAD_DIM:(g + 1) * HEAD_DIM] = o[:, g * tq:(g + 1) * tq].T.astype(BF16)

    @pl.when(qi < n_lat_q)
    def _():
        def body(t, carry):
            start = pl.multiple_of(t * tk, tk)
            return step(k_ref[pl.ds(start, tk), :], vt_ref[:, pl.ds(start, tk)], carry)

        carry = lax.fori_loop(0, seq // tk, body, init, unroll=4)
        finish(step(kc_ref[...], vtc_ref[...], carry))

    @pl.when(qi == n_lat_q)
    def _():
        finish(step(kc_ref[...], vtc_ref[...], init))


def _flash_attention(qk, vt, n_batch, seq, ctx_len, n_q_heads):
    nt = qk.shape[0]
    group = n_q_heads // GQA_KV_HEADS
    tq = ctx_len
    tk = min(1024, seq)
    n_lat_q = seq // tq
    ctx_blk0 = n_batch * seq // tq
    hk0 = n_q_heads
    gw = group * HEAD_DIM

    def q_idx(b, kh, qi):
        return (jnp.where(qi < n_lat_q, b * n_lat_q + qi, ctx_blk0 + b), kh)

    return pl.pallas_call(
        functools.partial(_flash_kernel, tq=tq, tk=tk, seq=seq, group=group, n_lat_q=n_lat_q),
        grid=(n_batch, GQA_KV_HEADS, n_lat_q + 1),
        in_specs=[pl.BlockSpec((tq, gw), q_idx),
                  pl.BlockSpec((seq, HEAD_DIM), lambda b, kh, qi: (b, hk0 + kh)),
                  pl.BlockSpec((HEAD_DIM, seq), lambda b, kh, qi: (kh, b)),
                  pl.BlockSpec((tq, HEAD_DIM), lambda b, kh, qi: (ctx_blk0 + b, hk0 + kh)),
                  pl.BlockSpec((HEAD_DIM, tq), lambda b, kh, qi: (kh, ctx_blk0 + b))],
        out_specs=pl.BlockSpec((tq, gw), q_idx),
        out_shape=jax.ShapeDtypeStruct((nt, n_q_heads * HEAD_DIM), BF16),
        scratch_shapes=[pltpu.VMEM((group * tq, HEAD_DIM), BF16)],
        compiler_params=_params(("parallel", "parallel", "arbitrary")),
        name="gqa_flash_attention",
    )(qk, qk, vt, qk, vt)


def _out_kernel(*refs, n_lhs):
    lhs_refs = refs[:n_lhs]
    (w_ref, x_ref, ga_ref, g2_ref, sc_ref, sh_ref, wr_ref, br_ref, xo_ref, tok_ref, lg_ref) = refs[n_lhs:]
    acc = None
    k0 = 0
    for a_ref in lhs_refs:
        kp = a_ref.shape[1]
        part = jnp.dot(a_ref[...], w_ref[k0:k0 + kp, :], preferred_element_type=F32)
        acc = part if acc is None else acc + part
        k0 += kp
    xn = x_ref[...] + ga_ref[...] * acc
    xo_ref[...] = xn
    tok = _rms_modulate(xn, g2_ref[...], sc_ref[...], sh_ref[...])
    tok_ref[...] = tok
    lg_ref[...] = lax.dot_general(wr_ref[...], tok.astype(BF16), (((1,), (1,)), ((), ())),
                                  preferred_element_type=F32) + br_ref[...]


def _out_proj(lhs, w, xs, mods, g2, w_router, b_router, seg, tm):
    nt, d = xs.shape
    k = w.shape[0]
    modspec = lambda which: pl.BlockSpec((None, None, 1, d), lambda i: (seg(i), which, 0, 0))
    return pl.pallas_call(
        functools.partial(_out_kernel, n_lhs=len(lhs)),
        grid=(nt // tm,),
        in_specs=[pl.BlockSpec((tm, a.shape[1]), lambda i: (i, 0)) for a in lhs] + [
            pl.BlockSpec((k, d), lambda i: (0, 0)),
            pl.BlockSpec((tm, d), lambda i: (i, 0)),
            modspec(2),
            pl.BlockSpec((1, d), lambda i: (0, 0)),
            modspec(4),
            modspec(3),
            pl.BlockSpec((ROUTER_W, d), lambda i: (0, 0)),
            pl.BlockSpec((ROUTER_W, 1), lambda i: (0, 0))],
        out_specs=[pl.BlockSpec((tm, d), lambda i: (i, 0)),
                   pl.BlockSpec((tm, d), lambda i: (i, 0)),
                   pl.BlockSpec((ROUTER_W, tm), lambda i: (0, i))],
        out_shape=[jax.ShapeDtypeStruct((nt, d), F32),
                   jax.ShapeDtypeStruct((nt, d), F32),
                   jax.ShapeDtypeStruct((ROUTER_W, nt), F32)],
        compiler_params=_params(("parallel",)),
        name="out_proj_residual_router",
    )(*lhs, w, xs, mods, g2.reshape(1, d), mods, mods, w_router, b_router)


def _expert_kernel(blk_e_ref, n_used_ref, xs_ref, w1_ref, w3_ref, w2_ref, o_ref):
    b = pl.program_id(0)

    @pl.when(b < n_used_ref[0])
    def _():
        x = xs_ref[...].astype(BF16)
        h1 = jnp.dot(x, w1_ref[...].astype(BF16), preferred_element_type=F32)
        h3 = jnp.dot(x, w3_ref[...].astype(BF16), preferred_element_type=F32)
        a = (h1 * _sigmoid(h1) * h3).astype(BF16)
        o_ref[...] = jnp.dot(a, w2_ref[...].astype(BF16), preferred_element_type=F32)

    @pl.when(b >= n_used_ref[0])
    def _():
        o_ref[...] = jnp.zeros_like(o_ref)


def _experts(xs, blk_e, n_used, w1, w3, w2, layer):
    p, d = xs.shape
    hid = w1.shape[-1]
    nb = p // MOE_ROWS
    grid_spec = pltpu.PrefetchScalarGridSpec(
        num_scalar_prefetch=2,
        grid=(nb,),
        in_specs=[pl.BlockSpec((MOE_ROWS, d), lambda b, be, nu: (jnp.minimum(b, nu[0] - 1), 0)),
                  pl.BlockSpec((None, None, d, hid), lambda b, be, nu: (layer, be[b], 0, 0)),
                  pl.BlockSpec((None, None, d, hid), lambda b, be, nu: (layer, be[b], 0, 0)),
                  pl.BlockSpec((None, None, hid, d), lambda b, be, nu: (layer, be[b], 0, 0))],
        out_specs=pl.BlockSpec((MOE_ROWS, d), lambda b, be, nu: (b, 0)),
    )
    return pl.pallas_call(
        _expert_kernel,
        grid_spec=grid_spec,
        out_shape=jax.ShapeDtypeStruct((p, d), F32),
        compiler_params=_params(("arbitrary",)),
        name="moe_experts",
    )(blk_e, n_used, xs, w1, w3, w2)


def _first_index(vals, target):
    idx = jnp.full(target.shape, len(vals) - 1, jnp.int32)
    for i in range(len(vals) - 2, -1, -1):
        idx = jnp.where(vals[i] == target, i, idx)
    return idx


def _router_kernel(lg_ref, gate_ref, pos_ref, blk_ref, nused_ref, e_s, rank_s, *, n_tiles):
    tw = ROUTE_TILE
    tri = jnp.where(lax.broadcasted_iota(jnp.int32, (tw, tw), 0) < lax.broadcasted_iota(jnp.int32, (tw, tw), 1),
                    1.0, 0.0).astype(BF16)
    eid = lax.broadcasted_iota(jnp.int32, (MOE_EXPERTS, tw), 0)
    epg = MOE_EXPERTS_PER_GROUP

    def pass1(c, run):
        sl = pl.ds(pl.multiple_of(c * tw, tw), tw)
        g = [lg_ref[i:i + 1, sl] for i in range(MOE_GROUPS)]
        gm = functools.reduce(jnp.maximum, g)
        gidx = _first_index(g, gm)
        gval = 1.0 / functools.reduce(lambda a, b: a + b, [jnp.exp(gi - gm) for gi in g])
        le = []
        for e in range(epg):
            sel = lg_ref[MOE_GROUPS + (MOE_GROUPS - 1) * epg + e:MOE_GROUPS + (MOE_GROUPS - 1) * epg + e + 1, sl]
            for gg in range(MOE_GROUPS - 2, -1, -1):
                sel = jnp.where(gidx == gg, lg_ref[MOE_GROUPS + gg * epg + e:MOE_GROUPS + gg * epg + e + 1, sl], sel)
            le.append(sel)
        m1 = functools.reduce(jnp.maximum, le)
        i1 = _first_index(le, m1)
        le2 = [jnp.where(i1 == e, -jnp.inf, le[e]) for e in range(epg)]
        m2 = functools.reduce(jnp.maximum, le2)
        i2 = _first_index(le2, m2)
        t = jnp.exp(m2 - m1)
        inv = 1.0 / (1.0 + t)
        gate_ref[0:1, sl] = inv * gval
        gate_ref[1:2, sl] = t * inv * gval
        e0 = gidx * epg + i1
        e1 = gidx * epg + i2
        oh0 = eid == e0
        oh1 = eid == e1
        oh0f = jnp.where(oh0, 1.0, 0.0)
        oh1f = jnp.where(oh1, 1.0, 0.0)
        pre0 = jnp.dot(oh0f.astype(BF16), tri, preferred_element_type=F32)
        pre1 = jnp.dot(oh1f.astype(BF16), tri, preferred_element_type=F32)
        c0 = oh0f.sum(axis=1, keepdims=True)
        c1 = oh1f.sum(axis=1, keepdims=True)
        rank_s[0:1, sl] = jnp.where(oh0, run + pre0, 0.0).sum(axis=0, keepdims=True)
        rank_s[1:2, sl] = jnp.where(oh1, run + c0 + pre1, 0.0).sum(axis=0, keepdims=True)
        e_s[0:1, sl] = e0
        e_s[1:2, sl] = e1
        return run + c0 + c1

    counts = lax.fori_loop(0, n_tiles, pass1, jnp.zeros((MOE_EXPERTS, 1), F32))
    blocks = jnp.floor((counts + (MOE_ROWS - 1)) * (1.0 / MOE_ROWS))
    lincl = jnp.where(lax.broadcasted_iota(jnp.int32, (MOE_EXPERTS, MOE_EXPERTS), 1)
                      <= lax.broadcasted_iota(jnp.int32, (MOE_EXPERTS, MOE_EXPERTS), 0), 1.0, 0.0).astype(BF16)
    end_blocks = jnp.dot(lincl, jnp.broadcast_to(blocks, (MOE_EXPERTS, LANES)).astype(BF16),
                         preferred_element_type=F32)[:, 0:1]
    start_rows = (end_blocks - blocks) * MOE_ROWS

    def pass2(c, carry):
        sl = pl.ds(pl.multiple_of(c * tw, tw), tw)
        for k in range(MOE_TOP_K):
            base = jnp.where(eid == e_s[k:k + 1, sl], start_rows, 0.0).sum(axis=0, keepdims=True)
            pos_ref[k:k + 1, sl] = (base + rank_s[k:k + 1, sl]).astype(jnp.int32)
        return carry

    lax.fori_loop(0, n_tiles, pass2, 0)
    bl = lax.broadcasted_iota(jnp.int32, (MOE_EXPERTS, PLAN_LANES), 1).astype(F32)
    blk = jnp.where(end_blocks <= bl, 1.0, 0.0).sum(axis=0, keepdims=True)
    blk_ref[...] = jnp.minimum(blk, MOE_EXPERTS - 1.0).astype(jnp.int32)
    nused_ref[...] = jnp.broadcast_to(end_blocks[MOE_EXPERTS - 1:MOE_EXPERTS, :], (1, LANES)).astype(jnp.int32)


def _router(lg_t):
    n = lg_t.shape[1]
    assert n % ROUTE_TILE == 0
    return pl.pallas_call(
        functools.partial(_router_kernel, n_tiles=n // ROUTE_TILE),
        out_shape=[jax.ShapeDtypeStruct((MOE_TOP_K, n), F32),
                   jax.ShapeDtypeStruct((MOE_TOP_K, n), jnp.int32),
                   jax.ShapeDtypeStruct((1, PLAN_LANES), jnp.int32),
                   jax.ShapeDtypeStruct((1, LANES), jnp.int32)],
        scratch_shapes=[pltpu.VMEM((MOE_TOP_K, n), jnp.int32), pltpu.VMEM((MOE_TOP_K, n), F32)],
        compiler_params=pltpu.CompilerParams(vmem_limit_bytes=VMEM_LIMIT),
        name="moe_router_plan",
    )(lg_t)


def _dispatch_kernel(p0_ref, p1_ref, tok_ref, init_ref, xs_ref, sem, *, tm):
    del init_ref
    base = pl.program_id(0) * tm

    def copy(r, p_ref):
        return pltpu.make_async_copy(tok_ref.at[pl.ds(r, 1)], xs_ref.at[pl.ds(p_ref[base + r], 1)], sem)

    def start(r, carry):
        copy(r, p0_ref).start()
        copy(r, p1_ref).start()
        return carry

    def wait(r, carry):
        copy(r, p0_ref).wait()
        copy(r, p1_ref).wait()
        return carry

    lax.fori_loop(0, tm, start, 0, unroll=8)
    lax.fori_loop(0, tm, wait, 0, unroll=8)


def _dispatch(tok, pos0, pos1, n_slots, tm):
    nt, d = tok.shape
    grid_spec = pltpu.PrefetchScalarGridSpec(
        num_scalar_prefetch=2,
        grid=(nt // tm,),
        in_specs=[pl.BlockSpec((tm, d), lambda i, p0, p1: (i, 0)),
                  pl.BlockSpec(memory_space=pl.ANY)],
        out_specs=pl.BlockSpec(memory_space=pl.ANY),
        scratch_shapes=[pltpu.SemaphoreType.DMA(())],
    )
    return pl.pallas_call(
        functools.partial(_dispatch_kernel, tm=tm),
        grid_spec=grid_spec,
        out_shape=jax.ShapeDtypeStruct((n_slots, d), tok.dtype),
        input_output_aliases={3: 0},
        compiler_params=_params(("arbitrary",)),
        name="moe_dispatch",
    )(pos0, pos1, tok, jnp.zeros((n_slots, d), tok.dtype))


def _combine_kernel(p0_ref, p1_ref, x_ref, yb_ref, gt_ref, ga_ref, *rest, tm, final):
    if final:
        gf_ref, o_ref, y0_s, y1_s, sem = rest
    else:
        o_ref, y0_s, y1_s, sem = rest
    base = pl.program_id(0) * tm

    def copy(r, p_ref, y_s):
        return pltpu.make_async_copy(yb_ref.at[pl.ds(p_ref[base + r], 1)], y_s.at[pl.ds(r, 1)], sem)

    def start(r, carry):
        copy(r, p0_ref, y0_s).start()
        copy(r, p1_ref, y1_s).start()
        return carry

    def wait(r, carry):
        copy(r, p0_ref, y0_s).wait()
        copy(r, p1_ref, y1_s).wait()
        return carry

    lax.fori_loop(0, tm, start, 0, unroll=8)
    lax.fori_loop(0, tm, wait, 0, unroll=8)
    gt = gt_ref[...]
    f = gt[:, 0:1] * y0_s[...] + gt[:, 1:2] * y1_s[...]
    xn = x_ref[...] + ga_ref[...] * f
    if final:
        r = lax.rsqrt(jnp.mean(xn * xn, axis=-1, keepdims=True) + RMS_EPS)
        xn = xn * r * gf_ref[...]
    o_ref[...] = xn


def _combine(xs, yb, pos0, pos1, gate, mods, seg, tm, final_g=None, n_out_rows=None):
    nt, d = xs.shape
    rows = nt if n_out_rows is None else n_out_rows
    row_blk = lambda i, p0, p1: (i, 0)
    in_specs = [pl.BlockSpec((tm, d), row_blk),
                pl.BlockSpec(memory_space=pl.ANY),
                pl.BlockSpec((tm, MOE_TOP_K), row_blk),
                pl.BlockSpec((None, None, 1, d), lambda i, p0, p1: (seg(i), 5, 0, 0))]
    args = [xs, yb, gate, mods]
    if final_g is not None:
        in_specs.append(pl.BlockSpec((1, d), lambda i, p0, p1: (0, 0)))
        args.append(final_g.reshape(1, d))
    grid_spec = pltpu.PrefetchScalarGridSpec(
        num_scalar_prefetch=2,
        grid=(rows // tm,),
        in_specs=in_specs,
        out_specs=pl.BlockSpec((tm, d), row_blk),
        scratch_shapes=[pltpu.VMEM((tm, d), F32), pltpu.VMEM((tm, d), F32), pltpu.SemaphoreType.DMA(())],
    )
    return pl.pallas_call(
        functools.partial(_combine_kernel, tm=tm, final=final_g is not None),
        grid_spec=grid_spec,
        out_shape=jax.ShapeDtypeStruct((rows, d), F32),
        compiler_params=_params(("arbitrary",)),
        name="moe_combine_residual",
    )(pos0, pos1, *args)


def _rope_tables(seq, pad_rows):
    t = jnp.arange(seq, dtype=jnp.int32)
    row = (t // GRID_W).astype(F32)
    col = (t % GRID_W).astype(F32)
    inv_freq = ROPE_THETA ** (-jnp.arange(ROPE_AXIS_DIM // 2, dtype=F32) * 2.0 / ROPE_AXIS_DIM)
    ang = jnp.concatenate([row[:, None] * inv_freq, col[:, None] * inv_freq], axis=-1)
    cos, sin = jnp.cos(ang), jnp.sin(ang)
    cos = jnp.concatenate([cos, jnp.ones((pad_rows, HEAD_DIM // 2), F32)], axis=0)
    sin = jnp.concatenate([sin, jnp.zeros((pad_rows, HEAD_DIM // 2), F32)], axis=0)
    return cos, sin


def _split_pairs(w, n_heads):
    d = w.shape[0]
    return w.reshape(d, n_heads, HEAD_DIM // 2, 2).transpose(0, 1, 3, 2).reshape(d, n_heads * HEAD_DIM)


def _gain_rope_tables(cos, sin, q_gain, k_gain):
    def tables(g):
        ge, go = g[0::2][None, :], g[1::2][None, :]
        return (jnp.concatenate([ge * cos, go * cos], axis=1),
                jnp.concatenate([-go * sin, ge * sin], axis=1))
    cq, sq = tables(q_gain)
    ck, sk = tables(k_gain)
    return jnp.stack([cq, ck]), jnp.stack([sq, sk])


def kernel(x, c, ctx, c_ctx, mod_w, mod_b, norm_mix_g, norm_ffn_g, norm_final_g, na_sg_w_in, na_sg_w_out, na_rpb, sg_w_s, sg_b_s, sg_ln_g, sg_ln_b, gqa_w_in, gqa_w_out, gqa_q_gain, gqa_k_gain, moe_w_group, moe_b_group, moe_w_expert, moe_b_expert, moe_w1, moe_w3, moe_w2):
    n_batch, seq, d = x.shape
    ctx_len = ctx.shape[1]
    depth = mod_w.shape[0]
    n_lat = n_batch * seq
    tm = min(ROW_TILE, ctx_len * n_batch)
    assert seq % tm == 0 and (n_batch * ctx_len) % tm == 0 and seq % ctx_len == 0
    assert ctx_len % GRID_W == 0 and n_batch + 1 <= 8
    tiles_per_batch = seq // tm
    seg = _seg_fn(tiles_per_batch, n_batch)
    n_q_heads = d // HEAD_DIM
    n_tok = n_lat + n_batch * ctx_len
    n_blocks = -(-n_tok * MOE_TOP_K // MOE_ROWS) + MOE_EXPERTS
    assert n_blocks <= PLAN_LANES and n_tok % ROUTE_TILE == 0

    xs = jnp.concatenate([x.reshape(n_lat, d), ctx.reshape(n_batch * ctx_len, d)], axis=0)
    cond = jnp.zeros((8, d), F32).at[:n_batch].set(c).at[n_batch].set(c_ctx)
    mod_all = _modulation(cond, mod_w, mod_b)
    cos, sin = _rope_tables(seq, tm)

    q_w = n_q_heads * HEAD_DIM
    kv_w = GQA_KV_HEADS * HEAD_DIM
    out = None
    for layer in range(depth):
        last = layer == depth - 1
        mods = mod_all[layer, :n_batch + 1].reshape(n_batch + 1, 6, 1, d)
        i = layer // 2
        if layer % 2 == 0:
            w_in = na_sg_w_in[i]
            w_main = jnp.concatenate([w_in[:, :2 * NA_WIDTH], w_in[:, 3 * NA_WIDTH:]], axis=1).astype(BF16)
            w_vt = w_in[:, 2 * NA_WIDTH:3 * NA_WIDTH].T.astype(BF16)
            qk, vt, uz = _in_proj_even(xs, norm_mix_g[layer], mods, w_main, w_vt, seg, tm)
            att_lat, att_ctx = _na_attention(qk, vt, _na_bias_table(na_rpb[i], seq // GRID_W), n_batch, seq, ctx_len)
            att = jnp.concatenate([att_lat, att_ctx], axis=0)
            gat = _spatial_gating(uz, sg_w_s[i], sg_b_s[i], sg_ln_g[i], sg_ln_b[i], tm)
            lhs, w_out = [att, gat], na_sg_w_out[i].astype(BF16)
        else:
            ctab, stab = _gain_rope_tables(cos, sin, gqa_q_gain[i] * (ATTN_SCALE * LOG2E), gqa_k_gain[i])
            w_qk = _split_pairs(gqa_w_in[i][:, :q_w + kv_w], n_q_heads + GQA_KV_HEADS).astype(BF16)
            w_vt = gqa_w_in[i][:, q_w + kv_w:].T.astype(BF16)
            qk, vt = _in_proj_odd(xs, norm_mix_g[layer], mods, w_qk, w_vt, ctab, stab,
                                  seg, tm, tiles_per_batch, n_lat // tm, q_w)
            att = _flash_attention(qk, vt, n_batch, seq, ctx_len, n_q_heads)
            lhs, w_out = [att], gqa_w_out[i].astype(BF16)

        pad = ROUTER_W - MOE_GROUPS - MOE_EXPERTS
        w_router = jnp.concatenate([moe_w_group[layer], moe_w_expert[layer], jnp.zeros((d, pad), F32)],
                                   axis=1).T.astype(BF16)
        b_router = jnp.concatenate([moe_b_group[layer], moe_b_expert[layer], jnp.zeros((pad,), F32)])[:, None]
        xs, tok, lg_t = _out_proj(lhs, w_out, xs, mods, norm_ffn_g[layer], w_router, b_router, seg, tm)

        gate, pos, blk, nused = _router(lg_t)
        pos0, pos1 = pos[0], pos[1]
        slots = _dispatch(tok, pos0, pos1, n_blocks * MOE_ROWS, tm)
        yb = _experts(slots, blk[0, :n_blocks], nused[0, :1], moe_w1, moe_w3, moe_w2, layer)
        if last:
            out = _combine(xs, yb, pos0, pos1, gate.T, mods, seg, tm, final_g=norm_final_g, n_out_rows=n_lat)
        else:
            xs = _combine(xs, yb, pos0, pos1, gate.T, mods, seg, tm)
    return out.reshape(n_batch, seq, d)
```

```python
import functools

import jax
import jax.numpy as jnp
from jax import lax
from jax.experimental import pallas as pl
from jax.experimental.pallas import tpu as pltpu

F32 = jnp.float32
BF16 = jnp.bfloat16

GRID_W = 64
HEAD_DIM = 128
NA_HEADS = 8
NA_WIDTH = NA_HEADS * HEAD_DIM
NA_WIN_ROWS = 8
NA_WIN_COLS = 16
NA_BLOCK_ROWS = 4
NA_SPAN_ROWS = 12
SG_GROUPS = 8
SG_DIM = 128
SG_WIDTH = SG_GROUPS * SG_DIM
SG_CHUNK = 128
GQA_KV_HEADS = 4
ROPE_THETA = 10000.0
ROPE_AXIS_DIM = HEAD_DIM // 2
MOE_GROUPS = 4
MOE_EXPERTS_PER_GROUP = 8
MOE_EXPERTS = MOE_GROUPS * MOE_EXPERTS_PER_GROUP
MOE_TOP_K = 2
RMS_EPS = 1e-6
LN_EPS = 1e-5

LOG2E = 1.4426950408889634
MASK_VALUE = -1e30
ATTN_SCALE = HEAD_DIM ** -0.5
LANES = 128
MXU_N = 256
ROUTER_W = LANES
ROW_TILE = 512
MOE_ROWS = 256
FLASH_KEY_TILE = 1024
ROUTE_TILE = 512
PLAN_LANES = 256
VMEM_LIMIT = 48 * 1024 * 1024
VMEM_LIMIT_BIG = 56 * 1024 * 1024


def _params(sem):
    return pltpu.CompilerParams(dimension_semantics=sem, vmem_limit_bytes=VMEM_LIMIT)


def _sigmoid(x):
    return 1.0 / (1.0 + jnp.exp(-x))


def _gelu_tanh(x):
    cdf = 0.5 * (1.0 + jnp.tanh(0.7978845608028654 * (x + 0.044715 * (x * x * x))))
    return x * cdf


def _pack_bf16_pairs(x):
    c = x.shape[1] // 2
    bits = pltpu.bitcast(x.astype(jnp.bfloat16).astype(F32), jnp.uint32)
    return (bits[:, :c] >> 16) | bits[:, c:]


def _unpack_bf16_pairs(w):
    lo = pltpu.bitcast(w << 16, F32)
    hi = pltpu.bitcast(w & jnp.uint32(0xFFFF0000), F32)
    return jnp.concatenate([lo, hi], axis=1)


def _rms_modulate(x, g, sc, sh):
    r = lax.rsqrt(jnp.mean(x * x, axis=-1, keepdims=True) + RMS_EPS)
    return (x * r * g) * (1.0 + sc) + sh


def _mod_kernel(c_ref, w_ref, b_ref, o_ref):
    c = c_ref[...]
    cs = (c * _sigmoid(c)).astype(BF16)
    o_ref[0] = jnp.dot(cs, w_ref[0].astype(BF16), preferred_element_type=F32) + b_ref[0]


def _modulation(cond, mod_w, mod_b):
    depth, d, n = mod_w.shape
    tn = 1024
    return pl.pallas_call(
        _mod_kernel,
        grid=(depth, n // tn),
        in_specs=[pl.BlockSpec((8, d), lambda l, j: (0, 0)),
                  pl.BlockSpec((1, d, tn), lambda l, j: (l, 0, j)),
                  pl.BlockSpec((1, 1, tn), lambda l, j: (l, 0, j))],
        out_specs=pl.BlockSpec((1, 8, tn), lambda l, j: (l, 0, j)),
        out_shape=jax.ShapeDtypeStruct((depth, 8, n), F32),
        compiler_params=_params(("parallel", "parallel")),
        name="adaln_modulation",
    )(cond, mod_w, mod_b.reshape(depth, 1, n))


def _seg_fn(tiles_per_batch, n_batch):
    return lambda i: jnp.minimum(i // tiles_per_batch, n_batch)


def _in_prologue(x_ref, g_ref, sc_ref, sh_ref, hx_ref):
    hx_ref[...] = _rms_modulate(x_ref[...], g_ref[...], sc_ref[...], sh_ref[...]).astype(BF16)


def _in_even_kernel(x_ref, g_ref, sc_ref, sh_ref, w_ref, wvt_ref, qk_ref, vt_ref, uz_ref, hx_ref, *, tn, q_width):
    _in_prologue(x_ref, g_ref, sc_ref, sh_ref, hx_ref)
    n_qk = qk_ref.shape[1]
    for j in range(n_qk // tn):
        cols = slice(j * tn, (j + 1) * tn)
        acc = jnp.dot(hx_ref[...], w_ref[:, cols], preferred_element_type=F32)
        if j * tn < q_width:
            acc = acc * (ATTN_SCALE * LOG2E)
        qk_ref[:, cols] = acc.astype(BF16)
    for j in range(uz_ref.shape[1] // tn):
        uz_ref[:, j * tn:(j + 1) * tn] = jnp.dot(hx_ref[...], w_ref[:, n_qk + j * tn:n_qk + (j + 1) * tn],
                                                 preferred_element_type=F32)
    for j in range(vt_ref.shape[0] // tn):
        vt_ref[j * tn:(j + 1) * tn, :] = lax.dot_general(
            wvt_ref[j * tn:(j + 1) * tn, :], hx_ref[...], (((1,), (1,)), ((), ())),
            preferred_element_type=F32).astype(BF16)


def _in_odd_kernel(x_ref, g_ref, sc_ref, sh_ref, w_ref, wvt_ref, cq_ref, sq_ref, ck_ref, sk_ref,
                   qk_ref, vt_ref, hx_ref, *, q_width):
    _in_prologue(x_ref, g_ref, sc_ref, sh_ref, hx_ref)
    for hp in range(qk_ref.shape[1] // MXU_N):
        c_ref, s_ref = (cq_ref, sq_ref) if hp * MXU_N < q_width else (ck_ref, sk_ref)
        y2 = jnp.dot(hx_ref[...], w_ref[:, hp * MXU_N:(hp + 1) * MXU_N], preferred_element_type=F32)
        for h in range(MXU_N // HEAD_DIM):
            y = y2[:, h * HEAD_DIM:(h + 1) * HEAD_DIM]
            cols = slice(hp * MXU_N + h * HEAD_DIM, hp * MXU_N + (h + 1) * HEAD_DIM)
            r = lax.rsqrt(jnp.mean(y * y, axis=-1, keepdims=True) + RMS_EPS)
            sw = pltpu.roll(y, HEAD_DIM // 2, 1)
            qk_ref[:, cols] = ((y * c_ref[...] + sw * s_ref[...]) * r).astype(BF16)
    vt_ref[...] = lax.dot_general(wvt_ref[...], hx_ref[...], (((1,), (1,)), ((), ())),
                                  preferred_element_type=F32).astype(BF16)


def _common_in_specs(tm, d, seg):
    return [pl.BlockSpec((tm, d), lambda i, *_: (i, 0)),
            pl.BlockSpec((1, d), lambda i, *_: (0, 0)),
            pl.BlockSpec((None, None, 1, d), lambda i, *_: (seg(i), 1, 0, 0)),
            pl.BlockSpec((None, None, 1, d), lambda i, *_: (seg(i), 0, 0, 0))]


def _in_proj_even(xs, g, mods, w, w_vt, seg, tm):
    nt, d = xs.shape
    n = w.shape[1]
    n_qk = 2 * NA_WIDTH
    v_w = w_vt.shape[0]
    resident = dict(pipeline_mode=pl.Buffered(1))
    return pl.pallas_call(
        functools.partial(_in_even_kernel, tn=512, q_width=NA_WIDTH),
        grid=(nt // tm,),
        in_specs=_common_in_specs(tm, d, seg) + [
            pl.BlockSpec((d, n), lambda i: (0, 0), **resident),
            pl.BlockSpec((v_w, d), lambda i: (0, 0), **resident)],
        out_specs=[pl.BlockSpec((tm, n_qk), lambda i: (i, 0)),
                   pl.BlockSpec((v_w, tm), lambda i: (0, i)),
                   pl.BlockSpec((tm, n - n_qk), lambda i: (i, 0))],
        out_shape=[jax.ShapeDtypeStruct((nt, n_qk), BF16),
                   jax.ShapeDtypeStruct((v_w, nt), BF16),
                   jax.ShapeDtypeStruct((nt, n - n_qk), F32)],
        scratch_shapes=[pltpu.VMEM((tm, d), BF16)],
        compiler_params=pltpu.CompilerParams(dimension_semantics=("parallel",),
                                             vmem_limit_bytes=VMEM_LIMIT_BIG),
        name="in_proj_even",
    )(xs, g.reshape(1, d), mods, mods, w, w_vt)


def _in_proj_odd(xs, g, mods, w_qk, w_vt, ctab, stab, seg, tm, tiles_per_batch, n_lat_tiles, q_width):
    nt, d = xs.shape
    n_qk = w_qk.shape[1]
    kv_w = w_vt.shape[0]
    rope_row = lambda i: jnp.where(i < n_lat_tiles, i % tiles_per_batch, tiles_per_batch)
    tab_spec = lambda which: pl.BlockSpec((None, tm, HEAD_DIM), lambda i: (which, rope_row(i), 0))
    resident = dict(pipeline_mode=pl.Buffered(1))
    return pl.pallas_call(
        functools.partial(_in_odd_kernel, q_width=q_width),
        grid=(nt // tm,),
        in_specs=_common_in_specs(tm, d, seg) + [
            pl.BlockSpec((d, n_qk), lambda i: (0, 0), **resident),
            pl.BlockSpec((kv_w, d), lambda i: (0, 0), **resident),
            tab_spec(0), tab_spec(0), tab_spec(1), tab_spec(1)],
        out_specs=[pl.BlockSpec((tm, n_qk), lambda i: (i, 0)),
                   pl.BlockSpec((kv_w, tm), lambda i: (0, i))],
        out_shape=[jax.ShapeDtypeStruct((nt, n_qk), BF16),
                   jax.ShapeDtypeStruct((kv_w, nt), BF16)],
        scratch_shapes=[pltpu.VMEM((tm, d), BF16)],
        compiler_params=_params(("parallel",)),
        name="in_proj_odd",
    )(xs, g.reshape(1, d), mods, mods, w_qk, w_vt, ctab, stab, ctab, stab)


def _na_kernel(q_ref, k_ref, vt_ref, qc_ref, kc_ref, vtc_ref, bias_ref, o_ref, oc_ref, *, n_rows, blocks_per_step):
    c = pl.program_id(2)
    bq = NA_BLOCK_ROWS * GRID_W
    span = NA_SPAN_ROWS * GRID_W
    n_blocks = n_rows // NA_BLOCK_ROWS
    dn = (((1,), (1,)), ((), ()))
    kc = kc_ref[...]
    vtc = vtc_ref[...]

    for j in range(blocks_per_step):
        blk = c * blocks_per_step + j
        kr_base = jnp.clip(blk * NA_BLOCK_ROWS - NA_WIN_ROWS // 2, 0, n_rows - NA_SPAN_ROWS)
        start = pl.multiple_of(kr_base * GRID_W, NA_BLOCK_ROWS * GRID_W)
        kind = jnp.where(blk == 0, 0, jnp.where(blk == n_blocks - 1, 2, 1))
        q = q_ref[j * bq:(j + 1) * bq, :]
        s_nb = lax.dot_general(k_ref[pl.ds(start, span), :], q, dn, preferred_element_type=F32) + bias_ref[kind]
        s_cx = lax.dot_general(kc, q, dn, preferred_element_type=F32)
        m = jnp.maximum(s_nb.max(axis=0, keepdims=True), s_cx.max(axis=0, keepdims=True))
        p_nb = jnp.exp2(s_nb - m)
        p_cx = jnp.exp2(s_cx - m)
        l = p_nb.sum(axis=0, keepdims=True) + p_cx.sum(axis=0, keepdims=True)
        o = (jnp.dot(vt_ref[:, pl.ds(start, span)], p_nb.astype(BF16), preferred_element_type=F32)
             + jnp.dot(vtc, p_cx.astype(BF16), preferred_element_type=F32))
        o_ref[j * bq:(j + 1) * bq, :] = (o / l).T.astype(BF16)

    @pl.when(c == pl.num_programs(2) - 1)
    def _():
        s = lax.dot_general(kc, qc_ref[...], dn, preferred_element_type=F32)
        p = jnp.exp2(s - s.max(axis=0, keepdims=True))
        l = p.sum(axis=0, keepdims=True)
        o = jnp.dot(vtc, p.astype(BF16), preferred_element_type=F32)
        oc_ref[...] = (o / l).T.astype(BF16)


def _na_bias_table(rpb, n_rows):
    n_heads = rpb.shape[0]
    qc = jnp.arange(GRID_W)[:, None]
    kc = jnp.arange(GRID_W)[None, :]
    kc0 = jnp.clip(qc - NA_WIN_COLS // 2, 0, GRID_W - NA_WIN_COLS)
    valid = (kc >= kc0) & (kc < kc0 + NA_WIN_COLS)
    off_c = kc - qc + (NA_WIN_COLS - 1)
    col = jnp.zeros((n_heads, 2 * NA_WIN_ROWS - 1, GRID_W, GRID_W), F32)
    for o in range(2 * NA_WIN_COLS - 1):
        col = col + jnp.where((off_c == o)[None, None], rpb[:, :, o, None, None].astype(F32), 0.0)
    col = jnp.where(valid[None, None], col * LOG2E, MASK_VALUE).transpose(0, 1, 3, 2)
    tables = []
    for r0, kr_base in ((0, 0), (NA_BLOCK_ROWS, 0), (n_rows - NA_BLOCK_ROWS, n_rows - NA_SPAN_ROWS)):
        per_row = []
        for t in range(NA_BLOCK_ROWS):
            q_row = r0 + t
            win0 = min(max(q_row - NA_WIN_ROWS // 2, 0), n_rows - NA_WIN_ROWS)
            u_lo = win0 - kr_base
            off_lo = win0 - q_row + (NA_WIN_ROWS - 1)
            assert 0 <= u_lo <= NA_SPAN_ROWS - NA_WIN_ROWS
            per_row.append(jnp.pad(col[:, off_lo:off_lo + NA_WIN_ROWS],
                                   ((0, 0), (u_lo, NA_SPAN_ROWS - NA_WIN_ROWS - u_lo), (0, 0), (0, 0)),
                                   constant_values=MASK_VALUE))
        tb = jnp.stack(per_row, axis=3)
        tables.append(tb.reshape(n_heads, NA_SPAN_ROWS * GRID_W, NA_BLOCK_ROWS * GRID_W))
    return jnp.stack(tables)


def _na_attention(qk, vt, bias_tbl, n_batch, seq, ctx_len):
    n_rows = seq // GRID_W
    bq = NA_BLOCK_ROWS * GRID_W
    span = NA_SPAN_ROWS * GRID_W
    step_rows = min(2048, seq)
    assert ctx_len == bq and n_rows % NA_BLOCK_ROWS == 0 and n_rows >= NA_SPAN_ROWS + NA_BLOCK_ROWS
    steps = seq // step_rows
    ctx_blk0 = n_batch * seq // ctx_len
    h_k = NA_HEADS
    return pl.pallas_call(
        functools.partial(_na_kernel, n_rows=n_rows, blocks_per_step=step_rows // bq),
        grid=(n_batch, NA_HEADS, steps),
        in_specs=[pl.BlockSpec((step_rows, HEAD_DIM), lambda b, h, c: (b * steps + c, h)),
                  pl.BlockSpec((seq, HEAD_DIM), lambda b, h, c: (b, h_k + h)),
                  pl.BlockSpec((HEAD_DIM, seq), lambda b, h, c: (h, b)),
                  pl.BlockSpec((ctx_len, HEAD_DIM), lambda b, h, c: (ctx_blk0 + b, h)),
                  pl.BlockSpec((ctx_len, HEAD_DIM), lambda b, h, c: (ctx_blk0 + b, h_k + h)),
                  pl.BlockSpec((HEAD_DIM, ctx_len), lambda b, h, c: (h, ctx_blk0 + b)),
                  pl.BlockSpec((3, None, span, bq), lambda b, h, c: (0, h, 0, 0))],
        out_specs=[pl.BlockSpec((step_rows, HEAD_DIM), lambda b, h, c: (b * steps + c, h)),
                   pl.BlockSpec((ctx_len, HEAD_DIM), lambda b, h, c: (b, h))],
        out_shape=[jax.ShapeDtypeStruct((n_batch * seq, NA_WIDTH), BF16),
                   jax.ShapeDtypeStruct((n_batch * ctx_len, NA_WIDTH), BF16)],
        compiler_params=_params(("parallel", "parallel", "arbitrary")),
        name="neighbourhood_attention",
    )(qk, qk, vt, qk, qk, vt, bias_tbl)


def _sg_kernel(uz_ref, ws_ref, bs_ref, lg_ref, lb_ref, o_ref, *, n_chunks):
    for g in range(SG_GROUPS):
        w = ws_ref[g].astype(BF16)
        for n in range(n_chunks):
            rows = slice(n * SG_CHUNK, (n + 1) * SG_CHUNK)
            u = uz_ref[rows, g * SG_DIM:(g + 1) * SG_DIM]
            z = uz_ref[rows, SG_WIDTH + g * SG_DIM:SG_WIDTH + (g + 1) * SG_DIM]
            zz = _gelu_tanh(z)
            mu = jnp.mean(zz, axis=-1, keepdims=True)
            xc = zz - mu
            var = jnp.mean(xc * xc, axis=-1, keepdims=True)
            zn = xc * lax.rsqrt(var + LN_EPS) * lg_ref[g] + lb_ref[g]
            mixed = jnp.dot(w, zn.astype(BF16), preferred_element_type=F32) + bs_ref[g]
            o_ref[rows, g * SG_DIM:(g + 1) * SG_DIM] = (_gelu_tanh(u) * mixed).astype(BF16)


def _spatial_gating(uz, w_s, b_s, ln_g, ln_b, tm):
    nt = uz.shape[0]
    full3 = lambda i: (0, 0, 0)
    return pl.pallas_call(
        functools.partial(_sg_kernel, n_chunks=tm // SG_CHUNK),
        grid=(nt // tm,),
        in_specs=[pl.BlockSpec((tm, 2 * SG_WIDTH), lambda i: (i, 0)),
                  pl.BlockSpec((SG_GROUPS, SG_CHUNK, SG_CHUNK), full3),
                  pl.BlockSpec((SG_GROUPS, SG_CHUNK, 1), full3),
                  pl.BlockSpec((SG_GROUPS, 1, SG_DIM), full3),
                  pl.BlockSpec((SG_GROUPS, 1, SG_DIM), full3)],
        out_specs=pl.BlockSpec((tm, SG_WIDTH), lambda i: (i, 0)),
        out_shape=jax.ShapeDtypeStruct((nt, SG_WIDTH), BF16),
        compiler_params=_params(("parallel",)),
        name="spatial_gating",
    )(uz, w_s, b_s.reshape(SG_GROUPS, SG_CHUNK, 1), ln_g.reshape(SG_GROUPS, 1, SG_DIM),
      ln_b.reshape(SG_GROUPS, 1, SG_DIM))


def _flash_kernel(q_ref, k_ref, vt_ref, kc_ref, vtc_ref, o_ref, q_s, s_a, s_b, s_c, acc_s, m_s, l_s,
                  *, tq, tk, seq, group, n_lat_q):
    qi = pl.program_id(2)
    for g in range(group):
        q_s[g * tq:(g + 1) * tq, :] = q_ref[:, g * HEAD_DIM:(g + 1) * HEAD_DIM]
    m_rows = group * tq
    n_blk = m_rows // MXU_N
    n_tiles = seq // tk
    dn = (((1,), (1,)), ((), ()))
    blk_cols = lambda n: slice(n * MXU_N, (n + 1) * MXU_N)

    def scores(k, s_ref, n):
        s_ref[:, blk_cols(n)] = lax.dot_general(k, q_s[blk_cols(n), :], dn, preferred_element_type=F32)

    def consume(s_ref, vt, n):
        cols = blk_cols(n)
        s = s_ref[:, cols]
        m_old = m_s[:, cols]
        m_new = jnp.maximum(m_old, s.max(axis=0, keepdims=True))
        alpha = jnp.exp2(m_old - m_new)
        p = jnp.exp2(s - m_new)
        l_s[:, cols] = alpha * l_s[:, cols] + p.sum(axis=0, keepdims=True)
        acc_s[:, cols] = alpha * acc_s[:, cols] + jnp.dot(vt, p.astype(BF16), preferred_element_type=F32)
        m_s[:, cols] = m_new

    def scores_and_consume(k_next, s_next, s_cur, vt_cur):
        for n in range(n_blk):
            scores(k_next, s_next, n)
            consume(s_cur, vt_cur, n)

    k_tile = lambda t: k_ref[pl.ds(pl.multiple_of(t * tk, tk), tk), :]
    vt_tile = lambda t: vt_ref[:, pl.ds(pl.multiple_of(t * tk, tk), tk)]

    m_s[...] = jnp.full(m_s.shape, MASK_VALUE, F32)
    l_s[...] = jnp.zeros(l_s.shape, F32)
    acc_s[...] = jnp.zeros(acc_s.shape, F32)

    @pl.when(qi < n_lat_q)
    def _():
        for n in range(n_blk):
            scores(k_tile(0), s_a, n)

        def pair(u, carry):
            t0 = 2 * u
            scores_and_consume(k_tile(t0 + 1), s_b, s_a, vt_tile(t0))
            scores_and_consume(k_tile(t0 + 2), s_a, s_b, vt_tile(t0 + 1))
            return carry

        lax.fori_loop(0, n_tiles // 2 - 1, pair, 0)
        scores_and_consume(k_tile(n_tiles - 1), s_b, s_a, vt_tile(n_tiles - 2))
        scores_and_consume(kc_ref[...], s_c, s_b, vt_tile(n_tiles - 1))
        for n in range(n_blk):
            consume(s_c, vtc_ref[...], n)

    @pl.when(qi == n_lat_q)
    def _():
        for n in range(n_blk):
            scores(kc_ref[...], s_c, n)
            consume(s_c, vtc_ref[...], n)

    o = acc_s[...] / l_s[...]
    for g in range(group):
        o_ref[:, g * HEAD_DIM:(g + 1) * HEAD_DIM] = o[:, g * tq:(g + 1) * tq].T.astype(BF16)


def _flash_attention(qk, vt, n_batch, seq, ctx_len, n_q_heads):
    nt = qk.shape[0]
    group = n_q_heads // GQA_KV_HEADS
    tq = ctx_len
    tk = min(FLASH_KEY_TILE, seq // 2)
    assert seq % (2 * tk) == 0 and (group * tq) % MXU_N == 0
    n_lat_q = seq // tq
    ctx_blk0 = n_batch * seq // tq
    hk0 = n_q_heads
    gw = group * HEAD_DIM
    m_rows = group * tq

    def q_idx(b, kh, qi):
        return (jnp.where(qi < n_lat_q, b * n_lat_q + qi, ctx_blk0 + b), kh)

    return pl.pallas_call(
        functools.partial(_flash_kernel, tq=tq, tk=tk, seq=seq, group=group, n_lat_q=n_lat_q),
        grid=(n_batch, GQA_KV_HEADS, n_lat_q + 1),
        in_specs=[pl.BlockSpec((tq, gw), q_idx),
                  pl.BlockSpec((seq, HEAD_DIM), lambda b, kh, qi: (b, hk0 + kh)),
                  pl.BlockSpec((HEAD_DIM, seq), lambda b, kh, qi: (kh, b)),
                  pl.BlockSpec((tq, HEAD_DIM), lambda b, kh, qi: (ctx_blk0 + b, hk0 + kh)),
                  pl.BlockSpec((HEAD_DIM, tq), lambda b, kh, qi: (kh, ctx_blk0 + b))],
        out_specs=pl.BlockSpec((tq, gw), q_idx),
        out_shape=jax.ShapeDtypeStruct((nt, n_q_heads * HEAD_DIM), BF16),
        scratch_shapes=[pltpu.VMEM((m_rows, HEAD_DIM), BF16),
                        pltpu.VMEM((tk, m_rows), F32), pltpu.VMEM((tk, m_rows), F32),
                        pltpu.VMEM((tq, m_rows), F32),
                        pltpu.VMEM((HEAD_DIM, m_rows), F32),
                        pltpu.VMEM((1, m_rows), F32), pltpu.VMEM((1, m_rows), F32)],
        compiler_params=_params(("parallel", "parallel", "arbitrary")),
        name="gqa_flash_attention",
    )(qk, qk, vt, qk, vt)


def _out_kernel(*refs, n_lhs):
    lhs_refs = refs[:n_lhs]
    (w_ref, x_ref, ga_ref, g2_ref, sc_ref, sh_ref, wr_ref, br_ref, xo_ref, tok_ref, lg_ref) = refs[n_lhs:]
    acc = None
    k0 = 0
    for a_ref in lhs_refs:
        kp = a_ref.shape[1]
        part = jnp.dot(a_ref[...], w_ref[k0:k0 + kp, :], preferred_element_type=F32)
        acc = part if acc is None else acc + part
        k0 += kp
    xn = x_ref[...] + ga_ref[...] * acc
    xo_ref[...] = xn
    tok = _rms_modulate(xn, g2_ref[...], sc_ref[...], sh_ref[...])
    tok_ref[...] = _pack_bf16_pairs(tok)
    lg_ref[...] = lax.dot_general(wr_ref[...], tok.astype(BF16), (((1,), (1,)), ((), ())),
                                  preferred_element_type=F32) + br_ref[...]


def _out_proj(lhs, w, xs, mods, g2, w_router, b_router, seg, tm):
    nt, d = xs.shape
    k = w.shape[0]
    modspec = lambda which: pl.BlockSpec((None, None, 1, d), lambda i: (seg(i), which, 0, 0))
    return pl.pallas_call(
        functools.partial(_out_kernel, n_lhs=len(lhs)),
        grid=(nt // tm,),
        in_specs=[pl.BlockSpec((tm, a.shape[1]), lambda i: (i, 0)) for a in lhs] + [
            pl.BlockSpec((k, d), lambda i: (0, 0)),
            pl.BlockSpec((tm, d), lambda i: (i, 0)),
            modspec(2),
            pl.BlockSpec((1, d), lambda i: (0, 0)),
            modspec(4),
            modspec(3),
            pl.BlockSpec((ROUTER_W, d), lambda i: (0, 0)),
            pl.BlockSpec((ROUTER_W, 1), lambda i: (0, 0))],
        out_specs=[pl.BlockSpec((tm, d), lambda i: (i, 0)),
                   pl.BlockSpec((tm, d // 2), lambda i: (i, 0)),
                   pl.BlockSpec((ROUTER_W, tm), lambda i: (0, i))],
        out_shape=[jax.ShapeDtypeStruct((nt, d), F32),
                   jax.ShapeDtypeStruct((nt, d // 2), jnp.uint32),
                   jax.ShapeDtypeStruct((ROUTER_W, nt), F32)],
        compiler_params=_params(("parallel",)),
        name="out_proj_residual_router",
    )(*lhs, w, xs, mods, g2.reshape(1, d), mods, mods, w_router, b_router)


def _expert_kernel(blk_e_ref, n_used_ref, xs_ref, w1_ref, w3_ref, w2_ref, o_ref):
    b = pl.program_id(0)

    @pl.when(b < n_used_ref[0])
    def _():
        x = _unpack_bf16_pairs(xs_ref[...]).astype(BF16)
        h1 = jnp.dot(x, w1_ref[...].astype(BF16), preferred_element_type=F32)
        h3 = jnp.dot(x, w3_ref[...].astype(BF16), preferred_element_type=F32)
        a = (h1 * _sigmoid(h1) * h3).astype(BF16)
        o_ref[...] = _pack_bf16_pairs(jnp.dot(a, w2_ref[...].astype(BF16), preferred_element_type=F32))

    @pl.when(b >= n_used_ref[0])
    def _():
        o_ref[...] = jnp.zeros_like(o_ref)


def _experts(xs, blk_e, n_used, w1, w3, w2, layer):
    p, dw = xs.shape
    d, hid = w1.shape[-2:]
    nb = p // MOE_ROWS
    grid_spec = pltpu.PrefetchScalarGridSpec(
        num_scalar_prefetch=2,
        grid=(nb,),
        in_specs=[pl.BlockSpec((MOE_ROWS, dw), lambda b, be, nu: (jnp.minimum(b, nu[0] - 1), 0)),
                  pl.BlockSpec((None, None, d, hid), lambda b, be, nu: (layer, be[b], 0, 0)),
                  pl.BlockSpec((None, None, d, hid), lambda b, be, nu: (layer, be[b], 0, 0)),
                  pl.BlockSpec((None, None, hid, d), lambda b, be, nu: (layer, be[b], 0, 0))],
        out_specs=pl.BlockSpec((MOE_ROWS, dw), lambda b, be, nu: (b, 0)),
    )
    return pl.pallas_call(
        _expert_kernel,
        grid_spec=grid_spec,
        out_shape=jax.ShapeDtypeStruct((p, dw), jnp.uint32),
        compiler_params=_params(("arbitrary",)),
        name="moe_experts",
    )(blk_e, n_used, xs, w1, w3, w2)


def _first_index(vals, target):
    idx = jnp.full(target.shape, len(vals) - 1, jnp.int32)
    for i in range(len(vals) - 2, -1, -1):
        idx = jnp.where(vals[i] == target, i, idx)
    return idx


def _router_kernel(lg_ref, gate_ref, pos_ref, blk_ref, nused_ref, e_s, rank_s, *, n_tiles):
    tw = ROUTE_TILE
    tri = jnp.where(lax.broadcasted_iota(jnp.int32, (tw, tw), 0) < lax.broadcasted_iota(jnp.int32, (tw, tw), 1),
                    1.0, 0.0).astype(BF16)
    eid = lax.broadcasted_iota(jnp.int32, (MOE_EXPERTS, tw), 0)
    epg = MOE_EXPERTS_PER_GROUP

    def pass1(c, run):
        sl = pl.ds(pl.multiple_of(c * tw, tw), tw)
        g = [lg_ref[i:i + 1, sl] for i in range(MOE_GROUPS)]
        gm = functools.reduce(jnp.maximum, g)
        gidx = _first_index(g, gm)
        gval = 1.0 / functools.reduce(lambda a, b: a + b, [jnp.exp(gi - gm) for gi in g])
        le = []
        for e in range(epg):
            sel = lg_ref[MOE_GROUPS + (MOE_GROUPS - 1) * epg + e:MOE_GROUPS + (MOE_GROUPS - 1) * epg + e + 1, sl]
            for gg in range(MOE_GROUPS - 2, -1, -1):
                sel = jnp.where(gidx == gg, lg_ref[MOE_GROUPS + gg * epg + e:MOE_GROUPS + gg * epg + e + 1, sl], sel)
            le.append(sel)
        m1 = functools.reduce(jnp.maximum, le)
        i1 = _first_index(le, m1)
        le2 = [jnp.where(i1 == e, -jnp.inf, le[e]) for e in range(epg)]
        m2 = functools.reduce(jnp.maximum, le2)
        i2 = _first_index(le2, m2)
        t = jnp.exp(m2 - m1)
        inv = 1.0 / (1.0 + t)
        gate_ref[0:1, sl] = inv * gval
        gate_ref[1:2, sl] = t * inv * gval
        e0 = gidx * epg + i1
        e1 = gidx * epg + i2
        oh0 = eid == e0
        oh1 = eid == e1
        oh0f = jnp.where(oh0, 1.0, 0.0)
        oh1f = jnp.where(oh1, 1.0, 0.0)
        pre0 = jnp.dot(oh0f.astype(BF16), tri, preferred_element_type=F32)
        pre1 = jnp.dot(oh1f.astype(BF16), tri, preferred_element_type=F32)
        c0 = oh0f.sum(axis=1, keepdims=True)
        c1 = oh1f.sum(axis=1, keepdims=True)
        rank_s[0:1, sl] = jnp.where(oh0, run + pre0, 0.0).sum(axis=0, keepdims=True)
        rank_s[1:2, sl] = jnp.where(oh1, run + c0 + pre1, 0.0).sum(axis=0, keepdims=True)
        e_s[0:1, sl] = e0
        e_s[1:2, sl] = e1
        return run + c0 + c1

    counts = lax.fori_loop(0, n_tiles, pass1, jnp.zeros((MOE_EXPERTS, 1), F32))
    blocks = jnp.floor((counts + (MOE_ROWS - 1)) * (1.0 / MOE_ROWS))
    lincl = jnp.where(lax.broadcasted_iota(jnp.int32, (MOE_EXPERTS, MOE_EXPERTS), 1)
                      <= lax.broadcasted_iota(jnp.int32, (MOE_EXPERTS, MOE_EXPERTS), 0), 1.0, 0.0).astype(BF16)
    end_blocks = jnp.dot(lincl, jnp.broadcast_to(blocks, (MOE_EXPERTS, LANES)).astype(BF16),
                         preferred_element_type=F32)[:, 0:1]
    start_rows = (end_blocks - blocks) * MOE_ROWS

    def pass2(c, carry):
        sl = pl.ds(pl.multiple_of(c * tw, tw), tw)
        for k in range(MOE_TOP_K):
            base = jnp.where(eid == e_s[k:k + 1, sl], start_rows, 0.0).sum(axis=0, keepdims=True)
            pos_ref[k:k + 1, sl] = (base + rank_s[k:k + 1, sl]).astype(jnp.int32)
        return carry

    lax.fori_loop(0, n_tiles, pass2, 0)
    bl = lax.broadcasted_iota(jnp.int32, (MOE_EXPERTS, PLAN_LANES), 1).astype(F32)
    blk = jnp.where(end_blocks <= bl, 1.0, 0.0).sum(axis=0, keepdims=True)
    blk_ref[...] = jnp.minimum(blk, MOE_EXPERTS - 1.0).astype(jnp.int32)
    nused_ref[...] = jnp.broadcast_to(end_blocks[MOE_EXPERTS - 1:MOE_EXPERTS, :], (1, LANES)).astype(jnp.int32)


def _router(lg_t):
    n = lg_t.shape[1]
    assert n % ROUTE_TILE == 0
    return pl.pallas_call(
        functools.partial(_router_kernel, n_tiles=n // ROUTE_TILE),
        out_shape=[jax.ShapeDtypeStruct((MOE_TOP_K, n), F32),
                   jax.ShapeDtypeStruct((MOE_TOP_K, n), jnp.int32),
                   jax.ShapeDtypeStruct((1, PLAN_LANES), jnp.int32),
                   jax.ShapeDtypeStruct((1, LANES), jnp.int32)],
        scratch_shapes=[pltpu.VMEM((MOE_TOP_K, n), jnp.int32), pltpu.VMEM((MOE_TOP_K, n), F32)],
        compiler_params=pltpu.CompilerParams(vmem_limit_bytes=VMEM_LIMIT),
        name="moe_router_plan",
    )(lg_t)


def _dispatch_kernel(p0_ref, p1_ref, tok_ref, init_ref, xs_ref, sem, *, tm):
    del init_ref
    base = pl.program_id(0) * tm

    def copy(r, p_ref):
        return pltpu.make_async_copy(tok_ref.at[pl.ds(r, 1)], xs_ref.at[pl.ds(p_ref[base + r], 1)], sem)

    def start(r, carry):
        copy(r, p0_ref).start()
        copy(r, p1_ref).start()
        return carry

    def wait(r, carry):
        copy(r, p0_ref).wait()
        copy(r, p1_ref).wait()
        return carry

    lax.fori_loop(0, tm, start, 0, unroll=8)
    lax.fori_loop(0, tm, wait, 0, unroll=8)


def _dispatch(tok, pos0, pos1, n_slots, tm):
    nt, d = tok.shape
    grid_spec = pltpu.PrefetchScalarGridSpec(
        num_scalar_prefetch=2,
        grid=(nt // tm,),
        in_specs=[pl.BlockSpec((tm, d), lambda i, p0, p1: (i, 0)),
                  pl.BlockSpec(memory_space=pl.ANY)],
        out_specs=pl.BlockSpec(memory_space=pl.ANY),
        scratch_shapes=[pltpu.SemaphoreType.DMA(())],
    )
    return pl.pallas_call(
        functools.partial(_dispatch_kernel, tm=tm),
        grid_spec=grid_spec,
        out_shape=jax.ShapeDtypeStruct((n_slots, d), tok.dtype),
        input_output_aliases={3: 0},
        compiler_params=_params(("arbitrary",)),
        name="moe_dispatch",
    )(pos0, pos1, tok, jnp.zeros((n_slots, d), tok.dtype))


def _combine_kernel(p0_ref, p1_ref, x_ref, yb_ref, gt_ref, ga_ref, *rest, tm, final):
    if final:
        gf_ref, o_ref, y0_s, y1_s, sem = rest
    else:
        o_ref, y0_s, y1_s, sem = rest
    base = pl.program_id(0) * tm

    def copy(r, p_ref, y_s):
        return pltpu.make_async_copy(yb_ref.at[pl.ds(p_ref[base + r], 1)], y_s.at[pl.ds(r, 1)], sem)

    def start(r, carry):
        copy(r, p0_ref, y0_s).start()
        copy(r, p1_ref, y1_s).start()
        return carry

    def wait(r, carry):
        copy(r, p0_ref, y0_s).wait()
        copy(r, p1_ref, y1_s).wait()
        return carry

    lax.fori_loop(0, tm, start, 0, unroll=8)
    lax.fori_loop(0, tm, wait, 0, unroll=8)
    gt = gt_ref[...]
    f = gt[:, 0:1] * _unpack_bf16_pairs(y0_s[...]) + gt[:, 1:2] * _unpack_bf16_pairs(y1_s[...])
    xn = x_ref[...] + ga_ref[...] * f
    if final:
        r = lax.rsqrt(jnp.mean(xn * xn, axis=-1, keepdims=True) + RMS_EPS)
        xn = xn * r * gf_ref[...]
    o_ref[...] = xn


def _combine(xs, yb, pos0, pos1, gate, mods, seg, tm, final_g=None, n_out_rows=None):
    nt, d = xs.shape
    rows = nt if n_out_rows is None else n_out_rows
    row_blk = lambda i, p0, p1: (i, 0)
    in_specs = [pl.BlockSpec((tm, d), row_blk),
                pl.BlockSpec(memory_space=pl.ANY),
                pl.BlockSpec((tm, MOE_TOP_K), row_blk),
                pl.BlockSpec((None, None, 1, d), lambda i, p0, p1: (seg(i), 5, 0, 0))]
    args = [xs, yb, gate, mods]
    if final_g is not None:
        in_specs.append(pl.BlockSpec((1, d), lambda i, p0, p1: (0, 0)))
        args.append(final_g.reshape(1, d))
    grid_spec = pltpu.PrefetchScalarGridSpec(
        num_scalar_prefetch=2,
        grid=(rows // tm,),
        in_specs=in_specs,
        out_specs=pl.BlockSpec((tm, d), row_blk),
        scratch_shapes=[pltpu.VMEM((tm, d // 2), jnp.uint32), pltpu.VMEM((tm, d // 2), jnp.uint32),
                        pltpu.SemaphoreType.DMA(())],
    )
    return pl.pallas_call(
        functools.partial(_combine_kernel, tm=tm, final=final_g is not None),
        grid_spec=grid_spec,
        out_shape=jax.ShapeDtypeStruct((rows, d), F32),
        compiler_params=_params(("arbitrary",)),
        name="moe_combine_residual",
    )(pos0, pos1, *args)


def _rope_tables(seq, pad_rows):
    t = jnp.arange(seq, dtype=jnp.int32)
    row = (t // GRID_W).astype(F32)
    col = (t % GRID_W).astype(F32)
    inv_freq = ROPE_THETA ** (-jnp.arange(ROPE_AXIS_DIM // 2, dtype=F32) * 2.0 / ROPE_AXIS_DIM)
    ang = jnp.concatenate([row[:, None] * inv_freq, col[:, None] * inv_freq], axis=-1)
    cos, sin = jnp.cos(ang), jnp.sin(ang)
    cos = jnp.concatenate([cos, jnp.ones((pad_rows, HEAD_DIM // 2), F32)], axis=0)
    sin = jnp.concatenate([sin, jnp.zeros((pad_rows, HEAD_DIM // 2), F32)], axis=0)
    return cos, sin


def _split_pairs(w, n_heads):
    d = w.shape[0]
    return w.reshape(d, n_heads, HEAD_DIM // 2, 2).transpose(0, 1, 3, 2).reshape(d, n_heads * HEAD_DIM)


def _gain_rope_tables(cos, sin, q_gain, k_gain):
    def tables(g):
        ge, go = g[0::2][None, :], g[1::2][None, :]
        return (jnp.concatenate([ge * cos, go * cos], axis=1),
                jnp.concatenate([-go * sin, ge * sin], axis=1))
    cq, sq = tables(q_gain)
    ck, sk = tables(k_gain)
    return jnp.stack([cq, ck]), jnp.stack([sq, sk])


def kernel(x, c, ctx, c_ctx, mod_w, mod_b, norm_mix_g, norm_ffn_g, norm_final_g, na_sg_w_in, na_sg_w_out, na_rpb, sg_w_s, sg_b_s, sg_ln_g, sg_ln_b, gqa_w_in, gqa_w_out, gqa_q_gain, gqa_k_gain, moe_w_group, moe_b_group, moe_w_expert, moe_b_expert, moe_w1, moe_w3, moe_w2):
    n_batch, seq, d = x.shape
    ctx_len = ctx.shape[1]
    depth = mod_w.shape[0]
    n_lat = n_batch * seq
    tm = min(ROW_TILE, ctx_len * n_batch)
    assert seq % tm == 0 and (n_batch * ctx_len) % tm == 0 and seq % ctx_len == 0
    assert ctx_len % GRID_W == 0 and n_batch + 1 <= 8
    tiles_per_batch = seq // tm
    seg = _seg_fn(tiles_per_batch, n_batch)
    n_q_heads = d // HEAD_DIM
    n_tok = n_lat + n_batch * ctx_len
    n_blocks = -(-n_tok * MOE_TOP_K // MOE_ROWS) + MOE_EXPERTS
    assert n_blocks <= PLAN_LANES and n_tok % ROUTE_TILE == 0

    xs = jnp.concatenate([x.reshape(n_lat, d), ctx.reshape(n_batch * ctx_len, d)], axis=0)
    cond = jnp.zeros((8, d), F32).at[:n_batch].set(c).at[n_batch].set(c_ctx)
    mod_all = _modulation(cond, mod_w, mod_b)
    cos, sin = _rope_tables(seq, tm)

    q_w = n_q_heads * HEAD_DIM
    kv_w = GQA_KV_HEADS * HEAD_DIM
    out = None
    for layer in range(depth):
        last = layer == depth - 1
        mods = mod_all[layer, :n_batch + 1].reshape(n_batch + 1, 6, 1, d)
        i = layer // 2
        if layer % 2 == 0:
            w_in = na_sg_w_in[i]
            w_main = jnp.concatenate([w_in[:, :2 * NA_WIDTH], w_in[:, 3 * NA_WIDTH:]], axis=1).astype(BF16)
            w_vt = w_in[:, 2 * NA_WIDTH:3 * NA_WIDTH].T.astype(BF16)
            qk, vt, uz = _in_proj_even(xs, norm_mix_g[layer], mods, w_main, w_vt, seg, tm)
            att_lat, att_ctx = _na_attention(qk, vt, _na_bias_table(na_rpb[i], seq // GRID_W), n_batch, seq, ctx_len)
            att = jnp.concatenate([att_lat, att_ctx], axis=0)
            gat = _spatial_gating(uz, sg_w_s[i], sg_b_s[i], sg_ln_g[i], sg_ln_b[i], tm)
            lhs, w_out = [att, gat], na_sg_w_out[i].astype(BF16)
        else:
            ctab, stab = _gain_rope_tables(cos, sin, gqa_q_gain[i] * (ATTN_SCALE * LOG2E), gqa_k_gain[i])
            w_qk = _split_pairs(gqa_w_in[i][:, :q_w + kv_w], n_q_heads + GQA_KV_HEADS).astype(BF16)
            w_vt = gqa_w_in[i][:, q_w + kv_w:].T.astype(BF16)
            qk, vt = _in_proj_odd(xs, norm_mix_g[layer], mods, w_qk, w_vt, ctab, stab,
                                  seg, tm, tiles_per_batch, n_lat // tm, q_w)
            att = _flash_attention(qk, vt, n_batch, seq, ctx_len, n_q_heads)
            lhs, w_out = [att], gqa_w_out[i].astype(BF16)

        pad = ROUTER_W - MOE_GROUPS - MOE_EXPERTS
        w_router = jnp.concatenate([moe_w_group[layer], moe_w_expert[layer], jnp.zeros((d, pad), F32)],
                                   axis=1).T.astype(BF16)
        b_router = jnp.concatenate([moe_b_group[layer], moe_b_expert[layer], jnp.zeros((pad,), F32)])[:, None]
        xs, tok, lg_t = _out_proj(lhs, w_out, xs, mods, norm_ffn_g[layer], w_router, b_router, seg, tm)

        gate, pos, blk, nused = _router(lg_t)
        pos0, pos1 = pos[0], pos[1]
        slots = _dispatch(tok, pos0, pos1, n_blocks * MOE_ROWS, tm)
        yb = _experts(slots, blk[0, :n_blocks], nused[0, :1], moe_w1, moe_w3, moe_w2, layer)
        if last:
            out = _combine(xs, yb, pos0, pos1, gate.T, mods, seg, tm, final_g=norm_final_g, n_out_rows=n_lat)
        else:
            xs = _combine(xs, yb, pos0, pos1, gate.T, mods, seg, tm)
    return out.reshape(n_batch, seq, d)
```

```python
import functools

import jax
import jax.numpy as jnp
from jax import lax
from jax.experimental import pallas as pl
from jax.experimental.pallas import tpu as pltpu

F32 = jnp.float32
BF16 = jnp.bfloat16

GRID_W = 64
HEAD_DIM = 128
NA_HEADS = 8
NA_WIDTH = NA_HEADS * HEAD_DIM
NA_WIN_ROWS = 8
NA_WIN_COLS = 16
NA_BLOCK_ROWS = 4
NA_SPAN_ROWS = 12
SG_GROUPS = 8
SG_DIM = 128
SG_WIDTH = SG_GROUPS * SG_DIM
SG_CHUNK = 128
GQA_KV_HEADS = 4
ROPE_THETA = 10000.0
ROPE_AXIS_DIM = HEAD_DIM // 2
MOE_GROUPS = 4
MOE_EXPERTS_PER_GROUP = 8
MOE_EXPERTS = MOE_GROUPS * MOE_EXPERTS_PER_GROUP
MOE_TOP_K = 2
RMS_EPS = 1e-6
LN_EPS = 1e-5

LOG2E = 1.4426950408889634
MASK_VALUE = -1e30
ATTN_SCALE = HEAD_DIM ** -0.5
LANES = 128
MXU_N = 256
BF16_ROWS = 16
ROUTER_W = LANES
ROW_TILE = 512
MOE_ROWS = 256
FLASH_KEY_TILE = 1024
ROUTE_TILE = 512
PLAN_LANES = 256
VMEM_LIMIT = 48 * 1024 * 1024
VMEM_LIMIT_BIG = 56 * 1024 * 1024


def _params(sem):
    return pltpu.CompilerParams(dimension_semantics=sem, vmem_limit_bytes=VMEM_LIMIT)


def _sigmoid(x):
    return 1.0 / (1.0 + jnp.exp(-x))


def _gelu_tanh(x):
    cdf = 0.5 * (1.0 + jnp.tanh(0.7978845608028654 * (x + 0.044715 * (x * x * x))))
    return x * cdf


def _pack_bf16_pairs(x):
    c = x.shape[1] // 2
    bits = pltpu.bitcast(x.astype(jnp.bfloat16).astype(F32), jnp.uint32)
    return (bits[:, :c] >> 16) | bits[:, c:]


def _unpack_bf16_pairs(w):
    lo = pltpu.bitcast(w << 16, F32)
    hi = pltpu.bitcast(w & jnp.uint32(0xFFFF0000), F32)
    return jnp.concatenate([lo, hi], axis=1)


def _rms_modulate(x, g, sc, sh):
    r = lax.rsqrt(jnp.mean(x * x, axis=-1, keepdims=True) + RMS_EPS)
    return (x * r * g) * (1.0 + sc) + sh


def _mod_kernel(c_ref, w_ref, b_ref, o_ref):
    c = c_ref[...]
    cs = (c * _sigmoid(c)).astype(BF16)
    o_ref[0] = jnp.dot(cs, w_ref[0].astype(BF16), preferred_element_type=F32) + b_ref[0]


def _modulation(cond, mod_w, mod_b):
    depth, d, n = mod_w.shape
    tn = 1024
    return pl.pallas_call(
        _mod_kernel,
        grid=(depth, n // tn),
        in_specs=[pl.BlockSpec((8, d), lambda l, j: (0, 0)),
                  pl.BlockSpec((1, d, tn), lambda l, j: (l, 0, j)),
                  pl.BlockSpec((1, 1, tn), lambda l, j: (l, 0, j))],
        out_specs=pl.BlockSpec((1, 8, tn), lambda l, j: (l, 0, j)),
        out_shape=jax.ShapeDtypeStruct((depth, 8, n), F32),
        compiler_params=_params(("parallel", "parallel")),
        name="adaln_modulation",
    )(cond, mod_w, mod_b.reshape(depth, 1, n))


def _seg_fn(tiles_per_batch, n_batch):
    return lambda i: jnp.minimum(i // tiles_per_batch, n_batch)


def _in_prologue(x_ref, g_ref, sc_ref, sh_ref, hx_ref):
    hx_ref[...] = _rms_modulate(x_ref[...], g_ref[...], sc_ref[...], sh_ref[...]).astype(BF16)


def _in_even_kernel(x_ref, g_ref, sc_ref, sh_ref, w_ref, wvt_ref, qk_ref, vt_ref, uz_ref, hx_ref, *, tn, q_width):
    _in_prologue(x_ref, g_ref, sc_ref, sh_ref, hx_ref)
    n_qk = qk_ref.shape[1]
    for j in range(n_qk // tn):
        cols = slice(j * tn, (j + 1) * tn)
        acc = jnp.dot(hx_ref[...], w_ref[:, cols], preferred_element_type=F32)
        if j * tn < q_width:
            acc = acc * (ATTN_SCALE * LOG2E)
        qk_ref[:, cols] = acc.astype(BF16)
    for j in range(uz_ref.shape[1] // tn):
        uz_ref[:, j * tn:(j + 1) * tn] = jnp.dot(hx_ref[...], w_ref[:, n_qk + j * tn:n_qk + (j + 1) * tn],
                                                 preferred_element_type=F32)
    for j in range(vt_ref.shape[0] // tn):
        vt_ref[j * tn:(j + 1) * tn, :] = lax.dot_general(
            wvt_ref[j * tn:(j + 1) * tn, :], hx_ref[...], (((1,), (1,)), ((), ())),
            preferred_element_type=F32).astype(BF16)


def _in_odd_kernel(x_ref, g_ref, sc_ref, sh_ref, w_ref, wvt_ref, cq_ref, sq_ref, ck_ref, sk_ref,
                   qk_ref, vt_ref, hx_ref, *, q_width):
    _in_prologue(x_ref, g_ref, sc_ref, sh_ref, hx_ref)
    for hp in range(qk_ref.shape[1] // MXU_N):
        c_ref, s_ref = (cq_ref, sq_ref) if hp * MXU_N < q_width else (ck_ref, sk_ref)
        y2 = jnp.dot(hx_ref[...], w_ref[:, hp * MXU_N:(hp + 1) * MXU_N], preferred_element_type=F32)
        for h in range(MXU_N // HEAD_DIM):
            y = y2[:, h * HEAD_DIM:(h + 1) * HEAD_DIM]
            cols = slice(hp * MXU_N + h * HEAD_DIM, hp * MXU_N + (h + 1) * HEAD_DIM)
            r = lax.rsqrt(jnp.mean(y * y, axis=-1, keepdims=True) + RMS_EPS)
            sw = pltpu.roll(y, HEAD_DIM // 2, 1)
            qk_ref[:, cols] = ((y * c_ref[...] + sw * s_ref[...]) * r).astype(BF16)
    vt_ref[...] = lax.dot_general(wvt_ref[...], hx_ref[...], (((1,), (1,)), ((), ())),
                                  preferred_element_type=F32).astype(BF16)


def _common_in_specs(tm, d, seg):
    return [pl.BlockSpec((tm, d), lambda i, *_: (i, 0)),
            pl.BlockSpec((1, d), lambda i, *_: (0, 0)),
            pl.BlockSpec((None, None, 1, d), lambda i, *_: (seg(i), 1, 0, 0)),
            pl.BlockSpec((None, None, 1, d), lambda i, *_: (seg(i), 0, 0, 0))]


def _in_proj_even(xs, g, mods, w, w_vt, seg, tm):
    nt, d = xs.shape
    n = w.shape[1]
    n_qk = 2 * NA_WIDTH
    v_w = w_vt.shape[0]
    resident = dict(pipeline_mode=pl.Buffered(1))
    return pl.pallas_call(
        functools.partial(_in_even_kernel, tn=512, q_width=NA_WIDTH),
        grid=(nt // tm,),
        in_specs=_common_in_specs(tm, d, seg) + [
            pl.BlockSpec((d, n), lambda i: (0, 0), **resident),
            pl.BlockSpec((v_w, d), lambda i: (0, 0), **resident)],
        out_specs=[pl.BlockSpec((tm, n_qk), lambda i: (i, 0)),
                   pl.BlockSpec((v_w, tm), lambda i: (0, i)),
                   pl.BlockSpec((tm, n - n_qk), lambda i: (i, 0))],
        out_shape=[jax.ShapeDtypeStruct((nt, n_qk), BF16),
                   jax.ShapeDtypeStruct((v_w, nt), BF16),
                   jax.ShapeDtypeStruct((nt, n - n_qk), F32)],
        scratch_shapes=[pltpu.VMEM((tm, d), BF16)],
        compiler_params=pltpu.CompilerParams(dimension_semantics=("parallel",),
                                             vmem_limit_bytes=VMEM_LIMIT_BIG),
        name="in_proj_even",
    )(xs, g.reshape(1, d), mods, mods, w, w_vt)


def _in_proj_odd(xs, g, mods, w_qk, w_vt, ctab, stab, seg, tm, tiles_per_batch, n_lat_tiles, q_width):
    nt, d = xs.shape
    n_qk = w_qk.shape[1]
    kv_w = w_vt.shape[0]
    rope_row = lambda i: jnp.where(i < n_lat_tiles, i % tiles_per_batch, tiles_per_batch)
    tab_spec = lambda which: pl.BlockSpec((None, tm, HEAD_DIM), lambda i: (which, rope_row(i), 0))
    resident = dict(pipeline_mode=pl.Buffered(1))
    return pl.pallas_call(
        functools.partial(_in_odd_kernel, q_width=q_width),
        grid=(nt // tm,),
        in_specs=_common_in_specs(tm, d, seg) + [
            pl.BlockSpec((d, n_qk), lambda i: (0, 0), **resident),
            pl.BlockSpec((kv_w, d), lambda i: (0, 0), **resident),
            tab_spec(0), tab_spec(0), tab_spec(1), tab_spec(1)],
        out_specs=[pl.BlockSpec((tm, n_qk), lambda i: (i, 0)),
                   pl.BlockSpec((kv_w, tm), lambda i: (0, i))],
        out_shape=[jax.ShapeDtypeStruct((nt, n_qk), BF16),
                   jax.ShapeDtypeStruct((kv_w, nt), BF16)],
        scratch_shapes=[pltpu.VMEM((tm, d), BF16)],
        compiler_params=_params(("parallel",)),
        name="in_proj_odd",
    )(xs, g.reshape(1, d), mods, mods, w_qk, w_vt, ctab, stab, ctab, stab)


def _na_kernel(q_ref, k_ref, vt_ref, qc_ref, kc_ref, vtc_ref, bias_ref, o_ref, oc_ref, *, n_rows, blocks_per_step):
    c = pl.program_id(2)
    bq = NA_BLOCK_ROWS * GRID_W
    span = NA_SPAN_ROWS * GRID_W
    n_blocks = n_rows // NA_BLOCK_ROWS
    dn = (((1,), (1,)), ((), ()))
    kc = kc_ref[...]
    vtc = vtc_ref[...]

    def span_start(j):
        blk = c * blocks_per_step + j
        kr_base = jnp.clip(blk * NA_BLOCK_ROWS - NA_WIN_ROWS // 2, 0, n_rows - NA_SPAN_ROWS)
        return blk, pl.multiple_of(kr_base * GRID_W, NA_BLOCK_ROWS * GRID_W)

    def scores(j):
        blk, start = span_start(j)
        kind = jnp.where(blk == 0, 0, jnp.where(blk == n_blocks - 1, 2, 1))
        q = q_ref[j * bq:(j + 1) * bq, :]
        s_nb = lax.dot_general(k_ref[pl.ds(start, span), :], q, dn, preferred_element_type=F32) + bias_ref[kind]
        s_cx = lax.dot_general(kc, q, dn, preferred_element_type=F32)
        return s_nb, s_cx

    def finish(j, s_nb, s_cx):
        _, start = span_start(j)
        m = jnp.maximum(s_nb.max(axis=0, keepdims=True), s_cx.max(axis=0, keepdims=True))
        p_nb = jnp.exp2(s_nb - m)
        p_cx = jnp.exp2(s_cx - m)
        l = p_nb.sum(axis=0, keepdims=True) + p_cx.sum(axis=0, keepdims=True)
        o = (jnp.dot(vt_ref[:, pl.ds(start, span)], p_nb.astype(BF16), preferred_element_type=F32)
             + jnp.dot(vtc, p_cx.astype(BF16), preferred_element_type=F32))
        o_ref[j * bq:(j + 1) * bq, :] = (o / l).T.astype(BF16)

    cur = scores(0)
    for j in range(blocks_per_step):
        nxt = scores(j + 1) if j + 1 < blocks_per_step else None
        finish(j, *cur)
        cur = nxt

    @pl.when(c == pl.num_programs(2) - 1)
    def _():
        s = lax.dot_general(kc, qc_ref[...], dn, preferred_element_type=F32)
        p = jnp.exp2(s - s.max(axis=0, keepdims=True))
        l = p.sum(axis=0, keepdims=True)
        o = jnp.dot(vtc, p.astype(BF16), preferred_element_type=F32)
        oc_ref[...] = (o / l).T.astype(BF16)


def _na_bias_table(rpb, n_rows):
    n_heads = rpb.shape[0]
    qc = jnp.arange(GRID_W)[:, None]
    kc = jnp.arange(GRID_W)[None, :]
    kc0 = jnp.clip(qc - NA_WIN_COLS // 2, 0, GRID_W - NA_WIN_COLS)
    valid = (kc >= kc0) & (kc < kc0 + NA_WIN_COLS)
    off_c = kc - qc + (NA_WIN_COLS - 1)
    col = jnp.zeros((n_heads, 2 * NA_WIN_ROWS - 1, GRID_W, GRID_W), F32)
    for o in range(2 * NA_WIN_COLS - 1):
        col = col + jnp.where((off_c == o)[None, None], rpb[:, :, o, None, None].astype(F32), 0.0)
    col = jnp.where(valid[None, None], col * LOG2E, MASK_VALUE).transpose(0, 1, 3, 2)
    tables = []
    for r0, kr_base in ((0, 0), (NA_BLOCK_ROWS, 0), (n_rows - NA_BLOCK_ROWS, n_rows - NA_SPAN_ROWS)):
        per_row = []
        for t in range(NA_BLOCK_ROWS):
            q_row = r0 + t
            win0 = min(max(q_row - NA_WIN_ROWS // 2, 0), n_rows - NA_WIN_ROWS)
            u_lo = win0 - kr_base
            off_lo = win0 - q_row + (NA_WIN_ROWS - 1)
            assert 0 <= u_lo <= NA_SPAN_ROWS - NA_WIN_ROWS
            per_row.append(jnp.pad(col[:, off_lo:off_lo + NA_WIN_ROWS],
                                   ((0, 0), (u_lo, NA_SPAN_ROWS - NA_WIN_ROWS - u_lo), (0, 0), (0, 0)),
                                   constant_values=MASK_VALUE))
        tb = jnp.stack(per_row, axis=3)
        tables.append(tb.reshape(n_heads, NA_SPAN_ROWS * GRID_W, NA_BLOCK_ROWS * GRID_W))
    return jnp.stack(tables)


def _na_attention(qk, vt, bias_tbl, n_batch, seq, ctx_len):
    n_rows = seq // GRID_W
    bq = NA_BLOCK_ROWS * GRID_W
    span = NA_SPAN_ROWS * GRID_W
    step_rows = min(2048, seq)
    assert ctx_len == bq and n_rows % NA_BLOCK_ROWS == 0 and n_rows >= NA_SPAN_ROWS + NA_BLOCK_ROWS
    steps = seq // step_rows
    ctx_blk0 = n_batch * seq // ctx_len
    h_k = NA_HEADS
    return pl.pallas_call(
        functools.partial(_na_kernel, n_rows=n_rows, blocks_per_step=step_rows // bq),
        grid=(n_batch, NA_HEADS, steps),
        in_specs=[pl.BlockSpec((step_rows, HEAD_DIM), lambda b, h, c: (b * steps + c, h)),
                  pl.BlockSpec((seq, HEAD_DIM), lambda b, h, c: (b, h_k + h)),
                  pl.BlockSpec((HEAD_DIM, seq), lambda b, h, c: (h, b)),
                  pl.BlockSpec((ctx_len, HEAD_DIM), lambda b, h, c: (ctx_blk0 + b, h)),
                  pl.BlockSpec((ctx_len, HEAD_DIM), lambda b, h, c: (ctx_blk0 + b, h_k + h)),
                  pl.BlockSpec((HEAD_DIM, ctx_len), lambda b, h, c: (h, ctx_blk0 + b)),
                  pl.BlockSpec((3, None, span, bq), lambda b, h, c: (0, h, 0, 0))],
        out_specs=[pl.BlockSpec((step_rows, HEAD_DIM), lambda b, h, c: (b * steps + c, h)),
                   pl.BlockSpec((ctx_len, HEAD_DIM), lambda b, h, c: (b, h))],
        out_shape=[jax.ShapeDtypeStruct((n_batch * seq, NA_WIDTH), BF16),
                   jax.ShapeDtypeStruct((n_batch * ctx_len, NA_WIDTH), BF16)],
        compiler_params=_params(("parallel", "parallel", "arbitrary")),
        name="neighbourhood_attention",
    )(qk, qk, vt, qk, qk, vt, bias_tbl)


def _sg_kernel(uz_ref, ws_ref, bs_ref, lg_ref, lb_ref, o_ref, *, n_chunks):
    for g in range(SG_GROUPS):
        w = ws_ref[g].astype(BF16)
        for n in range(n_chunks):
            rows = slice(n * SG_CHUNK, (n + 1) * SG_CHUNK)
            u = uz_ref[rows, g * SG_DIM:(g + 1) * SG_DIM]
            z = uz_ref[rows, SG_WIDTH + g * SG_DIM:SG_WIDTH + (g + 1) * SG_DIM]
            zz = _gelu_tanh(z)
            mu = jnp.mean(zz, axis=-1, keepdims=True)
            xc = zz - mu
            var = jnp.mean(xc * xc, axis=-1, keepdims=True)
            zn = xc * lax.rsqrt(var + LN_EPS) * lg_ref[g] + lb_ref[g]
            mixed = jnp.dot(w, zn.astype(BF16), preferred_element_type=F32) + bs_ref[g]
            o_ref[rows, g * SG_DIM:(g + 1) * SG_DIM] = (_gelu_tanh(u) * mixed).astype(BF16)


def _spatial_gating(uz, w_s, b_s, ln_g, ln_b, tm):
    nt = uz.shape[0]
    full3 = lambda i: (0, 0, 0)
    return pl.pallas_call(
        functools.partial(_sg_kernel, n_chunks=tm // SG_CHUNK),
        grid=(nt // tm,),
        in_specs=[pl.BlockSpec((tm, 2 * SG_WIDTH), lambda i: (i, 0)),
                  pl.BlockSpec((SG_GROUPS, SG_CHUNK, SG_CHUNK), full3),
                  pl.BlockSpec((SG_GROUPS, SG_CHUNK, 1), full3),
                  pl.BlockSpec((SG_GROUPS, 1, SG_DIM), full3),
                  pl.BlockSpec((SG_GROUPS, 1, SG_DIM), full3)],
        out_specs=pl.BlockSpec((tm, SG_WIDTH), lambda i: (i, 0)),
        out_shape=jax.ShapeDtypeStruct((nt, SG_WIDTH), BF16),
        compiler_params=_params(("parallel",)),
        name="spatial_gating",
    )(uz, w_s, b_s.reshape(SG_GROUPS, SG_CHUNK, 1), ln_g.reshape(SG_GROUPS, 1, SG_DIM),
      ln_b.reshape(SG_GROUPS, 1, SG_DIM))


def _flash_kernel(q_ref, k_ref, vt_ref, kc_ref, vtc_ref, o_ref, q_s, s_a, s_b, s_c, acc_s, m_s,
                  *, tq, tk, seq, group, n_lat_q):
    qi = pl.program_id(2)
    for g in range(group):
        q_s[g * tq:(g + 1) * tq, :] = q_ref[:, g * HEAD_DIM:(g + 1) * HEAD_DIM]
    m_rows = group * tq
    n_blk = m_rows // MXU_N
    n_tiles = seq // tk
    dn = (((1,), (1,)), ((), ()))
    blk_cols = lambda n: slice(n * MXU_N, (n + 1) * MXU_N)

    def scores(k, s_ref, n):
        s_ref[:, blk_cols(n)] = lax.dot_general(k, q_s[blk_cols(n), :], dn, preferred_element_type=F32)

    def with_ones_row(vt):
        ones = jnp.where(lax.broadcasted_iota(jnp.int32, (BF16_ROWS, vt.shape[1]), 0) == 0, 1.0, 0.0)
        return jnp.concatenate([vt, ones.astype(BF16)], axis=0)

    def consume(s_ref, vt_aug, n):
        cols = blk_cols(n)
        s = s_ref[:, cols]
        m_old = m_s[:, cols]
        m_new = jnp.maximum(m_old, s.max(axis=0, keepdims=True))
        alpha = jnp.exp2(m_old - m_new)
        p = jnp.exp2(s - m_new).astype(BF16)
        acc_s[:, cols] = alpha * acc_s[:, cols] + jnp.dot(vt_aug, p, preferred_element_type=F32)
        m_s[:, cols] = m_new

    def scores_and_consume(k_next, s_next, s_cur, vt_cur):
        vt_aug = with_ones_row(vt_cur)
        for n in range(n_blk):
            scores(k_next, s_next, n)
            consume(s_cur, vt_aug, n)

    k_tile = lambda t: k_ref[pl.ds(pl.multiple_of(t * tk, tk), tk), :]
    vt_tile = lambda t: vt_ref[:, pl.ds(pl.multiple_of(t * tk, tk), tk)]

    m_s[...] = jnp.full(m_s.shape, MASK_VALUE, F32)
    acc_s[...] = jnp.zeros(acc_s.shape, F32)

    @pl.when(qi < n_lat_q)
    def _():
        for n in range(n_blk):
            scores(k_tile(0), s_a, n)

        def pair(u, carry):
            t0 = 2 * u
            scores_and_consume(k_tile(t0 + 1), s_b, s_a, vt_tile(t0))
            scores_and_consume(k_tile(t0 + 2), s_a, s_b, vt_tile(t0 + 1))
            return carry

        lax.fori_loop(0, n_tiles // 2 - 1, pair, 0)
        scores_and_consume(k_tile(n_tiles - 1), s_b, s_a, vt_tile(n_tiles - 2))
        scores_and_consume(kc_ref[...], s_c, s_b, vt_tile(n_tiles - 1))
        vtc_aug = with_ones_row(vtc_ref[...])
        for n in range(n_blk):
            consume(s_c, vtc_aug, n)

    @pl.when(qi == n_lat_q)
    def _():
        vtc_aug = with_ones_row(vtc_ref[...])
        for n in range(n_blk):
            scores(kc_ref[...], s_c, n)
            consume(s_c, vtc_aug, n)

    o = acc_s[:HEAD_DIM, :] / acc_s[HEAD_DIM:HEAD_DIM + 1, :]
    for g in range(group):
        o_ref[:, g * HEAD_DIM:(g + 1) * HEAD_DIM] = o[:, g * tq:(g + 1) * tq].T.astype(BF16)


def _flash_attention(qk, vt, n_batch, seq, ctx_len, n_q_heads):
    nt = qk.shape[0]
    group = n_q_heads // GQA_KV_HEADS
    tq = ctx_len
    tk = min(FLASH_KEY_TILE, seq // 2)
    assert seq % (2 * tk) == 0 and (group * tq) % MXU_N == 0
    n_lat_q = seq // tq
    ctx_blk0 = n_batch * seq // tq
    hk0 = n_q_heads
    gw = group * HEAD_DIM
    m_rows = group * tq

    def q_idx(b, kh, qi):
        return (jnp.where(qi < n_lat_q, b * n_lat_q + qi, ctx_blk0 + b), kh)

    return pl.pallas_call(
        functools.partial(_flash_kernel, tq=tq, tk=tk, seq=seq, group=group, n_lat_q=n_lat_q),
        grid=(n_batch, GQA_KV_HEADS, n_lat_q + 1),
        in_specs=[pl.BlockSpec((tq, gw), q_idx),
                  pl.BlockSpec((seq, HEAD_DIM), lambda b, kh, qi: (b, hk0 + kh)),
                  pl.BlockSpec((HEAD_DIM, seq), lambda b, kh, qi: (kh, b)),
                  pl.BlockSpec((tq, HEAD_DIM), lambda b, kh, qi: (ctx_blk0 + b, hk0 + kh)),
                  pl.BlockSpec((HEAD_DIM, tq), lambda b, kh, qi: (kh, ctx_blk0 + b))],
        out_specs=pl.BlockSpec((tq, gw), q_idx),
        out_shape=jax.ShapeDtypeStruct((nt, n_q_heads * HEAD_DIM), BF16),
        scratch_shapes=[pltpu.VMEM((m_rows, HEAD_DIM), BF16),
                        pltpu.VMEM((tk, m_rows), F32), pltpu.VMEM((tk, m_rows), F32),
                        pltpu.VMEM((tq, m_rows), F32),
                        pltpu.VMEM((HEAD_DIM + BF16_ROWS, m_rows), F32),
                        pltpu.VMEM((1, m_rows), F32)],
        compiler_params=_params(("parallel", "parallel", "arbitrary")),
        name="gqa_flash_attention",
    )(qk, qk, vt, qk, vt)


def _out_kernel(*refs, n_lhs):
    lhs_refs = refs[:n_lhs]
    (w_ref, x_ref, ga_ref, g2_ref, sc_ref, sh_ref, wr_ref, br_ref, xo_ref, tok_ref, lg_ref) = refs[n_lhs:]
    acc = None
    k0 = 0
    for a_ref in lhs_refs:
        kp = a_ref.shape[1]
        part = jnp.dot(a_ref[...], w_ref[k0:k0 + kp, :], preferred_element_type=F32)
        acc = part if acc is None else acc + part
        k0 += kp
    xn = x_ref[...] + ga_ref[...] * acc
    xo_ref[...] = xn
    tok = _rms_modulate(xn, g2_ref[...], sc_ref[...], sh_ref[...])
    tok_ref[...] = _pack_bf16_pairs(tok)
    lg_ref[...] = lax.dot_general(wr_ref[...], tok.astype(BF16), (((1,), (1,)), ((), ())),
                                  preferred_element_type=F32) + br_ref[...]


def _out_proj(lhs, w, xs, mods, g2, w_router, b_router, seg, tm):
    nt, d = xs.shape
    k = w.shape[0]
    modspec = lambda which: pl.BlockSpec((None, None, 1, d), lambda i: (seg(i), which, 0, 0))
    return pl.pallas_call(
        functools.partial(_out_kernel, n_lhs=len(lhs)),
        grid=(nt // tm,),
        in_specs=[pl.BlockSpec((tm, a.shape[1]), lambda i: (i, 0)) for a in lhs] + [
            pl.BlockSpec((k, d), lambda i: (0, 0)),
            pl.BlockSpec((tm, d), lambda i: (i, 0)),
            modspec(2),
            pl.BlockSpec((1, d), lambda i: (0, 0)),
            modspec(4),
            modspec(3),
            pl.BlockSpec((ROUTER_W, d), lambda i: (0, 0)),
            pl.BlockSpec((ROUTER_W, 1), lambda i: (0, 0))],
        out_specs=[pl.BlockSpec((tm, d), lambda i: (i, 0)),
                   pl.BlockSpec((tm, d // 2), lambda i: (i, 0)),
                   pl.BlockSpec((ROUTER_W, tm), lambda i: (0, i))],
        out_shape=[jax.ShapeDtypeStruct((nt, d), F32),
                   jax.ShapeDtypeStruct((nt, d // 2), jnp.uint32),
                   jax.ShapeDtypeStruct((ROUTER_W, nt), F32)],
        compiler_params=_params(("parallel",)),
        name="out_proj_residual_router",
    )(*lhs, w, xs, mods, g2.reshape(1, d), mods, mods, w_router, b_router)


def _expert_kernel(blk_e_ref, n_used_ref, xs_ref, w1_ref, w3_ref, w2_ref, o_ref):
    b = pl.program_id(0)

    @pl.when(b < n_used_ref[0])
    def _():
        x = _unpack_bf16_pairs(xs_ref[...]).astype(BF16)
        h1 = jnp.dot(x, w1_ref[...].astype(BF16), preferred_element_type=F32)
        h3 = jnp.dot(x, w3_ref[...].astype(BF16), preferred_element_type=F32)
        a = (h1 * _sigmoid(h1) * h3).astype(BF16)
        o_ref[...] = _pack_bf16_pairs(jnp.dot(a, w2_ref[...].astype(BF16), preferred_element_type=F32))

    @pl.when(b >= n_used_ref[0])
    def _():
        o_ref[...] = jnp.zeros_like(o_ref)


def _experts(xs, blk_e, n_used, w1, w3, w2, layer):
    p, dw = xs.shape
    d, hid = w1.shape[-2:]
    nb = p // MOE_ROWS
    grid_spec = pltpu.PrefetchScalarGridSpec(
        num_scalar_prefetch=2,
        grid=(nb,),
        in_specs=[pl.BlockSpec((MOE_ROWS, dw), lambda b, be, nu: (jnp.minimum(b, nu[0] - 1), 0)),
                  pl.BlockSpec((None, None, d, hid), lambda b, be, nu: (layer, be[b], 0, 0)),
                  pl.BlockSpec((None, None, d, hid), lambda b, be, nu: (layer, be[b], 0, 0)),
                  pl.BlockSpec((None, None, hid, d), lambda b, be, nu: (layer, be[b], 0, 0))],
        out_specs=pl.BlockSpec((MOE_ROWS, dw), lambda b, be, nu: (b, 0)),
    )
    return pl.pallas_call(
        _expert_kernel,
        grid_spec=grid_spec,
        out_shape=jax.ShapeDtypeStruct((p, dw), jnp.uint32),
        compiler_params=_params(("arbitrary",)),
        name="moe_experts",
    )(blk_e, n_used, xs, w1, w3, w2)


def _first_index(vals, target):
    idx = jnp.full(target.shape, len(vals) - 1, jnp.int32)
    for i in range(len(vals) - 2, -1, -1):
        idx = jnp.where(vals[i] == target, i, idx)
    return idx


def _router_kernel(lg_ref, gate_ref, pos_ref, blk_ref, nused_ref, e_s, rank_s, *, n_tiles):
    tw = ROUTE_TILE
    tri = jnp.where(lax.broadcasted_iota(jnp.int32, (tw, tw), 0) < lax.broadcasted_iota(jnp.int32, (tw, tw), 1),
                    1.0, 0.0).astype(BF16)
    eid = lax.broadcasted_iota(jnp.int32, (MOE_EXPERTS, tw), 0)
    epg = MOE_EXPERTS_PER_GROUP

    def pass1(c, run):
        sl = pl.ds(pl.multiple_of(c * tw, tw), tw)
        g = [lg_ref[i:i + 1, sl] for i in range(MOE_GROUPS)]
        gm = functools.reduce(jnp.maximum, g)
        gidx = _first_index(g, gm)
        gval = 1.0 / functools.reduce(lambda a, b: a + b, [jnp.exp(gi - gm) for gi in g])
        le = []
        for e in range(epg):
            sel = lg_ref[MOE_GROUPS + (MOE_GROUPS - 1) * epg + e:MOE_GROUPS + (MOE_GROUPS - 1) * epg + e + 1, sl]
            for gg in range(MOE_GROUPS - 2, -1, -1):
                sel = jnp.where(gidx == gg, lg_ref[MOE_GROUPS + gg * epg + e:MOE_GROUPS + gg * epg + e + 1, sl], sel)
            le.append(sel)
        m1 = functools.reduce(jnp.maximum, le)
        i1 = _first_index(le, m1)
        le2 = [jnp.where(i1 == e, -jnp.inf, le[e]) for e in range(epg)]
        m2 = functools.reduce(jnp.maximum, le2)
        i2 = _first_index(le2, m2)
        t = jnp.exp(m2 - m1)
        inv = 1.0 / (1.0 + t)
        gate_ref[0:1, sl] = inv * gval
        gate_ref[1:2, sl] = t * inv * gval
        e0 = gidx * epg + i1
        e1 = gidx * epg + i2
        oh0 = eid == e0
        oh1 = eid == e1
        oh0f = jnp.where(oh0, 1.0, 0.0)
        oh1f = jnp.where(oh1, 1.0, 0.0)
        pre0 = jnp.dot(oh0f.astype(BF16), tri, preferred_element_type=F32)
        pre1 = jnp.dot(oh1f.astype(BF16), tri, preferred_element_type=F32)
        c0 = oh0f.sum(axis=1, keepdims=True)
        c1 = oh1f.sum(axis=1, keepdims=True)
        rank_s[0:1, sl] = jnp.where(oh0, run + pre0, 0.0).sum(axis=0, keepdims=True)
        rank_s[1:2, sl] = jnp.where(oh1, run + c0 + pre1, 0.0).sum(axis=0, keepdims=True)
        e_s[0:1, sl] = e0
        e_s[1:2, sl] = e1
        return run + c0 + c1

    counts = lax.fori_loop(0, n_tiles, pass1, jnp.zeros((MOE_EXPERTS, 1), F32))
    blocks = jnp.floor((counts + (MOE_ROWS - 1)) * (1.0 / MOE_ROWS))
    lincl = jnp.where(lax.broadcasted_iota(jnp.int32, (MOE_EXPERTS, MOE_EXPERTS), 1)
                      <= lax.broadcasted_iota(jnp.int32, (MOE_EXPERTS, MOE_EXPERTS), 0), 1.0, 0.0).astype(BF16)
    end_blocks = jnp.dot(lincl, jnp.broadcast_to(blocks, (MOE_EXPERTS, LANES)).astype(BF16),
                         preferred_element_type=F32)[:, 0:1]
    start_rows = (end_blocks - blocks) * MOE_ROWS

    def pass2(c, carry):
        sl = pl.ds(pl.multiple_of(c * tw, tw), tw)
        for k in range(MOE_TOP_K):
            base = jnp.where(eid == e_s[k:k + 1, sl], start_rows, 0.0).sum(axis=0, keepdims=True)
            pos_ref[k:k + 1, sl] = (base + rank_s[k:k + 1, sl]).astype(jnp.int32)
        return carry

    lax.fori_loop(0, n_tiles, pass2, 0)
    bl = lax.broadcasted_iota(jnp.int32, (MOE_EXPERTS, PLAN_LANES), 1).astype(F32)
    blk = jnp.where(end_blocks <= bl, 1.0, 0.0).sum(axis=0, keepdims=True)
    blk_ref[...] = jnp.minimum(blk, MOE_EXPERTS - 1.0).astype(jnp.int32)
    nused_ref[...] = jnp.broadcast_to(end_blocks[MOE_EXPERTS - 1:MOE_EXPERTS, :], (1, LANES)).astype(jnp.int32)


def _router(lg_t):
    n = lg_t.shape[1]
    assert n % ROUTE_TILE == 0
    return pl.pallas_call(
        functools.partial(_router_kernel, n_tiles=n // ROUTE_TILE),
        out_shape=[jax.ShapeDtypeStruct((MOE_TOP_K, n), F32),
                   jax.ShapeDtypeStruct((MOE_TOP_K, n), jnp.int32),
                   jax.ShapeDtypeStruct((1, PLAN_LANES), jnp.int32),
                   jax.ShapeDtypeStruct((1, LANES), jnp.int32)],
        scratch_shapes=[pltpu.VMEM((MOE_TOP_K, n), jnp.int32), pltpu.VMEM((MOE_TOP_K, n), F32)],
        compiler_params=pltpu.CompilerParams(vmem_limit_bytes=VMEM_LIMIT),
        name="moe_router_plan",
    )(lg_t)


def _dispatch_kernel(p0_ref, p1_ref, tok_ref, init_ref, xs_ref, sem, *, tm):
    del init_ref
    base = pl.program_id(0) * tm

    def copy(r, p_ref):
        return pltpu.make_async_copy(tok_ref.at[pl.ds(r, 1)], xs_ref.at[pl.ds(p_ref[base + r], 1)], sem)

    def start(r, carry):
        copy(r, p0_ref).start()
        copy(r, p1_ref).start()
        return carry

    def wait(r, carry):
        copy(r, p0_ref).wait()
        copy(r, p1_ref).wait()
        return carry

    lax.fori_loop(0, tm, start, 0, unroll=8)
    lax.fori_loop(0, tm, wait, 0, unroll=8)


def _dispatch(tok, pos0, pos1, n_slots, tm):
    nt, d = tok.shape
    grid_spec = pltpu.PrefetchScalarGridSpec(
        num_scalar_prefetch=2,
        grid=(nt // tm,),
        in_specs=[pl.BlockSpec((tm, d), lambda i, p0, p1: (i, 0)),
                  pl.BlockSpec(memory_space=pl.ANY)],
        out_specs=pl.BlockSpec(memory_space=pl.ANY),
        scratch_shapes=[pltpu.SemaphoreType.DMA(())],
    )
    return pl.pallas_call(
        functools.partial(_dispatch_kernel, tm=tm),
        grid_spec=grid_spec,
        out_shape=jax.ShapeDtypeStruct((n_slots, d), tok.dtype),
        input_output_aliases={3: 0},
        compiler_params=_params(("arbitrary",)),
        name="moe_dispatch",
    )(pos0, pos1, tok, jnp.zeros((n_slots, d), tok.dtype))


def _combine_kernel(p0_ref, p1_ref, x_ref, yb_ref, gt_ref, ga_ref, *rest, tm, final):
    if final:
        gf_ref, o_ref, y0_s, y1_s, sem = rest
    else:
        o_ref, y0_s, y1_s, sem = rest
    base = pl.program_id(0) * tm

    def copy(r, p_ref, y_s):
        return pltpu.make_async_copy(yb_ref.at[pl.ds(p_ref[base + r], 1)], y_s.at[pl.ds(r, 1)], sem)

    def start(r, carry):
        copy(r, p0_ref, y0_s).start()
        copy(r, p1_ref, y1_s).start()
        return carry

    def wait(r, carry):
        copy(r, p0_ref, y0_s).wait()
        copy(r, p1_ref, y1_s).wait()
        return carry

    lax.fori_loop(0, tm, start, 0, unroll=8)
    lax.fori_loop(0, tm, wait, 0, unroll=8)
    gt = gt_ref[...]
    f = gt[:, 0:1] * _unpack_bf16_pairs(y0_s[...]) + gt[:, 1:2] * _unpack_bf16_pairs(y1_s[...])
    xn = x_ref[...] + ga_ref[...] * f
    if final:
        r = lax.rsqrt(jnp.mean(xn * xn, axis=-1, keepdims=True) + RMS_EPS)
        xn = xn * r * gf_ref[...]
    o_ref[...] = xn


def _combine(xs, yb, pos0, pos1, gate, mods, seg, tm, final_g=None, n_out_rows=None):
    nt, d = xs.shape
    rows = nt if n_out_rows is None else n_out_rows
    row_blk = lambda i, p0, p1: (i, 0)
    in_specs = [pl.BlockSpec((tm, d), row_blk),
                pl.BlockSpec(memory_space=pl.ANY),
                pl.BlockSpec((tm, MOE_TOP_K), row_blk),
                pl.BlockSpec((None, None, 1, d), lambda i, p0, p1: (seg(i), 5, 0, 0))]
    args = [xs, yb, gate, mods]
    if final_g is not None:
        in_specs.append(pl.BlockSpec((1, d), lambda i, p0, p1: (0, 0)))
        args.append(final_g.reshape(1, d))
    grid_spec = pltpu.PrefetchScalarGridSpec(
        num_scalar_prefetch=2,
        grid=(rows // tm,),
        in_specs=in_specs,
        out_specs=pl.BlockSpec((tm, d), row_blk),
        scratch_shapes=[pltpu.VMEM((tm, d // 2), jnp.uint32), pltpu.VMEM((tm, d // 2), jnp.uint32),
                        pltpu.SemaphoreType.DMA(())],
    )
    return pl.pallas_call(
        functools.partial(_combine_kernel, tm=tm, final=final_g is not None),
        grid_spec=grid_spec,
        out_shape=jax.ShapeDtypeStruct((rows, d), F32),
        compiler_params=_params(("arbitrary",)),
        name="moe_combine_residual",
    )(pos0, pos1, *args)


def _rope_tables(seq, pad_rows):
    t = jnp.arange(seq, dtype=jnp.int32)
    row = (t // GRID_W).astype(F32)
    col = (t % GRID_W).astype(F32)
    inv_freq = ROPE_THETA ** (-jnp.arange(ROPE_AXIS_DIM // 2, dtype=F32) * 2.0 / ROPE_AXIS_DIM)
    ang = jnp.concatenate([row[:, None] * inv_freq, col[:, None] * inv_freq], axis=-1)
    cos, sin = jnp.cos(ang), jnp.sin(ang)
    cos = jnp.concatenate([cos, jnp.ones((pad_rows, HEAD_DIM // 2), F32)], axis=0)
    sin = jnp.concatenate([sin, jnp.zeros((pad_rows, HEAD_DIM // 2), F32)], axis=0)
    return cos, sin


def _split_pairs(w, n_heads):
    d = w.shape[0]
    return w.reshape(d, n_heads, HEAD_DIM // 2, 2).transpose(0, 1, 3, 2).reshape(d, n_heads * HEAD_DIM)


def _gain_rope_tables(cos, sin, q_gain, k_gain):
    def tables(g):
        ge, go = g[0::2][None, :], g[1::2][None, :]
        return (jnp.concatenate([ge * cos, go * cos], axis=1),
                jnp.concatenate([-go * sin, ge * sin], axis=1))
    cq, sq = tables(q_gain)
    ck, sk = tables(k_gain)
    return jnp.stack([cq, ck]), jnp.stack([sq, sk])


def kernel(x, c, ctx, c_ctx, mod_w, mod_b, norm_mix_g, norm_ffn_g, norm_final_g, na_sg_w_in, na_sg_w_out, na_rpb, sg_w_s, sg_b_s, sg_ln_g, sg_ln_b, gqa_w_in, gqa_w_out, gqa_q_gain, gqa_k_gain, moe_w_group, moe_b_group, moe_w_expert, moe_b_expert, moe_w1, moe_w3, moe_w2):
    n_batch, seq, d = x.shape
    ctx_len = ctx.shape[1]
    depth = mod_w.shape[0]
    n_lat = n_batch * seq
    tm = min(ROW_TILE, ctx_len * n_batch)
    assert seq % tm == 0 and (n_batch * ctx_len) % tm == 0 and seq % ctx_len == 0
    assert ctx_len % GRID_W == 0 and n_batch + 1 <= 8
    tiles_per_batch = seq // tm
    seg = _seg_fn(tiles_per_batch, n_batch)
    n_q_heads = d // HEAD_DIM
    n_tok = n_lat + n_batch * ctx_len
    n_blocks = -(-n_tok * MOE_TOP_K // MOE_ROWS) + MOE_EXPERTS
    assert n_blocks <= PLAN_LANES and n_tok % ROUTE_TILE == 0

    xs = jnp.concatenate([x.reshape(n_lat, d), ctx.reshape(n_batch * ctx_len, d)], axis=0)
    cond = jnp.zeros((8, d), F32).at[:n_batch].set(c).at[n_batch].set(c_ctx)
    mod_all = _modulation(cond, mod_w, mod_b)
    cos, sin = _rope_tables(seq, tm)

    q_w = n_q_heads * HEAD_DIM
    kv_w = GQA_KV_HEADS * HEAD_DIM
    out = None
    for layer in range(depth):
        last = layer == depth - 1
        mods = mod_all[layer, :n_batch + 1].reshape(n_batch + 1, 6, 1, d)
        i = layer // 2
        if layer % 2 == 0:
            w_in = na_sg_w_in[i]
            w_main = jnp.concatenate([w_in[:, :2 * NA_WIDTH], w_in[:, 3 * NA_WIDTH:]], axis=1).astype(BF16)
            w_vt = w_in[:, 2 * NA_WIDTH:3 * NA_WIDTH].T.astype(BF16)
            qk, vt, uz = _in_proj_even(xs, norm_mix_g[layer], mods, w_main, w_vt, seg, tm)
            att_lat, att_ctx = _na_attention(qk, vt, _na_bias_table(na_rpb[i], seq // GRID_W), n_batch, seq, ctx_len)
            att = jnp.concatenate([att_lat, att_ctx], axis=0)
            gat = _spatial_gating(uz, sg_w_s[i], sg_b_s[i], sg_ln_g[i], sg_ln_b[i], tm)
            lhs, w_out = [att, gat], na_sg_w_out[i].astype(BF16)
        else:
            ctab, stab = _gain_rope_tables(cos, sin, gqa_q_gain[i] * (ATTN_SCALE * LOG2E), gqa_k_gain[i])
            w_qk = _split_pairs(gqa_w_in[i][:, :q_w + kv_w], n_q_heads + GQA_KV_HEADS).astype(BF16)
            w_vt = gqa_w_in[i][:, q_w + kv_w:].T.astype(BF16)
            qk, vt = _in_proj_odd(xs, norm_mix_g[layer], mods, w_qk, w_vt, ctab, stab,
                                  seg, tm, tiles_per_batch, n_lat // tm, q_w)
            att = _flash_attention(qk, vt, n_batch, seq, ctx_len, n_q_heads)
            lhs, w_out = [att], gqa_w_out[i].astype(BF16)

        pad = ROUTER_W - MOE_GROUPS - MOE_EXPERTS
        w_router = jnp.concatenate([moe_w_group[layer], moe_w_expert[layer], jnp.zeros((d, pad), F32)],
                                   axis=1).T.astype(BF16)
        b_router = jnp.concatenate([moe_b_group[layer], moe_b_expert[layer], jnp.zeros((pad,), F32)])[:, None]
        xs, tok, lg_t = _out_proj(lhs, w_out, xs, mods, norm_ffn_g[layer], w_router, b_router, seg, tm)

        gate, pos, blk, nused = _router(lg_t)
        pos0, pos1 = pos[0], pos[1]
        slots = _dispatch(tok, pos0, pos1, n_blocks * MOE_ROWS, tm)
        yb = _experts(slots, blk[0, :n_blocks], nused[0, :1], moe_w1, moe_w3, moe_w2, layer)
        if last:
            out = _combine(xs, yb, pos0, pos1, gate.T, mods, seg, tm, final_g=norm_final_g, n_out_rows=n_lat)
        else:
            xs = _combine(xs, yb, pos0, pos1, gate.T, mods, seg, tm)
    return out.reshape(n_batch, seq, d)
```

```python
import functools

import jax
import jax.numpy as jnp
from jax import lax
from jax.experimental import pallas as pl
from jax.experimental.pallas import tpu as pltpu

F32 = jnp.float32
BF16 = jnp.bfloat16

GRID_W = 64
HEAD_DIM = 128
NA_HEADS = 8
NA_WIDTH = NA_HEADS * HEAD_DIM
NA_WIN_ROWS = 8
NA_WIN_COLS = 16
NA_BLOCK_ROWS = 4
NA_SPAN_ROWS = 12
SG_GROUPS = 8
SG_DIM = 128
SG_WIDTH = SG_GROUPS * SG_DIM
SG_CHUNK = 128
GQA_KV_HEADS = 4
ROPE_THETA = 10000.0
ROPE_AXIS_DIM = HEAD_DIM // 2
MOE_GROUPS = 4
MOE_EXPERTS_PER_GROUP = 8
MOE_EXPERTS = MOE_GROUPS * MOE_EXPERTS_PER_GROUP
MOE_TOP_K = 2
RMS_EPS = 1e-6
LN_EPS = 1e-5

LOG2E = 1.4426950408889634
MASK_VALUE = -1e30
ATTN_SCALE = HEAD_DIM ** -0.5
LANES = 128
MXU_N = 256
BF16_ROWS = 16
ROUTER_W = LANES
ROW_TILE = 512
MOE_ROWS = 512
FLASH_KEY_TILE = 1024
ROUTE_TILE = 512
PLAN_LANES = 256
VMEM_LIMIT = 48 * 1024 * 1024
VMEM_LIMIT_BIG = 56 * 1024 * 1024


def _params(sem):
    return pltpu.CompilerParams(dimension_semantics=sem, vmem_limit_bytes=VMEM_LIMIT)


def _sigmoid(x):
    return 1.0 / (1.0 + jnp.exp(-x))


def _gelu_tanh(x):
    cdf = 0.5 * (1.0 + jnp.tanh(0.7978845608028654 * (x + 0.044715 * (x * x * x))))
    return x * cdf


def _pack_bf16_pairs(x):
    c = x.shape[1] // 2
    bits = pltpu.bitcast(x.astype(jnp.bfloat16).astype(F32), jnp.uint32)
    return (bits[:, :c] >> 16) | bits[:, c:]


def _unpack_bf16_pairs(w):
    lo = pltpu.bitcast(w << 16, F32)
    hi = pltpu.bitcast(w & jnp.uint32(0xFFFF0000), F32)
    return jnp.concatenate([lo, hi], axis=1)


def _rms_modulate(x, g, sc, sh):
    r = lax.rsqrt(jnp.mean(x * x, axis=-1, keepdims=True) + RMS_EPS)
    return (x * r * g) * (1.0 + sc) + sh


def _mod_kernel(c_ref, w_ref, b_ref, o_ref):
    c = c_ref[...]
    cs = (c * _sigmoid(c)).astype(BF16)
    o_ref[0] = jnp.dot(cs, w_ref[0].astype(BF16), preferred_element_type=F32) + b_ref[0]


def _modulation(cond, mod_w, mod_b):
    depth, d, n = mod_w.shape
    tn = 1024
    return pl.pallas_call(
        _mod_kernel,
        grid=(depth, n // tn),
        in_specs=[pl.BlockSpec((8, d), lambda l, j: (0, 0)),
                  pl.BlockSpec((1, d, tn), lambda l, j: (l, 0, j)),
                  pl.BlockSpec((1, 1, tn), lambda l, j: (l, 0, j))],
        out_specs=pl.BlockSpec((1, 8, tn), lambda l, j: (l, 0, j)),
        out_shape=jax.ShapeDtypeStruct((depth, 8, n), F32),
        compiler_params=_params(("parallel", "parallel")),
        name="adaln_modulation",
    )(cond, mod_w, mod_b.reshape(depth, 1, n))


def _seg_fn(tiles_per_batch, n_batch):
    return lambda i: jnp.minimum(i // tiles_per_batch, n_batch)


def _in_prologue(x_ref, g_ref, sc_ref, sh_ref, hx_ref):
    hx_ref[...] = _rms_modulate(x_ref[...], g_ref[...], sc_ref[...], sh_ref[...]).astype(BF16)


def _in_even_kernel(x_ref, g_ref, sc_ref, sh_ref, w_ref, wvt_ref, qk_ref, vt_ref, uz_ref, hx_ref, *, tn, q_width):
    _in_prologue(x_ref, g_ref, sc_ref, sh_ref, hx_ref)
    n_qk = qk_ref.shape[1]
    for j in range(n_qk // tn):
        cols = slice(j * tn, (j + 1) * tn)
        acc = jnp.dot(hx_ref[...], w_ref[:, cols], preferred_element_type=F32)
        if j * tn < q_width:
            acc = acc * (ATTN_SCALE * LOG2E)
        qk_ref[:, cols] = acc.astype(BF16)
    for j in range(uz_ref.shape[1] // tn):
        uz_ref[:, j * tn:(j + 1) * tn] = jnp.dot(hx_ref[...], w_ref[:, n_qk + j * tn:n_qk + (j + 1) * tn],
                                                 preferred_element_type=F32)
    for j in range(vt_ref.shape[0] // tn):
        vt_ref[j * tn:(j + 1) * tn, :] = lax.dot_general(
            wvt_ref[j * tn:(j + 1) * tn, :], hx_ref[...], (((1,), (1,)), ((), ())),
            preferred_element_type=F32).astype(BF16)


def _in_odd_kernel(x_ref, g_ref, sc_ref, sh_ref, w_ref, wvt_ref, cq_ref, sq_ref, ck_ref, sk_ref,
                   qk_ref, vt_ref, hx_ref, *, q_width):
    _in_prologue(x_ref, g_ref, sc_ref, sh_ref, hx_ref)
    for hp in range(qk_ref.shape[1] // MXU_N):
        c_ref, s_ref = (cq_ref, sq_ref) if hp * MXU_N < q_width else (ck_ref, sk_ref)
        y2 = jnp.dot(hx_ref[...], w_ref[:, hp * MXU_N:(hp + 1) * MXU_N], preferred_element_type=F32)
        for h in range(MXU_N // HEAD_DIM):
            y = y2[:, h * HEAD_DIM:(h + 1) * HEAD_DIM]
            cols = slice(hp * MXU_N + h * HEAD_DIM, hp * MXU_N + (h + 1) * HEAD_DIM)
            r = lax.rsqrt(jnp.mean(y * y, axis=-1, keepdims=True) + RMS_EPS)
            sw = pltpu.roll(y, HEAD_DIM // 2, 1)
            qk_ref[:, cols] = ((y * c_ref[...] + sw * s_ref[...]) * r).astype(BF16)
    vt_ref[...] = lax.dot_general(wvt_ref[...], hx_ref[...], (((1,), (1,)), ((), ())),
                                  preferred_element_type=F32).astype(BF16)


def _common_in_specs(tm, d, seg):
    return [pl.BlockSpec((tm, d), lambda i, *_: (i, 0)),
            pl.BlockSpec((1, d), lambda i, *_: (0, 0)),
            pl.BlockSpec((None, None, 1, d), lambda i, *_: (seg(i), 1, 0, 0)),
            pl.BlockSpec((None, None, 1, d), lambda i, *_: (seg(i), 0, 0, 0))]


def _in_proj_even(xs, g, mods, w, w_vt, seg, tm):
    nt, d = xs.shape
    n = w.shape[1]
    n_qk = 2 * NA_WIDTH
    v_w = w_vt.shape[0]
    resident = dict(pipeline_mode=pl.Buffered(1))
    return pl.pallas_call(
        functools.partial(_in_even_kernel, tn=512, q_width=NA_WIDTH),
        grid=(nt // tm,),
        in_specs=_common_in_specs(tm, d, seg) + [
            pl.BlockSpec((d, n), lambda i: (0, 0), **resident),
            pl.BlockSpec((v_w, d), lambda i: (0, 0), **resident)],
        out_specs=[pl.BlockSpec((tm, n_qk), lambda i: (i, 0)),
                   pl.BlockSpec((v_w, tm), lambda i: (0, i)),
                   pl.BlockSpec((tm, n - n_qk), lambda i: (i, 0))],
        out_shape=[jax.ShapeDtypeStruct((nt, n_qk), BF16),
                   jax.ShapeDtypeStruct((v_w, nt), BF16),
                   jax.ShapeDtypeStruct((nt, n - n_qk), F32)],
        scratch_shapes=[pltpu.VMEM((tm, d), BF16)],
        compiler_params=pltpu.CompilerParams(dimension_semantics=("parallel",),
                                             vmem_limit_bytes=VMEM_LIMIT_BIG),
        name="in_proj_even",
    )(xs, g.reshape(1, d), mods, mods, w, w_vt)


def _in_proj_odd(xs, g, mods, w_qk, w_vt, ctab, stab, seg, tm, tiles_per_batch, n_lat_tiles, q_width):
    nt, d = xs.shape
    n_qk = w_qk.shape[1]
    kv_w = w_vt.shape[0]
    rope_row = lambda i: jnp.where(i < n_lat_tiles, i % tiles_per_batch, tiles_per_batch)
    tab_spec = lambda which: pl.BlockSpec((None, tm, HEAD_DIM), lambda i: (which, rope_row(i), 0))
    resident = dict(pipeline_mode=pl.Buffered(1))
    return pl.pallas_call(
        functools.partial(_in_odd_kernel, q_width=q_width),
        grid=(nt // tm,),
        in_specs=_common_in_specs(tm, d, seg) + [
            pl.BlockSpec((d, n_qk), lambda i: (0, 0), **resident),
            pl.BlockSpec((kv_w, d), lambda i: (0, 0), **resident),
            tab_spec(0), tab_spec(0), tab_spec(1), tab_spec(1)],
        out_specs=[pl.BlockSpec((tm, n_qk), lambda i: (i, 0)),
                   pl.BlockSpec((kv_w, tm), lambda i: (0, i))],
        out_shape=[jax.ShapeDtypeStruct((nt, n_qk), BF16),
                   jax.ShapeDtypeStruct((kv_w, nt), BF16)],
        scratch_shapes=[pltpu.VMEM((tm, d), BF16)],
        compiler_params=_params(("parallel",)),
        name="in_proj_odd",
    )(xs, g.reshape(1, d), mods, mods, w_qk, w_vt, ctab, stab, ctab, stab)


def _na_kernel(q_ref, k_ref, vt_ref, qc_ref, kc_ref, vtc_ref, bias_ref, o_ref, oc_ref, *, n_rows, blocks_per_step):
    c = pl.program_id(2)
    bq = NA_BLOCK_ROWS * GRID_W
    span = NA_SPAN_ROWS * GRID_W
    n_blocks = n_rows // NA_BLOCK_ROWS
    dn = (((1,), (1,)), ((), ()))
    kc = kc_ref[...]
    vtc = vtc_ref[...]

    def span_start(j):
        blk = c * blocks_per_step + j
        kr_base = jnp.clip(blk * NA_BLOCK_ROWS - NA_WIN_ROWS // 2, 0, n_rows - NA_SPAN_ROWS)
        return blk, pl.multiple_of(kr_base * GRID_W, NA_BLOCK_ROWS * GRID_W)

    def scores(j):
        blk, start = span_start(j)
        kind = jnp.where(blk == 0, 0, jnp.where(blk == n_blocks - 1, 2, 1))
        q = q_ref[j * bq:(j + 1) * bq, :]
        s_nb = lax.dot_general(k_ref[pl.ds(start, span), :], q, dn, preferred_element_type=F32) + bias_ref[kind]
        s_cx = lax.dot_general(kc, q, dn, preferred_element_type=F32)
        return s_nb, s_cx

    def finish(j, s_nb, s_cx):
        _, start = span_start(j)
        m = jnp.maximum(s_nb.max(axis=0, keepdims=True), s_cx.max(axis=0, keepdims=True))
        p_nb = jnp.exp2(s_nb - m)
        p_cx = jnp.exp2(s_cx - m)
        l = p_nb.sum(axis=0, keepdims=True) + p_cx.sum(axis=0, keepdims=True)
        o = (jnp.dot(vt_ref[:, pl.ds(start, span)], p_nb.astype(BF16), preferred_element_type=F32)
             + jnp.dot(vtc, p_cx.astype(BF16), preferred_element_type=F32))
        o_ref[j * bq:(j + 1) * bq, :] = (o / l).T.astype(BF16)

    cur = scores(0)
    for j in range(blocks_per_step):
        nxt = scores(j + 1) if j + 1 < blocks_per_step else None
        finish(j, *cur)
        cur = nxt

    @pl.when(c == pl.num_programs(2) - 1)
    def _():
        s = lax.dot_general(kc, qc_ref[...], dn, preferred_element_type=F32)
        p = jnp.exp2(s - s.max(axis=0, keepdims=True))
        l = p.sum(axis=0, keepdims=True)
        o = jnp.dot(vtc, p.astype(BF16), preferred_element_type=F32)
        oc_ref[...] = (o / l).T.astype(BF16)


def _na_bias_table(rpb, n_rows):
    n_heads = rpb.shape[0]
    qc = jnp.arange(GRID_W)[:, None]
    kc = jnp.arange(GRID_W)[None, :]
    kc0 = jnp.clip(qc - NA_WIN_COLS // 2, 0, GRID_W - NA_WIN_COLS)
    valid = (kc >= kc0) & (kc < kc0 + NA_WIN_COLS)
    off_c = kc - qc + (NA_WIN_COLS - 1)
    col = jnp.zeros((n_heads, 2 * NA_WIN_ROWS - 1, GRID_W, GRID_W), F32)
    for o in range(2 * NA_WIN_COLS - 1):
        col = col + jnp.where((off_c == o)[None, None], rpb[:, :, o, None, None].astype(F32), 0.0)
    col = jnp.where(valid[None, None], col * LOG2E, MASK_VALUE).transpose(0, 1, 3, 2)
    tables = []
    for r0, kr_base in ((0, 0), (NA_BLOCK_ROWS, 0), (n_rows - NA_BLOCK_ROWS, n_rows - NA_SPAN_ROWS)):
        per_row = []
        for t in range(NA_BLOCK_ROWS):
            q_row = r0 + t
            win0 = min(max(q_row - NA_WIN_ROWS // 2, 0), n_rows - NA_WIN_ROWS)
            u_lo = win0 - kr_base
            off_lo = win0 - q_row + (NA_WIN_ROWS - 1)
            assert 0 <= u_lo <= NA_SPAN_ROWS - NA_WIN_ROWS
            per_row.append(jnp.pad(col[:, off_lo:off_lo + NA_WIN_ROWS],
                                   ((0, 0), (u_lo, NA_SPAN_ROWS - NA_WIN_ROWS - u_lo), (0, 0), (0, 0)),
                                   constant_values=MASK_VALUE))
        tb = jnp.stack(per_row, axis=3)
        tables.append(tb.reshape(n_heads, NA_SPAN_ROWS * GRID_W, NA_BLOCK_ROWS * GRID_W))
    return jnp.stack(tables)


def _na_attention(qk, vt, bias_tbl, n_batch, seq, ctx_len):
    n_rows = seq // GRID_W
    bq = NA_BLOCK_ROWS * GRID_W
    span = NA_SPAN_ROWS * GRID_W
    step_rows = min(2048, seq)
    assert ctx_len == bq and n_rows % NA_BLOCK_ROWS == 0 and n_rows >= NA_SPAN_ROWS + NA_BLOCK_ROWS
    steps = seq // step_rows
    ctx_blk0 = n_batch * seq // ctx_len
    h_k = NA_HEADS
    return pl.pallas_call(
        functools.partial(_na_kernel, n_rows=n_rows, blocks_per_step=step_rows // bq),
        grid=(n_batch, NA_HEADS, steps),
        in_specs=[pl.BlockSpec((step_rows, HEAD_DIM), lambda b, h, c: (b * steps + c, h)),
                  pl.BlockSpec((seq, HEAD_DIM), lambda b, h, c: (b, h_k + h)),
                  pl.BlockSpec((HEAD_DIM, seq), lambda b, h, c: (h, b)),
                  pl.BlockSpec((ctx_len, HEAD_DIM), lambda b, h, c: (ctx_blk0 + b, h)),
                  pl.BlockSpec((ctx_len, HEAD_DIM), lambda b, h, c: (ctx_blk0 + b, h_k + h)),
                  pl.BlockSpec((HEAD_DIM, ctx_len), lambda b, h, c: (h, ctx_blk0 + b)),
                  pl.BlockSpec((3, None, span, bq), lambda b, h, c: (0, h, 0, 0))],
        out_specs=[pl.BlockSpec((step_rows, HEAD_DIM), lambda b, h, c: (b * steps + c, h)),
                   pl.BlockSpec((ctx_len, HEAD_DIM), lambda b, h, c: (b, h))],
        out_shape=[jax.ShapeDtypeStruct((n_batch * seq, NA_WIDTH), BF16),
                   jax.ShapeDtypeStruct((n_batch * ctx_len, NA_WIDTH), BF16)],
        compiler_params=_params(("parallel", "parallel", "arbitrary")),
        name="neighbourhood_attention",
    )(qk, qk, vt, qk, qk, vt, bias_tbl)


def _sg_kernel(uz_ref, ws_ref, bs_ref, lg_ref, lb_ref, o_ref, *, n_chunks):
    for g in range(SG_GROUPS):
        w = ws_ref[g].astype(BF16)
        for n in range(n_chunks):
            rows = slice(n * SG_CHUNK, (n + 1) * SG_CHUNK)
            u = uz_ref[rows, g * SG_DIM:(g + 1) * SG_DIM]
            z = uz_ref[rows, SG_WIDTH + g * SG_DIM:SG_WIDTH + (g + 1) * SG_DIM]
            zz = _gelu_tanh(z)
            mu = jnp.mean(zz, axis=-1, keepdims=True)
            xc = zz - mu
            var = jnp.mean(xc * xc, axis=-1, keepdims=True)
            zn = xc * lax.rsqrt(var + LN_EPS) * lg_ref[g] + lb_ref[g]
            mixed = jnp.dot(w, zn.astype(BF16), preferred_element_type=F32) + bs_ref[g]
            o_ref[rows, g * SG_DIM:(g + 1) * SG_DIM] = (_gelu_tanh(u) * mixed).astype(BF16)


def _spatial_gating(uz, w_s, b_s, ln_g, ln_b, tm):
    nt = uz.shape[0]
    full3 = lambda i: (0, 0, 0)
    return pl.pallas_call(
        functools.partial(_sg_kernel, n_chunks=tm // SG_CHUNK),
        grid=(nt // tm,),
        in_specs=[pl.BlockSpec((tm, 2 * SG_WIDTH), lambda i: (i, 0)),
                  pl.BlockSpec((SG_GROUPS, SG_CHUNK, SG_CHUNK), full3),
                  pl.BlockSpec((SG_GROUPS, SG_CHUNK, 1), full3),
                  pl.BlockSpec((SG_GROUPS, 1, SG_DIM), full3),
                  pl.BlockSpec((SG_GROUPS, 1, SG_DIM), full3)],
        out_specs=pl.BlockSpec((tm, SG_WIDTH), lambda i: (i, 0)),
        out_shape=jax.ShapeDtypeStruct((nt, SG_WIDTH), BF16),
        compiler_params=_params(("parallel",)),
        name="spatial_gating",
    )(uz, w_s, b_s.reshape(SG_GROUPS, SG_CHUNK, 1), ln_g.reshape(SG_GROUPS, 1, SG_DIM),
      ln_b.reshape(SG_GROUPS, 1, SG_DIM))


def _flash_kernel(q_ref, k_ref, vt_ref, kc_ref, vtc_ref, o_ref, q_s, s_a, s_b, s_c, acc_s, m_s,
                  *, tq, tk, seq, group, n_lat_q):
    qi = pl.program_id(2)
    for g in range(group):
        q_s[g * tq:(g + 1) * tq, :] = q_ref[:, g * HEAD_DIM:(g + 1) * HEAD_DIM]
    m_rows = group * tq
    n_blk = m_rows // MXU_N
    n_tiles = seq // tk
    dn = (((1,), (1,)), ((), ()))
    blk_cols = lambda n: slice(n * MXU_N, (n + 1) * MXU_N)

    def scores(k, s_ref, n):
        s_ref[:, blk_cols(n)] = lax.dot_general(k, q_s[blk_cols(n), :], dn, preferred_element_type=F32)

    def with_ones_row(vt):
        ones = jnp.where(lax.broadcasted_iota(jnp.int32, (BF16_ROWS, vt.shape[1]), 0) == 0, 1.0, 0.0)
        return jnp.concatenate([vt, ones.astype(BF16)], axis=0)

    def consume(s_ref, vt_aug, n):
        cols = blk_cols(n)
        s = s_ref[:, cols]
        m_old = m_s[:, cols]
        m_new = jnp.maximum(m_old, s.max(axis=0, keepdims=True))
        alpha = jnp.exp2(m_old - m_new)
        p = jnp.exp2(s - m_new).astype(BF16)
        acc_s[:, cols] = alpha * acc_s[:, cols] + jnp.dot(vt_aug, p, preferred_element_type=F32)
        m_s[:, cols] = m_new

    def scores_and_consume(k_next, s_next, s_cur, vt_cur):
        vt_aug = with_ones_row(vt_cur)
        for n in range(n_blk):
            scores(k_next, s_next, n)
            consume(s_cur, vt_aug, n)

    k_tile = lambda t: k_ref[pl.ds(pl.multiple_of(t * tk, tk), tk), :]
    vt_tile = lambda t: vt_ref[:, pl.ds(pl.multiple_of(t * tk, tk), tk)]

    m_s[...] = jnp.full(m_s.shape, MASK_VALUE, F32)
    acc_s[...] = jnp.zeros(acc_s.shape, F32)

    @pl.when(qi < n_lat_q)
    def _():
        for n in range(n_blk):
            scores(k_tile(0), s_a, n)

        def pair(u, carry):
            t0 = 2 * u
            scores_and_consume(k_tile(t0 + 1), s_b, s_a, vt_tile(t0))
            scores_and_consume(k_tile(t0 + 2), s_a, s_b, vt_tile(t0 + 1))
            return carry

        lax.fori_loop(0, n_tiles // 2 - 1, pair, 0)
        scores_and_consume(k_tile(n_tiles - 1), s_b, s_a, vt_tile(n_tiles - 2))
        scores_and_consume(kc_ref[...], s_c, s_b, vt_tile(n_tiles - 1))
        vtc_aug = with_ones_row(vtc_ref[...])
        for n in range(n_blk):
            consume(s_c, vtc_aug, n)

    @pl.when(qi == n_lat_q)
    def _():
        vtc_aug = with_ones_row(vtc_ref[...])
        for n in range(n_blk):
            scores(kc_ref[...], s_c, n)
            consume(s_c, vtc_aug, n)

    o = acc_s[:HEAD_DIM, :] / acc_s[HEAD_DIM:HEAD_DIM + 1, :]
    for g in range(group):
        o_ref[:, g * HEAD_DIM:(g + 1) * HEAD_DIM] = o[:, g * tq:(g + 1) * tq].T.astype(BF16)


def _flash_attention(qk, vt, n_batch, seq, ctx_len, n_q_heads):
    nt = qk.shape[0]
    group = n_q_heads // GQA_KV_HEADS
    tq = ctx_len
    tk = min(FLASH_KEY_TILE, seq // 2)
    assert seq % (2 * tk) == 0 and (group * tq) % MXU_N == 0
    n_lat_q = seq // tq
    ctx_blk0 = n_batch * seq // tq
    hk0 = n_q_heads
    gw = group * HEAD_DIM
    m_rows = group * tq

    def q_idx(b, kh, qi):
        return (jnp.where(qi < n_lat_q, b * n_lat_q + qi, ctx_blk0 + b), kh)

    return pl.pallas_call(
        functools.partial(_flash_kernel, tq=tq, tk=tk, seq=seq, group=group, n_lat_q=n_lat_q),
        grid=(n_batch, GQA_KV_HEADS, n_lat_q + 1),
        in_specs=[pl.BlockSpec((tq, gw), q_idx),
                  pl.BlockSpec((seq, HEAD_DIM), lambda b, kh, qi: (b, hk0 + kh)),
                  pl.BlockSpec((HEAD_DIM, seq), lambda b, kh, qi: (kh, b)),
                  pl.BlockSpec((tq, HEAD_DIM), lambda b, kh, qi: (ctx_blk0 + b, hk0 + kh)),
                  pl.BlockSpec((HEAD_DIM, tq), lambda b, kh, qi: (kh, ctx_blk0 + b))],
        out_specs=pl.BlockSpec((tq, gw), q_idx),
        out_shape=jax.ShapeDtypeStruct((nt, n_q_heads * HEAD_DIM), BF16),
        scratch_shapes=[pltpu.VMEM((m_rows, HEAD_DIM), BF16),
                        pltpu.VMEM((tk, m_rows), F32), pltpu.VMEM((tk, m_rows), F32),
                        pltpu.VMEM((tq, m_rows), F32),
                        pltpu.VMEM((HEAD_DIM + BF16_ROWS, m_rows), F32),
                        pltpu.VMEM((1, m_rows), F32)],
        compiler_params=_params(("parallel", "parallel", "arbitrary")),
        name="gqa_flash_attention",
    )(qk, qk, vt, qk, vt)


def _out_kernel(*refs, n_lhs):
    lhs_refs = refs[:n_lhs]
    (w_ref, x_ref, ga_ref, g2_ref, sc_ref, sh_ref, wr_ref, br_ref, xo_ref, tok_ref, lg_ref) = refs[n_lhs:]
    acc = None
    k0 = 0
    for a_ref in lhs_refs:
        kp = a_ref.shape[1]
        part = jnp.dot(a_ref[...], w_ref[k0:k0 + kp, :], preferred_element_type=F32)
        acc = part if acc is None else acc + part
        k0 += kp
    xn = x_ref[...] + ga_ref[...] * acc
    xo_ref[...] = xn
    tok = _rms_modulate(xn, g2_ref[...], sc_ref[...], sh_ref[...])
    tok_ref[...] = _pack_bf16_pairs(tok)
    lg_ref[...] = lax.dot_general(wr_ref[...], tok.astype(BF16), (((1,), (1,)), ((), ())),
                                  preferred_element_type=F32) + br_ref[...]


def _out_proj(lhs, w, xs, mods, g2, w_router, b_router, seg, tm):
    nt, d = xs.shape
    k = w.shape[0]
    modspec = lambda which: pl.BlockSpec((None, None, 1, d), lambda i: (seg(i), which, 0, 0))
    return pl.pallas_call(
        functools.partial(_out_kernel, n_lhs=len(lhs)),
        grid=(nt // tm,),
        in_specs=[pl.BlockSpec((tm, a.shape[1]), lambda i: (i, 0)) for a in lhs] + [
            pl.BlockSpec((k, d), lambda i: (0, 0)),
            pl.BlockSpec((tm, d), lambda i: (i, 0)),
            modspec(2),
            pl.BlockSpec((1, d), lambda i: (0, 0)),
            modspec(4),
            modspec(3),
            pl.BlockSpec((ROUTER_W, d), lambda i: (0, 0)),
            pl.BlockSpec((ROUTER_W, 1), lambda i: (0, 0))],
        out_specs=[pl.BlockSpec((tm, d), lambda i: (i, 0)),
                   pl.BlockSpec((tm, d // 2), lambda i: (i, 0)),
                   pl.BlockSpec((ROUTER_W, tm), lambda i: (0, i))],
        out_shape=[jax.ShapeDtypeStruct((nt, d), F32),
                   jax.ShapeDtypeStruct((nt, d // 2), jnp.uint32),
                   jax.ShapeDtypeStruct((ROUTER_W, nt), F32)],
        compiler_params=_params(("parallel",)),
        name="out_proj_residual_router",
    )(*lhs, w, xs, mods, g2.reshape(1, d), mods, mods, w_router, b_router)


def _expert_kernel(blk_e_ref, n_used_ref, xs_ref, w1_ref, w3_ref, w2_ref, o_ref):
    b = pl.program_id(0)

    @pl.when(b < n_used_ref[0])
    def _():
        x = _unpack_bf16_pairs(xs_ref[...]).astype(BF16)
        h1 = jnp.dot(x, w1_ref[...].astype(BF16), preferred_element_type=F32)
        h3 = jnp.dot(x, w3_ref[...].astype(BF16), preferred_element_type=F32)
        a = (h1 * _sigmoid(h1) * h3).astype(BF16)
        o_ref[...] = _pack_bf16_pairs(jnp.dot(a, w2_ref[...].astype(BF16), preferred_element_type=F32))

    @pl.when(b >= n_used_ref[0])
    def _():
        o_ref[...] = jnp.zeros_like(o_ref)


def _experts(xs, blk_e, n_used, w1, w3, w2, layer):
    p, dw = xs.shape
    d, hid = w1.shape[-2:]
    nb = p // MOE_ROWS
    grid_spec = pltpu.PrefetchScalarGridSpec(
        num_scalar_prefetch=2,
        grid=(nb,),
        in_specs=[pl.BlockSpec((MOE_ROWS, dw), lambda b, be, nu: (jnp.minimum(b, nu[0] - 1), 0)),
                  pl.BlockSpec((None, None, d, hid), lambda b, be, nu: (layer, be[b], 0, 0)),
                  pl.BlockSpec((None, None, d, hid), lambda b, be, nu: (layer, be[b], 0, 0)),
                  pl.BlockSpec((None, None, hid, d), lambda b, be, nu: (layer, be[b], 0, 0))],
        out_specs=pl.BlockSpec((MOE_ROWS, dw), lambda b, be, nu: (b, 0)),
    )
    return pl.pallas_call(
        _expert_kernel,
        grid_spec=grid_spec,
        out_shape=jax.ShapeDtypeStruct((p, dw), jnp.uint32),
        compiler_params=_params(("arbitrary",)),
        name="moe_experts",
    )(blk_e, n_used, xs, w1, w3, w2)


def _first_index(vals, target):
    idx = jnp.full(target.shape, len(vals) - 1, jnp.int32)
    for i in range(len(vals) - 2, -1, -1):
        idx = jnp.where(vals[i] == target, i, idx)
    return idx


def _router_kernel(lg_ref, gate_ref, pos_ref, blk_ref, nused_ref, e_s, rank_s, *, n_tiles):
    tw = ROUTE_TILE
    tri = jnp.where(lax.broadcasted_iota(jnp.int32, (tw, tw), 0) < lax.broadcasted_iota(jnp.int32, (tw, tw), 1),
                    1.0, 0.0).astype(BF16)
    eid = lax.broadcasted_iota(jnp.int32, (MOE_EXPERTS, tw), 0)
    epg = MOE_EXPERTS_PER_GROUP

    def pass1(c, run):
        sl = pl.ds(pl.multiple_of(c * tw, tw), tw)
        g = [lg_ref[i:i + 1, sl] for i in range(MOE_GROUPS)]
        gm = functools.reduce(jnp.maximum, g)
        gidx = _first_index(g, gm)
        gval = 1.0 / functools.reduce(lambda a, b: a + b, [jnp.exp(gi - gm) for gi in g])
        le = []
        for e in range(epg):
            sel = lg_ref[MOE_GROUPS + (MOE_GROUPS - 1) * epg + e:MOE_GROUPS + (MOE_GROUPS - 1) * epg + e + 1, sl]
            for gg in range(MOE_GROUPS - 2, -1, -1):
                sel = jnp.where(gidx == gg, lg_ref[MOE_GROUPS + gg * epg + e:MOE_GROUPS + gg * epg + e + 1, sl], sel)
            le.append(sel)
        m1 = functools.reduce(jnp.maximum, le)
        i1 = _first_index(le, m1)
        le2 = [jnp.where(i1 == e, -jnp.inf, le[e]) for e in range(epg)]
        m2 = functools.reduce(jnp.maximum, le2)
        i2 = _first_index(le2, m2)
        t = jnp.exp(m2 - m1)
        inv = 1.0 / (1.0 + t)
        gate_ref[0:1, sl] = inv * gval
        gate_ref[1:2, sl] = t * inv * gval
        e0 = gidx * epg + i1
        e1 = gidx * epg + i2
        oh0 = eid == e0
        oh1 = eid == e1
        oh0f = jnp.where(oh0, 1.0, 0.0)
        oh1f = jnp.where(oh1, 1.0, 0.0)
        pre0 = jnp.dot(oh0f.astype(BF16), tri, preferred_element_type=F32)
        pre1 = jnp.dot(oh1f.astype(BF16), tri, preferred_element_type=F32)
        c0 = oh0f.sum(axis=1, keepdims=True)
        c1 = oh1f.sum(axis=1, keepdims=True)
        rank_s[0:1, sl] = jnp.where(oh0, run + pre0, 0.0).sum(axis=0, keepdims=True)
        rank_s[1:2, sl] = jnp.where(oh1, run + c0 + pre1, 0.0).sum(axis=0, keepdims=True)
        e_s[0:1, sl] = e0
        e_s[1:2, sl] = e1
        return run + c0 + c1

    counts = lax.fori_loop(0, n_tiles, pass1, jnp.zeros((MOE_EXPERTS, 1), F32))
    blocks = jnp.floor((counts + (MOE_ROWS - 1)) * (1.0 / MOE_ROWS))
    lincl = jnp.where(lax.broadcasted_iota(jnp.int32, (MOE_EXPERTS, MOE_EXPERTS), 1)
                      <= lax.broadcasted_iota(jnp.int32, (MOE_EXPERTS, MOE_EXPERTS), 0), 1.0, 0.0).astype(BF16)
    end_blocks = jnp.dot(lincl, jnp.broadcast_to(blocks, (MOE_EXPERTS, LANES)).astype(BF16),
                         preferred_element_type=F32)[:, 0:1]
    start_rows = (end_blocks - blocks) * MOE_ROWS

    def pass2(c, carry):
        sl = pl.ds(pl.multiple_of(c * tw, tw), tw)
        for k in range(MOE_TOP_K):
            base = jnp.where(eid == e_s[k:k + 1, sl], start_rows, 0.0).sum(axis=0, keepdims=True)
            pos_ref[k:k + 1, sl] = (base + rank_s[k:k + 1, sl]).astype(jnp.int32)
        return carry

    lax.fori_loop(0, n_tiles, pass2, 0)
    bl = lax.broadcasted_iota(jnp.int32, (MOE_EXPERTS, PLAN_LANES), 1).astype(F32)
    blk = jnp.where(end_blocks <= bl, 1.0, 0.0).sum(axis=0, keepdims=True)
    blk_ref[...] = jnp.minimum(blk, MOE_EXPERTS - 1.0).astype(jnp.int32)
    nused_ref[...] = jnp.broadcast_to(end_blocks[MOE_EXPERTS - 1:MOE_EXPERTS, :], (1, LANES)).astype(jnp.int32)


def _router(lg_t):
    n = lg_t.shape[1]
    assert n % ROUTE_TILE == 0
    return pl.pallas_call(
        functools.partial(_router_kernel, n_tiles=n // ROUTE_TILE),
        out_shape=[jax.ShapeDtypeStruct((MOE_TOP_K, n), F32),
                   jax.ShapeDtypeStruct((MOE_TOP_K, n), jnp.int32),
                   jax.ShapeDtypeStruct((1, PLAN_LANES), jnp.int32),
                   jax.ShapeDtypeStruct((1, LANES), jnp.int32)],
        scratch_shapes=[pltpu.VMEM((MOE_TOP_K, n), jnp.int32), pltpu.VMEM((MOE_TOP_K, n), F32)],
        compiler_params=pltpu.CompilerParams(vmem_limit_bytes=VMEM_LIMIT),
        name="moe_router_plan",
    )(lg_t)


def _dispatch_kernel(p0_ref, p1_ref, tok_ref, init_ref, xs_ref, sem, *, tm):
    del init_ref
    base = pl.program_id(0) * tm

    def copy(r, p_ref):
        return pltpu.make_async_copy(tok_ref.at[pl.ds(r, 1)], xs_ref.at[pl.ds(p_ref[base + r], 1)], sem)

    def start(r, carry):
        copy(r, p0_ref).start()
        copy(r, p1_ref).start()
        return carry

    def wait(r, carry):
        copy(r, p0_ref).wait()
        copy(r, p1_ref).wait()
        return carry

    lax.fori_loop(0, tm, start, 0, unroll=8)
    lax.fori_loop(0, tm, wait, 0, unroll=8)


def _dispatch(tok, pos0, pos1, init, tm):
    nt, d = tok.shape
    n_slots = init.shape[0]
    grid_spec = pltpu.PrefetchScalarGridSpec(
        num_scalar_prefetch=2,
        grid=(nt // tm,),
        in_specs=[pl.BlockSpec((tm, d), lambda i, p0, p1: (i, 0)),
                  pl.BlockSpec(memory_space=pl.ANY)],
        out_specs=pl.BlockSpec(memory_space=pl.ANY),
        scratch_shapes=[pltpu.SemaphoreType.DMA(())],
    )
    return pl.pallas_call(
        functools.partial(_dispatch_kernel, tm=tm),
        grid_spec=grid_spec,
        out_shape=jax.ShapeDtypeStruct((n_slots, d), tok.dtype),
        input_output_aliases={3: 0},
        compiler_params=_params(("arbitrary",)),
        name="moe_dispatch",
    )(pos0, pos1, tok, init)


def _combine_kernel(p0_ref, p1_ref, x_ref, yb_ref, gt_ref, ga_ref, *rest, tm, final):
    if final:
        gf_ref, o_ref, y0_s, y1_s, sem = rest
    else:
        o_ref, y0_s, y1_s, sem = rest
    i = pl.program_id(0)
    slot = i % 2

    def copy(tile, buf, r, p_ref, y_s):
        return pltpu.make_async_copy(yb_ref.at[pl.ds(p_ref[tile * tm + r], 1)], y_s.at[buf, pl.ds(r, 1)],
                                     sem.at[buf])

    def start_tile(tile, buf):
        def body(r, carry):
            copy(tile, buf, r, p0_ref, y0_s).start()
            copy(tile, buf, r, p1_ref, y1_s).start()
            return carry
        lax.fori_loop(0, tm, body, 0, unroll=8)

    def wait_tile(tile, buf):
        def body(r, carry):
            copy(tile, buf, r, p0_ref, y0_s).wait()
            copy(tile, buf, r, p1_ref, y1_s).wait()
            return carry
        lax.fori_loop(0, tm, body, 0, unroll=8)

    @pl.when(i == 0)
    def _():
        start_tile(i, slot)

    @pl.when(i + 1 < pl.num_programs(0))
    def _():
        start_tile(i + 1, 1 - slot)

    wait_tile(i, slot)
    gt = gt_ref[...]
    f = gt[:, 0:1] * _unpack_bf16_pairs(y0_s[slot]) + gt[:, 1:2] * _unpack_bf16_pairs(y1_s[slot])
    xn = x_ref[...] + ga_ref[...] * f
    if final:
        r = lax.rsqrt(jnp.mean(xn * xn, axis=-1, keepdims=True) + RMS_EPS)
        xn = xn * r * gf_ref[...]
    o_ref[...] = xn


def _combine(xs, yb, pos0, pos1, gate, mods, seg, tm, final_g=None, n_out_rows=None):
    nt, d = xs.shape
    rows = nt if n_out_rows is None else n_out_rows
    row_blk = lambda i, p0, p1: (i, 0)
    in_specs = [pl.BlockSpec((tm, d), row_blk),
                pl.BlockSpec(memory_space=pl.ANY),
                pl.BlockSpec((tm, MOE_TOP_K), row_blk),
                pl.BlockSpec((None, None, 1, d), lambda i, p0, p1: (seg(i), 5, 0, 0))]
    args = [xs, yb, gate, mods]
    if final_g is not None:
        in_specs.append(pl.BlockSpec((1, d), lambda i, p0, p1: (0, 0)))
        args.append(final_g.reshape(1, d))
    grid_spec = pltpu.PrefetchScalarGridSpec(
        num_scalar_prefetch=2,
        grid=(rows // tm,),
        in_specs=in_specs,
        out_specs=pl.BlockSpec((tm, d), row_blk),
        scratch_shapes=[pltpu.VMEM((2, tm, d // 2), jnp.uint32), pltpu.VMEM((2, tm, d // 2), jnp.uint32),
                        pltpu.SemaphoreType.DMA((2,))],
    )
    return pl.pallas_call(
        functools.partial(_combine_kernel, tm=tm, final=final_g is not None),
        grid_spec=grid_spec,
        out_shape=jax.ShapeDtypeStruct((rows, d), F32),
        compiler_params=_params(("arbitrary",)),
        name="moe_combine_residual",
    )(pos0, pos1, *args)


def _rope_tables(seq, pad_rows):
    t = jnp.arange(seq, dtype=jnp.int32)
    row = (t // GRID_W).astype(F32)
    col = (t % GRID_W).astype(F32)
    inv_freq = ROPE_THETA ** (-jnp.arange(ROPE_AXIS_DIM // 2, dtype=F32) * 2.0 / ROPE_AXIS_DIM)
    ang = jnp.concatenate([row[:, None] * inv_freq, col[:, None] * inv_freq], axis=-1)
    cos, sin = jnp.cos(ang), jnp.sin(ang)
    cos = jnp.concatenate([cos, jnp.ones((pad_rows, HEAD_DIM // 2), F32)], axis=0)
    sin = jnp.concatenate([sin, jnp.zeros((pad_rows, HEAD_DIM // 2), F32)], axis=0)
    return cos, sin


def _split_pairs(w, n_heads):
    d = w.shape[0]
    return w.reshape(d, n_heads, HEAD_DIM // 2, 2).transpose(0, 1, 3, 2).reshape(d, n_heads * HEAD_DIM)


def _gain_rope_tables(cos, sin, q_gain, k_gain):
    def tables(g):
        ge, go = g[0::2][None, :], g[1::2][None, :]
        return (jnp.concatenate([ge * cos, go * cos], axis=1),
                jnp.concatenate([-go * sin, ge * sin], axis=1))
    cq, sq = tables(q_gain)
    ck, sk = tables(k_gain)
    return jnp.stack([cq, ck]), jnp.stack([sq, sk])


def kernel(x, c, ctx, c_ctx, mod_w, mod_b, norm_mix_g, norm_ffn_g, norm_final_g, na_sg_w_in, na_sg_w_out, na_rpb, sg_w_s, sg_b_s, sg_ln_g, sg_ln_b, gqa_w_in, gqa_w_out, gqa_q_gain, gqa_k_gain, moe_w_group, moe_b_group, moe_w_expert, moe_b_expert, moe_w1, moe_w3, moe_w2):
    n_batch, seq, d = x.shape
    ctx_len = ctx.shape[1]
    depth = mod_w.shape[0]
    n_lat = n_batch * seq
    tm = min(ROW_TILE, ctx_len * n_batch)
    assert seq % tm == 0 and (n_batch * ctx_len) % tm == 0 and seq % ctx_len == 0
    assert ctx_len % GRID_W == 0 and n_batch + 1 <= 8
    tiles_per_batch = seq // tm
    seg = _seg_fn(tiles_per_batch, n_batch)
    n_q_heads = d // HEAD_DIM
    n_tok = n_lat + n_batch * ctx_len
    n_blocks = -(-n_tok * MOE_TOP_K // MOE_ROWS) + MOE_EXPERTS
    assert n_blocks <= PLAN_LANES and n_tok % ROUTE_TILE == 0

    xs = jnp.concatenate([x.reshape(n_lat, d), ctx.reshape(n_batch * ctx_len, d)], axis=0)
    cond = jnp.zeros((8, d), F32).at[:n_batch].set(c).at[n_batch].set(c_ctx)
    mod_all = _modulation(cond, mod_w, mod_b)
    cos, sin = _rope_tables(seq, tm)

    q_w = n_q_heads * HEAD_DIM
    kv_w = GQA_KV_HEADS * HEAD_DIM
    out = None
    slots = jnp.zeros((n_blocks * MOE_ROWS, d // 2), jnp.uint32)
    for layer in range(depth):
        last = layer == depth - 1
        mods = mod_all[layer, :n_batch + 1].reshape(n_batch + 1, 6, 1, d)
        i = layer // 2
        if layer % 2 == 0:
            w_in = na_sg_w_in[i]
            w_main = jnp.concatenate([w_in[:, :2 * NA_WIDTH], w_in[:, 3 * NA_WIDTH:]], axis=1).astype(BF16)
            w_vt = w_in[:, 2 * NA_WIDTH:3 * NA_WIDTH].T.astype(BF16)
            qk, vt, uz = _in_proj_even(xs, norm_mix_g[layer], mods, w_main, w_vt, seg, tm)
            att_lat, att_ctx = _na_attention(qk, vt, _na_bias_table(na_rpb[i], seq // GRID_W), n_batch, seq, ctx_len)
            att = jnp.concatenate([att_lat, att_ctx], axis=0)
            gat = _spatial_gating(uz, sg_w_s[i], sg_b_s[i], sg_ln_g[i], sg_ln_b[i], tm)
            lhs, w_out = [att, gat], na_sg_w_out[i].astype(BF16)
        else:
            ctab, stab = _gain_rope_tables(cos, sin, gqa_q_gain[i] * (ATTN_SCALE * LOG2E), gqa_k_gain[i])
            w_qk = _split_pairs(gqa_w_in[i][:, :q_w + kv_w], n_q_heads + GQA_KV_HEADS).astype(BF16)
            w_vt = gqa_w_in[i][:, q_w + kv_w:].T.astype(BF16)
            qk, vt = _in_proj_odd(xs, norm_mix_g[layer], mods, w_qk, w_vt, ctab, stab,
                                  seg, tm, tiles_per_batch, n_lat // tm, q_w)
            att = _flash_attention(qk, vt, n_batch, seq, ctx_len, n_q_heads)
            lhs, w_out = [att], gqa_w_out[i].astype(BF16)

        pad = ROUTER_W - MOE_GROUPS - MOE_EXPERTS
        w_router = jnp.concatenate([moe_w_group[layer], moe_w_expert[layer], jnp.zeros((d, pad), F32)],
                                   axis=1).T.astype(BF16)
        b_router = jnp.concatenate([moe_b_group[layer], moe_b_expert[layer], jnp.zeros((pad,), F32)])[:, None]
        xs, tok, lg_t = _out_proj(lhs, w_out, xs, mods, norm_ffn_g[layer], w_router, b_router, seg, tm)

        gate, pos, blk, nused = _router(lg_t)
        pos0, pos1 = pos[0], pos[1]
        slots = _dispatch(tok, pos0, pos1, slots, tm)
        yb = _experts(slots, blk[0, :n_blocks], nused[0, :1], moe_w1, moe_w3, moe_w2, layer)
        if last:
            out = _combine(xs, yb, pos0, pos1, gate.T, mods, seg, tm, final_g=norm_final_g, n_out_rows=n_lat)
        else:
            xs = _combine(xs, yb, pos0, pos1, gate.T, mods, seg, tm)
    return out.reshape(n_batch, seq, d)
```

```python
import functools

import jax
import jax.numpy as jnp
from jax import lax
from jax.experimental import pallas as pl
from jax.experimental.pallas import tpu as pltpu

F32 = jnp.float32
BF16 = jnp.bfloat16

GRID_W = 64
HEAD_DIM = 128
NA_HEADS = 8
NA_WIDTH = NA_HEADS * HEAD_DIM
NA_WIN_ROWS = 8
NA_WIN_COLS = 16
NA_BLOCK_ROWS = 4
NA_SPAN_ROWS = 12
SG_GROUPS = 8
SG_DIM = 128
SG_WIDTH = SG_GROUPS * SG_DIM
SG_CHUNK = 128
GQA_KV_HEADS = 4
ROPE_THETA = 10000.0
ROPE_AXIS_DIM = HEAD_DIM // 2
MOE_GROUPS = 4
MOE_EXPERTS_PER_GROUP = 8
MOE_EXPERTS = MOE_GROUPS * MOE_EXPERTS_PER_GROUP
MOE_TOP_K = 2
RMS_EPS = 1e-6
LN_EPS = 1e-5

LOG2E = 1.4426950408889634
MASK_VALUE = -1e30
ATTN_SCALE = HEAD_DIM ** -0.5
LANES = 128
MXU_N = 256
BF16_ROWS = 16
ROUTER_W = LANES
ROW_TILE = 512
MOE_ROWS = 512
FLASH_KEY_TILE = 1024
ROUTE_TILE = 512
PLAN_LANES = 256
VMEM_LIMIT = 48 * 1024 * 1024
VMEM_LIMIT_BIG = 56 * 1024 * 1024


def _params(sem):
    return pltpu.CompilerParams(dimension_semantics=sem, vmem_limit_bytes=VMEM_LIMIT)


def _sigmoid(x):
    return 1.0 / (1.0 + jnp.exp(-x))


def _gelu_tanh(x):
    cdf = 0.5 * (1.0 + jnp.tanh(0.7978845608028654 * (x + 0.044715 * (x * x * x))))
    return x * cdf


def _pack_bf16_pairs(x):
    c = x.shape[1] // 2
    bits = pltpu.bitcast(x.astype(jnp.bfloat16).astype(F32), jnp.uint32)
    return (bits[:, :c] >> 16) | bits[:, c:]


def _unpack_bf16_pairs(w):
    lo = pltpu.bitcast(w << 16, F32)
    hi = pltpu.bitcast(w & jnp.uint32(0xFFFF0000), F32)
    return jnp.concatenate([lo, hi], axis=1)


def _rms_modulate(x, g, sc, sh):
    r = lax.rsqrt(jnp.mean(x * x, axis=-1, keepdims=True) + RMS_EPS)
    return (x * r * g) * (1.0 + sc) + sh


def _mod_kernel(c_ref, w_ref, b_ref, o_ref):
    c = c_ref[...]
    cs = (c * _sigmoid(c)).astype(BF16)
    o_ref[0] = jnp.dot(cs, w_ref[0].astype(BF16), preferred_element_type=F32) + b_ref[0]


def _modulation(cond, mod_w, mod_b):
    depth, d, n = mod_w.shape
    tn = 1024
    return pl.pallas_call(
        _mod_kernel,
        grid=(depth, n // tn),
        in_specs=[pl.BlockSpec((8, d), lambda l, j: (0, 0)),
                  pl.BlockSpec((1, d, tn), lambda l, j: (l, 0, j)),
                  pl.BlockSpec((1, 1, tn), lambda l, j: (l, 0, j))],
        out_specs=pl.BlockSpec((1, 8, tn), lambda l, j: (l, 0, j)),
        out_shape=jax.ShapeDtypeStruct((depth, 8, n), F32),
        compiler_params=_params(("parallel", "parallel")),
        name="adaln_modulation",
    )(cond, mod_w, mod_b.reshape(depth, 1, n))


def _seg_fn(tiles_per_batch, n_batch):
    return lambda i: jnp.minimum(i // tiles_per_batch, n_batch)


def _in_prologue(x_ref, g_ref, sc_ref, sh_ref, hx_ref):
    hx_ref[...] = _rms_modulate(x_ref[...], g_ref[...], sc_ref[...], sh_ref[...]).astype(BF16)


def _in_even_kernel(x_ref, g_ref, sc_ref, sh_ref, w_ref, wvt_ref, qk_ref, vt_ref, uz_ref, hx_ref, *, tn, q_width):
    _in_prologue(x_ref, g_ref, sc_ref, sh_ref, hx_ref)
    n_qk = qk_ref.shape[1]
    for j in range(n_qk // tn):
        cols = slice(j * tn, (j + 1) * tn)
        acc = jnp.dot(hx_ref[...], w_ref[:, cols], preferred_element_type=F32)
        if j * tn < q_width:
            acc = acc * (ATTN_SCALE * LOG2E)
        qk_ref[:, cols] = acc.astype(BF16)
    for j in range(uz_ref.shape[1] // tn):
        uz_ref[:, j * tn:(j + 1) * tn] = jnp.dot(hx_ref[...], w_ref[:, n_qk + j * tn:n_qk + (j + 1) * tn],
                                                 preferred_element_type=F32)
    for j in range(vt_ref.shape[0] // tn):
        vt_ref[j * tn:(j + 1) * tn, :] = lax.dot_general(
            wvt_ref[j * tn:(j + 1) * tn, :], hx_ref[...], (((1,), (1,)), ((), ())),
            preferred_element_type=F32).astype(BF16)


def _in_odd_kernel(x_ref, g_ref, sc_ref, sh_ref, w_ref, wvt_ref, cq_ref, sq_ref, ck_ref, sk_ref,
                   qk_ref, vt_ref, hx_ref, *, q_width):
    _in_prologue(x_ref, g_ref, sc_ref, sh_ref, hx_ref)
    for hp in range(qk_ref.shape[1] // MXU_N):
        c_ref, s_ref = (cq_ref, sq_ref) if hp * MXU_N < q_width else (ck_ref, sk_ref)
        y2 = jnp.dot(hx_ref[...], w_ref[:, hp * MXU_N:(hp + 1) * MXU_N], preferred_element_type=F32)
        for h in range(MXU_N // HEAD_DIM):
            y = y2[:, h * HEAD_DIM:(h + 1) * HEAD_DIM]
            cols = slice(hp * MXU_N + h * HEAD_DIM, hp * MXU_N + (h + 1) * HEAD_DIM)
            r = lax.rsqrt(jnp.mean(y * y, axis=-1, keepdims=True) + RMS_EPS)
            sw = pltpu.roll(y, HEAD_DIM // 2, 1)
            qk_ref[:, cols] = ((y * c_ref[...] + sw * s_ref[...]) * r).astype(BF16)
    vt_ref[...] = lax.dot_general(wvt_ref[...], hx_ref[...], (((1,), (1,)), ((), ())),
                                  preferred_element_type=F32).astype(BF16)


def _common_in_specs(tm, d, seg):
    return [pl.BlockSpec((tm, d), lambda i, *_: (i, 0)),
            pl.BlockSpec((1, d), lambda i, *_: (0, 0)),
            pl.BlockSpec((None, None, 1, d), lambda i, *_: (seg(i), 1, 0, 0)),
            pl.BlockSpec((None, None, 1, d), lambda i, *_: (seg(i), 0, 0, 0))]


def _in_proj_even(xs, g, mods, w, w_vt, seg, tm):
    nt, d = xs.shape
    n = w.shape[1]
    n_qk = 2 * NA_WIDTH
    v_w = w_vt.shape[0]
    resident = dict(pipeline_mode=pl.Buffered(1))
    return pl.pallas_call(
        functools.partial(_in_even_kernel, tn=512, q_width=NA_WIDTH),
        grid=(nt // tm,),
        in_specs=_common_in_specs(tm, d, seg) + [
            pl.BlockSpec((d, n), lambda i: (0, 0), **resident),
            pl.BlockSpec((v_w, d), lambda i: (0, 0), **resident)],
        out_specs=[pl.BlockSpec((tm, n_qk), lambda i: (i, 0)),
                   pl.BlockSpec((v_w, tm), lambda i: (0, i)),
                   pl.BlockSpec((tm, n - n_qk), lambda i: (i, 0))],
        out_shape=[jax.ShapeDtypeStruct((nt, n_qk), BF16),
                   jax.ShapeDtypeStruct((v_w, nt), BF16),
                   jax.ShapeDtypeStruct((nt, n - n_qk), F32)],
        scratch_shapes=[pltpu.VMEM((tm, d), BF16)],
        compiler_params=pltpu.CompilerParams(dimension_semantics=("parallel",),
                                             vmem_limit_bytes=VMEM_LIMIT_BIG),
        name="in_proj_even",
    )(xs, g.reshape(1, d), mods, mods, w, w_vt)


def _in_proj_odd(xs, g, mods, w_qk, w_vt, ctab, stab, seg, tm, tiles_per_batch, n_lat_tiles, q_width):
    nt, d = xs.shape
    n_qk = w_qk.shape[1]
    kv_w = w_vt.shape[0]
    rope_row = lambda i: jnp.where(i < n_lat_tiles, i % tiles_per_batch, tiles_per_batch)
    tab_spec = lambda which: pl.BlockSpec((None, tm, HEAD_DIM), lambda i: (which, rope_row(i), 0))
    resident = dict(pipeline_mode=pl.Buffered(1))
    return pl.pallas_call(
        functools.partial(_in_odd_kernel, q_width=q_width),
        grid=(nt // tm,),
        in_specs=_common_in_specs(tm, d, seg) + [
            pl.BlockSpec((d, n_qk), lambda i: (0, 0), **resident),
            pl.BlockSpec((kv_w, d), lambda i: (0, 0), **resident),
            tab_spec(0), tab_spec(0), tab_spec(1), tab_spec(1)],
        out_specs=[pl.BlockSpec((tm, n_qk), lambda i: (i, 0)),
                   pl.BlockSpec((kv_w, tm), lambda i: (0, i))],
        out_shape=[jax.ShapeDtypeStruct((nt, n_qk), BF16),
                   jax.ShapeDtypeStruct((kv_w, nt), BF16)],
        scratch_shapes=[pltpu.VMEM((tm, d), BF16)],
        compiler_params=_params(("parallel",)),
        name="in_proj_odd",
    )(xs, g.reshape(1, d), mods, mods, w_qk, w_vt, ctab, stab, ctab, stab)


def _na_kernel(q_ref, k_ref, vt_ref, qc_ref, kc_ref, vtc_ref, bias_ref, o_ref, oc_ref, *, n_rows, blocks_per_step):
    c = pl.program_id(2)
    bq = NA_BLOCK_ROWS * GRID_W
    span = NA_SPAN_ROWS * GRID_W
    n_blocks = n_rows // NA_BLOCK_ROWS
    dn = (((1,), (1,)), ((), ()))
    kc = kc_ref[...]
    vtc = vtc_ref[...]

    def span_start(j):
        blk = c * blocks_per_step + j
        kr_base = jnp.clip(blk * NA_BLOCK_ROWS - NA_WIN_ROWS // 2, 0, n_rows - NA_SPAN_ROWS)
        return blk, pl.multiple_of(kr_base * GRID_W, NA_BLOCK_ROWS * GRID_W)

    def scores(j):
        blk, start = span_start(j)
        kind = jnp.where(blk == 0, 0, jnp.where(blk == n_blocks - 1, 2, 1))
        q = q_ref[j * bq:(j + 1) * bq, :]
        s_nb = lax.dot_general(k_ref[pl.ds(start, span), :], q, dn, preferred_element_type=F32) + bias_ref[kind]
        s_cx = lax.dot_general(kc, q, dn, preferred_element_type=F32)
        return s_nb, s_cx

    def finish(j, s_nb, s_cx):
        _, start = span_start(j)
        m = jnp.maximum(s_nb.max(axis=0, keepdims=True), s_cx.max(axis=0, keepdims=True))
        p_nb = jnp.exp2(s_nb - m)
        p_cx = jnp.exp2(s_cx - m)
        l = p_nb.sum(axis=0, keepdims=True) + p_cx.sum(axis=0, keepdims=True)
        o = (jnp.dot(vt_ref[:, pl.ds(start, span)], p_nb.astype(BF16), preferred_element_type=F32)
             + jnp.dot(vtc, p_cx.astype(BF16), preferred_element_type=F32))
        o_ref[j * bq:(j + 1) * bq, :] = (o / l).T.astype(BF16)

    cur = scores(0)
    for j in range(blocks_per_step):
        nxt = scores(j + 1) if j + 1 < blocks_per_step else None
        finish(j, *cur)
        cur = nxt

    @pl.when(c == pl.num_programs(2) - 1)
    def _():
        s = lax.dot_general(kc, qc_ref[...], dn, preferred_element_type=F32)
        p = jnp.exp2(s - s.max(axis=0, keepdims=True))
        l = p.sum(axis=0, keepdims=True)
        o = jnp.dot(vtc, p.astype(BF16), preferred_element_type=F32)
        oc_ref[...] = (o / l).T.astype(BF16)


def _na_bias_table(rpb, n_rows):
    n_heads = rpb.shape[0]
    qc = jnp.arange(GRID_W)[:, None]
    kc = jnp.arange(GRID_W)[None, :]
    kc0 = jnp.clip(qc - NA_WIN_COLS // 2, 0, GRID_W - NA_WIN_COLS)
    valid = (kc >= kc0) & (kc < kc0 + NA_WIN_COLS)
    off_c = kc - qc + (NA_WIN_COLS - 1)
    col = jnp.zeros((n_heads, 2 * NA_WIN_ROWS - 1, GRID_W, GRID_W), F32)
    for o in range(2 * NA_WIN_COLS - 1):
        col = col + jnp.where((off_c == o)[None, None], rpb[:, :, o, None, None].astype(F32), 0.0)
    col = jnp.where(valid[None, None], col * LOG2E, MASK_VALUE).transpose(0, 1, 3, 2)
    tables = []
    for r0, kr_base in ((0, 0), (NA_BLOCK_ROWS, 0), (n_rows - NA_BLOCK_ROWS, n_rows - NA_SPAN_ROWS)):
        per_row = []
        for t in range(NA_BLOCK_ROWS):
            q_row = r0 + t
            win0 = min(max(q_row - NA_WIN_ROWS // 2, 0), n_rows - NA_WIN_ROWS)
            u_lo = win0 - kr_base
            off_lo = win0 - q_row + (NA_WIN_ROWS - 1)
            assert 0 <= u_lo <= NA_SPAN_ROWS - NA_WIN_ROWS
            per_row.append(jnp.pad(col[:, off_lo:off_lo + NA_WIN_ROWS],
                                   ((0, 0), (u_lo, NA_SPAN_ROWS - NA_WIN_ROWS - u_lo), (0, 0), (0, 0)),
                                   constant_values=MASK_VALUE))
        tb = jnp.stack(per_row, axis=3)
        tables.append(tb.reshape(n_heads, NA_SPAN_ROWS * GRID_W, NA_BLOCK_ROWS * GRID_W))
    return jnp.stack(tables)


def _na_attention(qk, vt, bias_tbl, n_batch, seq, ctx_len):
    n_rows = seq // GRID_W
    bq = NA_BLOCK_ROWS * GRID_W
    span = NA_SPAN_ROWS * GRID_W
    step_rows = min(2048, seq)
    assert ctx_len == bq and n_rows % NA_BLOCK_ROWS == 0 and n_rows >= NA_SPAN_ROWS + NA_BLOCK_ROWS
    steps = seq // step_rows
    ctx_blk0 = n_batch * seq // ctx_len
    h_k = NA_HEADS
    return pl.pallas_call(
        functools.partial(_na_kernel, n_rows=n_rows, blocks_per_step=step_rows // bq),
        grid=(n_batch, NA_HEADS, steps),
        in_specs=[pl.BlockSpec((step_rows, HEAD_DIM), lambda b, h, c: (b * steps + c, h)),
                  pl.BlockSpec((seq, HEAD_DIM), lambda b, h, c: (b, h_k + h)),
                  pl.BlockSpec((HEAD_DIM, seq), lambda b, h, c: (h, b)),
                  pl.BlockSpec((ctx_len, HEAD_DIM), lambda b, h, c: (ctx_blk0 + b, h)),
                  pl.BlockSpec((ctx_len, HEAD_DIM), lambda b, h, c: (ctx_blk0 + b, h_k + h)),
                  pl.BlockSpec((HEAD_DIM, ctx_len), lambda b, h, c: (h, ctx_blk0 + b)),
                  pl.BlockSpec((3, None, span, bq), lambda b, h, c: (0, h, 0, 0))],
        out_specs=[pl.BlockSpec((step_rows, HEAD_DIM), lambda b, h, c: (b * steps + c, h)),
                   pl.BlockSpec((ctx_len, HEAD_DIM), lambda b, h, c: (b, h))],
        out_shape=[jax.ShapeDtypeStruct((n_batch * seq, NA_WIDTH), BF16),
                   jax.ShapeDtypeStruct((n_batch * ctx_len, NA_WIDTH), BF16)],
        compiler_params=_params(("parallel", "parallel", "arbitrary")),
        name="neighbourhood_attention",
    )(qk, qk, vt, qk, qk, vt, bias_tbl)


def _sg_kernel(uz_ref, ws_ref, bs_ref, lg_ref, lb_ref, o_ref, *, n_chunks):
    for g in range(SG_GROUPS):
        w = ws_ref[g].astype(BF16)
        for n in range(n_chunks):
            rows = slice(n * SG_CHUNK, (n + 1) * SG_CHUNK)
            u = uz_ref[rows, g * SG_DIM:(g + 1) * SG_DIM]
            z = uz_ref[rows, SG_WIDTH + g * SG_DIM:SG_WIDTH + (g + 1) * SG_DIM]
            zz = _gelu_tanh(z)
            mu = jnp.mean(zz, axis=-1, keepdims=True)
            xc = zz - mu
            var = jnp.mean(xc * xc, axis=-1, keepdims=True)
            zn = xc * lax.rsqrt(var + LN_EPS) * lg_ref[g] + lb_ref[g]
            mixed = jnp.dot(w, zn.astype(BF16), preferred_element_type=F32) + bs_ref[g]
            o_ref[rows, g * SG_DIM:(g + 1) * SG_DIM] = (_gelu_tanh(u) * mixed).astype(BF16)


def _spatial_gating(uz, w_s, b_s, ln_g, ln_b, tm):
    nt = uz.shape[0]
    full3 = lambda i: (0, 0, 0)
    return pl.pallas_call(
        functools.partial(_sg_kernel, n_chunks=tm // SG_CHUNK),
        grid=(nt // tm,),
        in_specs=[pl.BlockSpec((tm, 2 * SG_WIDTH), lambda i: (i, 0)),
                  pl.BlockSpec((SG_GROUPS, SG_CHUNK, SG_CHUNK), full3),
                  pl.BlockSpec((SG_GROUPS, SG_CHUNK, 1), full3),
                  pl.BlockSpec((SG_GROUPS, 1, SG_DIM), full3),
                  pl.BlockSpec((SG_GROUPS, 1, SG_DIM), full3)],
        out_specs=pl.BlockSpec((tm, SG_WIDTH), lambda i: (i, 0)),
        out_shape=jax.ShapeDtypeStruct((nt, SG_WIDTH), BF16),
        compiler_params=_params(("parallel",)),
        name="spatial_gating",
    )(uz, w_s, b_s.reshape(SG_GROUPS, SG_CHUNK, 1), ln_g.reshape(SG_GROUPS, 1, SG_DIM),
      ln_b.reshape(SG_GROUPS, 1, SG_DIM))


def _flash_kernel(q_ref, qn_ref, k_ref, vt_ref, kc_ref, vtc_ref, o_ref, q_s, qn_s, s_a, s_b, s_c, acc_s, m_s,
                  *, tq, tk, seq, group, n_lat_q):
    qi = pl.program_id(2)
    for g in range(group):
        q_s[g * tq:(g + 1) * tq, :] = q_ref[:, g * HEAD_DIM:(g + 1) * HEAD_DIM]
    m_rows = group * tq
    n_blk = m_rows // MXU_N
    n_tiles = seq // tk
    dn = (((1,), (1,)), ((), ()))
    blk_cols = lambda n: slice(n * MXU_N, (n + 1) * MXU_N)

    def scores(k, s_ref, n):
        s_ref[:, blk_cols(n)] = lax.dot_general(k, q_s[blk_cols(n), :], dn, preferred_element_type=F32)

    def with_ones_row(vt):
        ones = jnp.where(lax.broadcasted_iota(jnp.int32, (BF16_ROWS, vt.shape[1]), 0) == 0, 1.0, 0.0)
        return jnp.concatenate([vt, ones.astype(BF16)], axis=0)

    def consume(s_ref, vt_aug, n):
        cols = blk_cols(n)
        s = s_ref[:, cols]
        m_old = m_s[:, cols]
        m_new = jnp.maximum(m_old, s.max(axis=0, keepdims=True))
        alpha = jnp.exp2(m_old - m_new)
        p = jnp.exp2(s - m_new).astype(BF16)
        acc_s[:, cols] = alpha * acc_s[:, cols] + jnp.dot(vt_aug, p, preferred_element_type=F32)
        m_s[:, cols] = m_new

    def scores_and_consume(k_next, s_next, s_cur, vt_cur):
        vt_aug = with_ones_row(vt_cur)
        for n in range(n_blk):
            scores(k_next, s_next, n)
            consume(s_cur, vt_aug, n)

    k_tile = lambda t: k_ref[pl.ds(pl.multiple_of(t * tk, tk), tk), :]
    vt_tile = lambda t: vt_ref[:, pl.ds(pl.multiple_of(t * tk, tk), tk)]

    m_s[...] = jnp.full(m_s.shape, MASK_VALUE, F32)
    acc_s[...] = jnp.zeros(acc_s.shape, F32)

    @pl.when(qi < n_lat_q)
    def _():
        @pl.when(qi == 0)
        def _():
            for n in range(n_blk):
                scores(k_tile(0), s_a, n)

        def pair(u, carry):
            t0 = 2 * u
            scores_and_consume(k_tile(t0 + 1), s_b, s_a, vt_tile(t0))
            scores_and_consume(k_tile(t0 + 2), s_a, s_b, vt_tile(t0 + 1))
            return carry

        lax.fori_loop(0, n_tiles // 2 - 1, pair, 0)
        scores_and_consume(k_tile(n_tiles - 1), s_b, s_a, vt_tile(n_tiles - 2))

        @pl.when(qi + 1 < n_lat_q)
        def _():
            for g in range(group):
                qn_s[g * tq:(g + 1) * tq, :] = qn_ref[:, g * HEAD_DIM:(g + 1) * HEAD_DIM]
            vt_aug = with_ones_row(vt_tile(n_tiles - 1))
            for n in range(n_blk):
                scores(kc_ref[...], s_c, n)
                s_a[:, blk_cols(n)] = lax.dot_general(k_tile(0), qn_s[blk_cols(n), :], dn,
                                                      preferred_element_type=F32)
                consume(s_b, vt_aug, n)

        @pl.when(qi + 1 >= n_lat_q)
        def _():
            scores_and_consume(kc_ref[...], s_c, s_b, vt_tile(n_tiles - 1))

        vtc_aug = with_ones_row(vtc_ref[...])
        for n in range(n_blk):
            consume(s_c, vtc_aug, n)

    @pl.when(qi == n_lat_q)
    def _():
        vtc_aug = with_ones_row(vtc_ref[...])
        for n in range(n_blk):
            scores(kc_ref[...], s_c, n)
            consume(s_c, vtc_aug, n)

    o = acc_s[:HEAD_DIM, :] / acc_s[HEAD_DIM:HEAD_DIM + 1, :]
    for g in range(group):
        o_ref[:, g * HEAD_DIM:(g + 1) * HEAD_DIM] = o[:, g * tq:(g + 1) * tq].T.astype(BF16)


def _flash_attention(qk, vt, n_batch, seq, ctx_len, n_q_heads):
    nt = qk.shape[0]
    group = n_q_heads // GQA_KV_HEADS
    tq = ctx_len
    tk = min(FLASH_KEY_TILE, seq // 2)
    assert seq % (2 * tk) == 0 and (group * tq) % MXU_N == 0
    n_lat_q = seq // tq
    ctx_blk0 = n_batch * seq // tq
    hk0 = n_q_heads
    gw = group * HEAD_DIM
    m_rows = group * tq

    def q_idx(b, kh, qi):
        return (jnp.where(qi < n_lat_q, b * n_lat_q + qi, ctx_blk0 + b), kh)

    def q_next_idx(b, kh, qi):
        return (b * n_lat_q + jnp.minimum(qi + 1, n_lat_q - 1), kh)

    return pl.pallas_call(
        functools.partial(_flash_kernel, tq=tq, tk=tk, seq=seq, group=group, n_lat_q=n_lat_q),
        grid=(n_batch, GQA_KV_HEADS, n_lat_q + 1),
        in_specs=[pl.BlockSpec((tq, gw), q_idx),
                  pl.BlockSpec((tq, gw), q_next_idx),
                  pl.BlockSpec((seq, HEAD_DIM), lambda b, kh, qi: (b, hk0 + kh)),
                  pl.BlockSpec((HEAD_DIM, seq), lambda b, kh, qi: (kh, b)),
                  pl.BlockSpec((tq, HEAD_DIM), lambda b, kh, qi: (ctx_blk0 + b, hk0 + kh)),
                  pl.BlockSpec((HEAD_DIM, tq), lambda b, kh, qi: (kh, ctx_blk0 + b))],
        out_specs=pl.BlockSpec((tq, gw), q_idx),
        out_shape=jax.ShapeDtypeStruct((nt, n_q_heads * HEAD_DIM), BF16),
        scratch_shapes=[pltpu.VMEM((m_rows, HEAD_DIM), BF16), pltpu.VMEM((m_rows, HEAD_DIM), BF16),
                        pltpu.VMEM((tk, m_rows), F32), pltpu.VMEM((tk, m_rows), F32),
                        pltpu.VMEM((tq, m_rows), F32),
                        pltpu.VMEM((HEAD_DIM + BF16_ROWS, m_rows), F32),
                        pltpu.VMEM((1, m_rows), F32)],
        compiler_params=_params(("parallel", "parallel", "arbitrary")),
        name="gqa_flash_attention",
    )(qk, qk, qk, vt, qk, vt)


def _out_kernel(*refs, n_lhs):
    lhs_refs = refs[:n_lhs]
    (w_ref, x_ref, ga_ref, g2_ref, sc_ref, sh_ref, wr_ref, br_ref, xo_ref, tok_ref, lg_ref) = refs[n_lhs:]
    acc = None
    k0 = 0
    for a_ref in lhs_refs:
        kp = a_ref.shape[1]
        part = jnp.dot(a_ref[...], w_ref[k0:k0 + kp, :], preferred_element_type=F32)
        acc = part if acc is None else acc + part
        k0 += kp
    xn = x_ref[...] + ga_ref[...] * acc
    xo_ref[...] = xn
    tok = _rms_modulate(xn, g2_ref[...], sc_ref[...], sh_ref[...])
    tok_ref[...] = _pack_bf16_pairs(tok)
    lg_ref[...] = lax.dot_general(wr_ref[...], tok.astype(BF16), (((1,), (1,)), ((), ())),
                                  preferred_element_type=F32) + br_ref[...]


def _out_proj(lhs, w, xs, mods, g2, w_router, b_router, seg, tm):
    nt, d = xs.shape
    k = w.shape[0]
    modspec = lambda which: pl.BlockSpec((None, None, 1, d), lambda i: (seg(i), which, 0, 0))
    return pl.pallas_call(
        functools.partial(_out_kernel, n_lhs=len(lhs)),
        grid=(nt // tm,),
        in_specs=[pl.BlockSpec((tm, a.shape[1]), lambda i: (i, 0)) for a in lhs] + [
            pl.BlockSpec((k, d), lambda i: (0, 0)),
            pl.BlockSpec((tm, d), lambda i: (i, 0)),
            modspec(2),
            pl.BlockSpec((1, d), lambda i: (0, 0)),
            modspec(4),
            modspec(3),
            pl.BlockSpec((ROUTER_W, d), lambda i: (0, 0)),
            pl.BlockSpec((ROUTER_W, 1), lambda i: (0, 0))],
        out_specs=[pl.BlockSpec((tm, d), lambda i: (i, 0)),
                   pl.BlockSpec((tm, d // 2), lambda i: (i, 0)),
                   pl.BlockSpec((ROUTER_W, tm), lambda i: (0, i))],
        out_shape=[jax.ShapeDtypeStruct((nt, d), F32),
                   jax.ShapeDtypeStruct((nt, d // 2), jnp.uint32),
                   jax.ShapeDtypeStruct((ROUTER_W, nt), F32)],
        compiler_params=_params(("parallel",)),
        name="out_proj_residual_router",
    )(*lhs, w, xs, mods, g2.reshape(1, d), mods, mods, w_router, b_router)


def _expert_kernel(blk_e_ref, n_used_ref, xs_ref, w1_ref, w3_ref, w2_ref, o_ref):
    b = pl.program_id(0)

    @pl.when(b < n_used_ref[0])
    def _():
        x = _unpack_bf16_pairs(xs_ref[...]).astype(BF16)
        h1 = jnp.dot(x, w1_ref[...].astype(BF16), preferred_element_type=F32)
        h3 = jnp.dot(x, w3_ref[...].astype(BF16), preferred_element_type=F32)
        a = (h1 * _sigmoid(h1) * h3).astype(BF16)
        o_ref[...] = _pack_bf16_pairs(jnp.dot(a, w2_ref[...].astype(BF16), preferred_element_type=F32))

    @pl.when(b >= n_used_ref[0])
    def _():
        o_ref[...] = jnp.zeros_like(o_ref)


def _experts(xs, blk_e, n_used, w1, w3, w2, layer):
    p, dw = xs.shape
    d, hid = w1.shape[-2:]
    nb = p // MOE_ROWS
    grid_spec = pltpu.PrefetchScalarGridSpec(
        num_scalar_prefetch=2,
        grid=(nb,),
        in_specs=[pl.BlockSpec((MOE_ROWS, dw), lambda b, be, nu: (jnp.minimum(b, nu[0] - 1), 0)),
                  pl.BlockSpec((None, None, d, hid), lambda b, be, nu: (layer, be[b], 0, 0)),
                  pl.BlockSpec((None, None, d, hid), lambda b, be, nu: (layer, be[b], 0, 0)),
                  pl.BlockSpec((None, None, hid, d), lambda b, be, nu: (layer, be[b], 0, 0))],
        out_specs=pl.BlockSpec((MOE_ROWS, dw), lambda b, be, nu: (b, 0)),
    )
    return pl.pallas_call(
        _expert_kernel,
        grid_spec=grid_spec,
        out_shape=jax.ShapeDtypeStruct((p, dw), jnp.uint32),
        compiler_params=_params(("arbitrary",)),
        name="moe_experts",
    )(blk_e, n_used, xs, w1, w3, w2)


def _first_index(vals, target):
    idx = jnp.full(target.shape, len(vals) - 1, jnp.int32)
    for i in range(len(vals) - 2, -1, -1):
        idx = jnp.where(vals[i] == target, i, idx)
    return idx


def _router_kernel(lg_ref, gate_ref, pos_ref, blk_ref, nused_ref, e_s, rank_s, *, n_tiles):
    tw = ROUTE_TILE
    tri = jnp.where(lax.broadcasted_iota(jnp.int32, (tw, tw), 0) < lax.broadcasted_iota(jnp.int32, (tw, tw), 1),
                    1.0, 0.0).astype(BF16)
    eid = lax.broadcasted_iota(jnp.int32, (MOE_EXPERTS, tw), 0)
    epg = MOE_EXPERTS_PER_GROUP

    def pass1(c, run):
        sl = pl.ds(pl.multiple_of(c * tw, tw), tw)
        g = [lg_ref[i:i + 1, sl] for i in range(MOE_GROUPS)]
        gm = functools.reduce(jnp.maximum, g)
        gidx = _first_index(g, gm)
        gval = 1.0 / functools.reduce(lambda a, b: a + b, [jnp.exp(gi - gm) for gi in g])
        le = []
        for e in range(epg):
            sel = lg_ref[MOE_GROUPS + (MOE_GROUPS - 1) * epg + e:MOE_GROUPS + (MOE_GROUPS - 1) * epg + e + 1, sl]
            for gg in range(MOE_GROUPS - 2, -1, -1):
                sel = jnp.where(gidx == gg, lg_ref[MOE_GROUPS + gg * epg + e:MOE_GROUPS + gg * epg + e + 1, sl], sel)
            le.append(sel)
        m1 = functools.reduce(jnp.maximum, le)
        i1 = _first_index(le, m1)
        le2 = [jnp.where(i1 == e, -jnp.inf, le[e]) for e in range(epg)]
        m2 = functools.reduce(jnp.maximum, le2)
        i2 = _first_index(le2, m2)
        t = jnp.exp(m2 - m1)
        inv = 1.0 / (1.0 + t)
        gate_ref[0:1, sl] = inv * gval
        gate_ref[1:2, sl] = t * inv * gval
        e0 = gidx * epg + i1
        e1 = gidx * epg + i2
        oh0 = eid == e0
        oh1 = eid == e1
        oh0f = jnp.where(oh0, 1.0, 0.0)
        oh1f = jnp.where(oh1, 1.0, 0.0)
        pre0 = jnp.dot(oh0f.astype(BF16), tri, preferred_element_type=F32)
        pre1 = jnp.dot(oh1f.astype(BF16), tri, preferred_element_type=F32)
        c0 = oh0f.sum(axis=1, keepdims=True)
        c1 = oh1f.sum(axis=1, keepdims=True)
        rank_s[0:1, sl] = jnp.where(oh0, run + pre0, 0.0).sum(axis=0, keepdims=True)
        rank_s[1:2, sl] = jnp.where(oh1, run + c0 + pre1, 0.0).sum(axis=0, keepdims=True)
        e_s[0:1, sl] = e0
        e_s[1:2, sl] = e1
        return run + c0 + c1

    counts = lax.fori_loop(0, n_tiles, pass1, jnp.zeros((MOE_EXPERTS, 1), F32))
    blocks = jnp.floor((counts + (MOE_ROWS - 1)) * (1.0 / MOE_ROWS))
    lincl = jnp.where(lax.broadcasted_iota(jnp.int32, (MOE_EXPERTS, MOE_EXPERTS), 1)
                      <= lax.broadcasted_iota(jnp.int32, (MOE_EXPERTS, MOE_EXPERTS), 0), 1.0, 0.0).astype(BF16)
    end_blocks = jnp.dot(lincl, jnp.broadcast_to(blocks, (MOE_EXPERTS, LANES)).astype(BF16),
                         preferred_element_type=F32)[:, 0:1]
    start_rows = (end_blocks - blocks) * MOE_ROWS

    def pass2(c, carry):
        sl = pl.ds(pl.multiple_of(c * tw, tw), tw)
        for k in range(MOE_TOP_K):
            base = jnp.where(eid == e_s[k:k + 1, sl], start_rows, 0.0).sum(axis=0, keepdims=True)
            pos_ref[k:k + 1, sl] = (base + rank_s[k:k + 1, sl]).astype(jnp.int32)
        return carry

    lax.fori_loop(0, n_tiles, pass2, 0)
    bl = lax.broadcasted_iota(jnp.int32, (MOE_EXPERTS, PLAN_LANES), 1).astype(F32)
    blk = jnp.where(end_blocks <= bl, 1.0, 0.0).sum(axis=0, keepdims=True)
    blk_ref[...] = jnp.minimum(blk, MOE_EXPERTS - 1.0).astype(jnp.int32)
    nused_ref[...] = jnp.broadcast_to(end_blocks[MOE_EXPERTS - 1:MOE_EXPERTS, :], (1, LANES)).astype(jnp.int32)


def _router(lg_t):
    n = lg_t.shape[1]
    assert n % ROUTE_TILE == 0
    return pl.pallas_call(
        functools.partial(_router_kernel, n_tiles=n // ROUTE_TILE),
        out_shape=[jax.ShapeDtypeStruct((MOE_TOP_K, n), F32),
                   jax.ShapeDtypeStruct((MOE_TOP_K, n), jnp.int32),
                   jax.ShapeDtypeStruct((1, PLAN_LANES), jnp.int32),
                   jax.ShapeDtypeStruct((1, LANES), jnp.int32)],
        scratch_shapes=[pltpu.VMEM((MOE_TOP_K, n), jnp.int32), pltpu.VMEM((MOE_TOP_K, n), F32)],
        compiler_params=pltpu.CompilerParams(vmem_limit_bytes=VMEM_LIMIT),
        name="moe_router_plan",
    )(lg_t)


def _dispatch_kernel(p0_ref, p1_ref, tok_ref, init_ref, xs_ref, sem, *, tm):
    del init_ref
    base = pl.program_id(0) * tm

    def copy(r, p_ref):
        return pltpu.make_async_copy(tok_ref.at[pl.ds(r, 1)], xs_ref.at[pl.ds(p_ref[base + r], 1)], sem)

    def start(r, carry):
        copy(r, p0_ref).start()
        copy(r, p1_ref).start()
        return carry

    def wait(r, carry):
        copy(r, p0_ref).wait()
        copy(r, p1_ref).wait()
        return carry

    lax.fori_loop(0, tm, start, 0, unroll=8)
    lax.fori_loop(0, tm, wait, 0, unroll=8)


def _dispatch(tok, pos0, pos1, init, tm):
    nt, d = tok.shape
    n_slots = init.shape[0]
    grid_spec = pltpu.PrefetchScalarGridSpec(
        num_scalar_prefetch=2,
        grid=(nt // tm,),
        in_specs=[pl.BlockSpec((tm, d), lambda i, p0, p1: (i, 0)),
                  pl.BlockSpec(memory_space=pl.ANY)],
        out_specs=pl.BlockSpec(memory_space=pl.ANY),
        scratch_shapes=[pltpu.SemaphoreType.DMA(())],
    )
    return pl.pallas_call(
        functools.partial(_dispatch_kernel, tm=tm),
        grid_spec=grid_spec,
        out_shape=jax.ShapeDtypeStruct((n_slots, d), tok.dtype),
        input_output_aliases={3: 0},
        compiler_params=_params(("arbitrary",)),
        name="moe_dispatch",
    )(pos0, pos1, tok, init)


def _combine_kernel(p0_ref, p1_ref, x_ref, yb_ref, gt_ref, ga_ref, *rest, tm, final):
    if final:
        gf_ref, o_ref, y0_s, y1_s, sem = rest
    else:
        o_ref, y0_s, y1_s, sem = rest
    i = pl.program_id(0)
    slot = i % 2

    def copy(tile, buf, r, p_ref, y_s):
        return pltpu.make_async_copy(yb_ref.at[pl.ds(p_ref[tile * tm + r], 1)], y_s.at[buf, pl.ds(r, 1)],
                                     sem.at[buf])

    def start_tile(tile, buf):
        def body(r, carry):
            copy(tile, buf, r, p0_ref, y0_s).start()
            copy(tile, buf, r, p1_ref, y1_s).start()
            return carry
        lax.fori_loop(0, tm, body, 0, unroll=8)

    def wait_tile(tile, buf):
        def body(r, carry):
            copy(tile, buf, r, p0_ref, y0_s).wait()
            copy(tile, buf, r, p1_ref, y1_s).wait()
            return carry
        lax.fori_loop(0, tm, body, 0, unroll=8)

    @pl.when(i == 0)
    def _():
        start_tile(i, slot)

    @pl.when(i + 1 < pl.num_programs(0))
    def _():
        start_tile(i + 1, 1 - slot)

    wait_tile(i, slot)
    gt = gt_ref[...]
    f = gt[:, 0:1] * _unpack_bf16_pairs(y0_s[slot]) + gt[:, 1:2] * _unpack_bf16_pairs(y1_s[slot])
    xn = x_ref[...] + ga_ref[...] * f
    if final:
        r = lax.rsqrt(jnp.mean(xn * xn, axis=-1, keepdims=True) + RMS_EPS)
        xn = xn * r * gf_ref[...]
    o_ref[...] = xn


def _combine(xs, yb, pos0, pos1, gate, mods, seg, tm, final_g=None, n_out_rows=None):
    nt, d = xs.shape
    rows = nt if n_out_rows is None else n_out_rows
    row_blk = lambda i, p0, p1: (i, 0)
    in_specs = [pl.BlockSpec((tm, d), row_blk),
                pl.BlockSpec(memory_space=pl.ANY),
                pl.BlockSpec((tm, MOE_TOP_K), row_blk),
                pl.BlockSpec((None, None, 1, d), lambda i, p0, p1: (seg(i), 5, 0, 0))]
    args = [xs, yb, gate, mods]
    if final_g is not None:
        in_specs.append(pl.BlockSpec((1, d), lambda i, p0, p1: (0, 0)))
        args.append(final_g.reshape(1, d))
    grid_spec = pltpu.PrefetchScalarGridSpec(
        num_scalar_prefetch=2,
        grid=(rows // tm,),
        in_specs=in_specs,
        out_specs=pl.BlockSpec((tm, d), row_blk),
        scratch_shapes=[pltpu.VMEM((2, tm, d // 2), jnp.uint32), pltpu.VMEM((2, tm, d // 2), jnp.uint32),
                        pltpu.SemaphoreType.DMA((2,))],
    )
    return pl.pallas_call(
        functools.partial(_combine_kernel, tm=tm, final=final_g is not None),
        grid_spec=grid_spec,
        out_shape=jax.ShapeDtypeStruct((rows, d), F32),
        compiler_params=_params(("arbitrary",)),
        name="moe_combine_residual",
    )(pos0, pos1, *args)


def _rope_tables(seq, pad_rows):
    t = jnp.arange(seq, dtype=jnp.int32)
    row = (t // GRID_W).astype(F32)
    col = (t % GRID_W).astype(F32)
    inv_freq = ROPE_THETA ** (-jnp.arange(ROPE_AXIS_DIM // 2, dtype=F32) * 2.0 / ROPE_AXIS_DIM)
    ang = jnp.concatenate([row[:, None] * inv_freq, col[:, None] * inv_freq], axis=-1)
    cos, sin = jnp.cos(ang), jnp.sin(ang)
    cos = jnp.concatenate([cos, jnp.ones((pad_rows, HEAD_DIM // 2), F32)], axis=0)
    sin = jnp.concatenate([sin, jnp.zeros((pad_rows, HEAD_DIM // 2), F32)], axis=0)
    return cos, sin


def _split_pairs(w, n_heads):
    d = w.shape[0]
    return w.reshape(d, n_heads, HEAD_DIM // 2, 2).transpose(0, 1, 3, 2).reshape(d, n_heads * HEAD_DIM)


def _gain_rope_tables(cos, sin, q_gain, k_gain):
    def tables(g):
        ge, go = g[0::2][None, :], g[1::2][None, :]
        return (jnp.concatenate([ge * cos, go * cos], axis=1),
                jnp.concatenate([-go * sin, ge * sin], axis=1))
    cq, sq = tables(q_gain)
    ck, sk = tables(k_gain)
    return jnp.stack([cq, ck]), jnp.stack([sq, sk])


def kernel(x, c, ctx, c_ctx, mod_w, mod_b, norm_mix_g, norm_ffn_g, norm_final_g, na_sg_w_in, na_sg_w_out, na_rpb, sg_w_s, sg_b_s, sg_ln_g, sg_ln_b, gqa_w_in, gqa_w_out, gqa_q_gain, gqa_k_gain, moe_w_group, moe_b_group, moe_w_expert, moe_b_expert, moe_w1, moe_w3, moe_w2):
    n_batch, seq, d = x.shape
    ctx_len = ctx.shape[1]
    depth = mod_w.shape[0]
    n_lat = n_batch * seq
    tm = min(ROW_TILE, ctx_len * n_batch)
    assert seq % tm == 0 and (n_batch * ctx_len) % tm == 0 and seq % ctx_len == 0
    assert ctx_len % GRID_W == 0 and n_batch + 1 <= 8
    tiles_per_batch = seq // tm
    seg = _seg_fn(tiles_per_batch, n_batch)
    n_q_heads = d // HEAD_DIM
    n_tok = n_lat + n_batch * ctx_len
    n_blocks = -(-n_tok * MOE_TOP_K // MOE_ROWS) + MOE_EXPERTS
    assert n_blocks <= PLAN_LANES and n_tok % ROUTE_TILE == 0

    xs = jnp.concatenate([x.reshape(n_lat, d), ctx.reshape(n_batch * ctx_len, d)], axis=0)
    cond = jnp.zeros((8, d), F32).at[:n_batch].set(c).at[n_batch].set(c_ctx)
    mod_all = _modulation(cond, mod_w, mod_b)
    cos, sin = _rope_tables(seq, tm)

    q_w = n_q_heads * HEAD_DIM
    kv_w = GQA_KV_HEADS * HEAD_DIM
    out = None
    slots = jnp.zeros((n_blocks * MOE_ROWS, d // 2), jnp.uint32)
    for layer in range(depth):
        last = layer == depth - 1
        mods = mod_all[layer, :n_batch + 1].reshape(n_batch + 1, 6, 1, d)
        i = layer // 2
        if layer % 2 == 0:
            w_in = na_sg_w_in[i]
            w_main = jnp.concatenate([w_in[:, :2 * NA_WIDTH], w_in[:, 3 * NA_WIDTH:]], axis=1).astype(BF16)
            w_vt = w_in[:, 2 * NA_WIDTH:3 * NA_WIDTH].T.astype(BF16)
            qk, vt, uz = _in_proj_even(xs, norm_mix_g[layer], mods, w_main, w_vt, seg, tm)
            att_lat, att_ctx = _na_attention(qk, vt, _na_bias_table(na_rpb[i], seq // GRID_W), n_batch, seq, ctx_len)
            att = jnp.concatenate([att_lat, att_ctx], axis=0)
            gat = _spatial_gating(uz, sg_w_s[i], sg_b_s[i], sg_ln_g[i], sg_ln_b[i], tm)
            lhs, w_out = [att, gat], na_sg_w_out[i].astype(BF16)
        else:
            ctab, stab = _gain_rope_tables(cos, sin, gqa_q_gain[i] * (ATTN_SCALE * LOG2E), gqa_k_gain[i])
            w_qk = _split_pairs(gqa_w_in[i][:, :q_w + kv_w], n_q_heads + GQA_KV_HEADS).astype(BF16)
            w_vt = gqa_w_in[i][:, q_w + kv_w:].T.astype(BF16)
            qk, vt = _in_proj_odd(xs, norm_mix_g[layer], mods, w_qk, w_vt, ctab, stab,
                                  seg, tm, tiles_per_batch, n_lat // tm, q_w)
            att = _flash_attention(qk, vt, n_batch, seq, ctx_len, n_q_heads)
            lhs, w_out = [att], gqa_w_out[i].astype(BF16)

        pad = ROUTER_W - MOE_GROUPS - MOE_EXPERTS
        w_router = jnp.concatenate([moe_w_group[layer], moe_w_expert[layer], jnp.zeros((d, pad), F32)],
                                   axis=1).T.astype(BF16)
        b_router = jnp.concatenate([moe_b_group[layer], moe_b_expert[layer], jnp.zeros((pad,), F32)])[:, None]
        xs, tok, lg_t = _out_proj(lhs, w_out, xs, mods, norm_ffn_g[layer], w_router, b_router, seg, tm)

        gate, pos, blk, nused = _router(lg_t)
        pos0, pos1 = pos[0], pos[1]
        slots = _dispatch(tok, pos0, pos1, slots, tm)
        yb = _experts(slots, blk[0, :n_blocks], nused[0, :1], moe_w1, moe_w3, moe_w2, layer)
        if last:
            out = _combine(xs, yb, pos0, pos1, gate.T, mods, seg, tm, final_g=norm_final_g, n_out_rows=n_lat)
        else:
            xs = _combine(xs, yb, pos0, pos1, gate.T, mods, seg, tm)
    return out.reshape(n_batch, seq, d)
```

```python
import functools

import jax
import jax.numpy as jnp
from jax import lax
from jax.experimental import pallas as pl
from jax.experimental.pallas import tpu as pltpu

F32 = jnp.float32
BF16 = jnp.bfloat16

GRID_W = 64
HEAD_DIM = 128
NA_HEADS = 8
NA_WIDTH = NA_HEADS * HEAD_DIM
NA_WIN_ROWS = 8
NA_WIN_COLS = 16
NA_BLOCK_ROWS = 4
NA_SPAN_ROWS = 12
SG_GROUPS = 8
SG_DIM = 128
SG_WIDTH = SG_GROUPS * SG_DIM
SG_CHUNK = 128
GQA_KV_HEADS = 4
ROPE_THETA = 10000.0
ROPE_AXIS_DIM = HEAD_DIM // 2
MOE_GROUPS = 4
MOE_EXPERTS_PER_GROUP = 8
MOE_EXPERTS = MOE_GROUPS * MOE_EXPERTS_PER_GROUP
MOE_TOP_K = 2
RMS_EPS = 1e-6
LN_EPS = 1e-5

LOG2E = 1.4426950408889634
MASK_VALUE = -1e30
ATTN_SCALE = HEAD_DIM ** -0.5
LANES = 128
MXU_N = 256
BF16_ROWS = 16
ROUTER_W = LANES
ROW_TILE = 512
MOE_ROWS = 512
FLASH_KEY_TILE = 1024
ROUTE_TILE = 512
PLAN_LANES = 256
VMEM_LIMIT = 48 * 1024 * 1024
VMEM_LIMIT_BIG = 56 * 1024 * 1024


def _params(sem):
    return pltpu.CompilerParams(dimension_semantics=sem, vmem_limit_bytes=VMEM_LIMIT)


def _sigmoid(x):
    return 1.0 / (1.0 + jnp.exp(-x))


def _gelu_tanh(x):
    cdf = 0.5 * (1.0 + jnp.tanh(0.7978845608028654 * (x + 0.044715 * (x * x * x))))
    return x * cdf


def _pack_bf16_pairs(x):
    c = x.shape[1] // 2
    bits = pltpu.bitcast(x.astype(jnp.bfloat16).astype(F32), jnp.uint32)
    return (bits[:, :c] >> 16) | bits[:, c:]


def _unpack_bf16_pairs(w):
    lo = pltpu.bitcast(w << 16, F32)
    hi = pltpu.bitcast(w & jnp.uint32(0xFFFF0000), F32)
    return jnp.concatenate([lo, hi], axis=1)


def _rms_modulate(x, g, sc, sh):
    r = lax.rsqrt(jnp.mean(x * x, axis=-1, keepdims=True) + RMS_EPS)
    return (x * r * g) * (1.0 + sc) + sh


def _mod_kernel(c_ref, w_ref, b_ref, o_ref):
    c = c_ref[...]
    cs = (c * _sigmoid(c)).astype(BF16)
    o_ref[0] = jnp.dot(cs, w_ref[0].astype(BF16), preferred_element_type=F32) + b_ref[0]


def _modulation(cond, mod_w, mod_b):
    depth, d, n = mod_w.shape
    tn = 1024
    return pl.pallas_call(
        _mod_kernel,
        grid=(depth, n // tn),
        in_specs=[pl.BlockSpec((8, d), lambda l, j: (0, 0)),
                  pl.BlockSpec((1, d, tn), lambda l, j: (l, 0, j)),
                  pl.BlockSpec((1, 1, tn), lambda l, j: (l, 0, j))],
        out_specs=pl.BlockSpec((1, 8, tn), lambda l, j: (l, 0, j)),
        out_shape=jax.ShapeDtypeStruct((depth, 8, n), F32),
        compiler_params=_params(("parallel", "parallel")),
        name="adaln_modulation",
    )(cond, mod_w, mod_b.reshape(depth, 1, n))


def _seg_fn(tiles_per_batch, n_batch):
    return lambda i: jnp.minimum(i // tiles_per_batch, n_batch)


def _in_prologue(x_ref, g_ref, sc_ref, sh_ref, hx_ref):
    hx_ref[...] = _rms_modulate(x_ref[...], g_ref[...], sc_ref[...], sh_ref[...]).astype(BF16)


def _in_even_kernel(x_ref, g_ref, sc_ref, sh_ref, w_ref, wvt_ref, qk_ref, vt_ref, uz_ref, hx_ref, *, tn, q_width):
    _in_prologue(x_ref, g_ref, sc_ref, sh_ref, hx_ref)
    n_qk = qk_ref.shape[1]
    uz0 = n_qk + vt_ref.shape[0]
    for j in range(n_qk // tn):
        cols = slice(j * tn, (j + 1) * tn)
        acc = jnp.dot(hx_ref[...], w_ref[:, cols], preferred_element_type=F32)
        if j * tn < q_width:
            acc = acc * (ATTN_SCALE * LOG2E)
        qk_ref[:, cols] = acc.astype(BF16)
    for j in range(uz_ref.shape[1] // tn):
        uz_ref[:, j * tn:(j + 1) * tn] = jnp.dot(hx_ref[...], w_ref[:, uz0 + j * tn:uz0 + (j + 1) * tn],
                                                 preferred_element_type=F32)
    for j in range(vt_ref.shape[0] // tn):
        vt_ref[j * tn:(j + 1) * tn, :] = lax.dot_general(
            wvt_ref[j * tn:(j + 1) * tn, :], hx_ref[...], (((1,), (1,)), ((), ())),
            preferred_element_type=F32).astype(BF16)


def _in_odd_kernel(x_ref, g_ref, sc_ref, sh_ref, w_ref, wvt_ref, cq_ref, sq_ref, ck_ref, sk_ref,
                   qk_ref, vt_ref, hx_ref, *, q_width):
    _in_prologue(x_ref, g_ref, sc_ref, sh_ref, hx_ref)
    for hp in range(qk_ref.shape[1] // MXU_N):
        c_ref, s_ref = (cq_ref, sq_ref) if hp * MXU_N < q_width else (ck_ref, sk_ref)
        y2 = jnp.dot(hx_ref[...], w_ref[:, hp * MXU_N:(hp + 1) * MXU_N], preferred_element_type=F32)
        for h in range(MXU_N // HEAD_DIM):
            y = y2[:, h * HEAD_DIM:(h + 1) * HEAD_DIM]
            cols = slice(hp * MXU_N + h * HEAD_DIM, hp * MXU_N + (h + 1) * HEAD_DIM)
            r = lax.rsqrt(jnp.mean(y * y, axis=-1, keepdims=True) + RMS_EPS)
            sw = pltpu.roll(y, HEAD_DIM // 2, 1)
            qk_ref[:, cols] = ((y * c_ref[...] + sw * s_ref[...]) * r).astype(BF16)
    vt_ref[...] = lax.dot_general(wvt_ref[...], hx_ref[...], (((1,), (1,)), ((), ())),
                                  preferred_element_type=F32).astype(BF16)


def _common_in_specs(tm, d, seg):
    return [pl.BlockSpec((tm, d), lambda i, *_: (i, 0)),
            pl.BlockSpec((1, d), lambda i, *_: (0, 0)),
            pl.BlockSpec((None, None, 1, d), lambda i, *_: (seg(i), 1, 0, 0)),
            pl.BlockSpec((None, None, 1, d), lambda i, *_: (seg(i), 0, 0, 0))]


def _in_proj_even(xs, g, mods, w, w_vt, seg, tm):
    nt, d = xs.shape
    v_w = w_vt.shape[0]
    n = w.shape[1] - v_w
    n_qk = 2 * NA_WIDTH
    resident = dict(pipeline_mode=pl.Buffered(1))
    return pl.pallas_call(
        functools.partial(_in_even_kernel, tn=512, q_width=NA_WIDTH),
        grid=(nt // tm,),
        in_specs=_common_in_specs(tm, d, seg) + [
            pl.BlockSpec((d, n + v_w), lambda i: (0, 0), **resident),
            pl.BlockSpec((v_w, d), lambda i: (0, 0), **resident)],
        out_specs=[pl.BlockSpec((tm, n_qk), lambda i: (i, 0)),
                   pl.BlockSpec((v_w, tm), lambda i: (0, i)),
                   pl.BlockSpec((tm, n - n_qk), lambda i: (i, 0))],
        out_shape=[jax.ShapeDtypeStruct((nt, n_qk), BF16),
                   jax.ShapeDtypeStruct((v_w, nt), BF16),
                   jax.ShapeDtypeStruct((nt, n - n_qk), F32)],
        scratch_shapes=[pltpu.VMEM((tm, d), BF16)],
        compiler_params=pltpu.CompilerParams(dimension_semantics=("parallel",),
                                             vmem_limit_bytes=VMEM_LIMIT_BIG),
        name="in_proj_even",
    )(xs, g.reshape(1, d), mods, mods, w, w_vt)


def _in_proj_odd(xs, g, mods, w_qk, w_vt, ctab, stab, seg, tm, tiles_per_batch, n_lat_tiles, q_width):
    nt, d = xs.shape
    n_qk = w_qk.shape[1]
    kv_w = w_vt.shape[0]
    rope_row = lambda i: jnp.where(i < n_lat_tiles, i % tiles_per_batch, tiles_per_batch)
    tab_spec = lambda which: pl.BlockSpec((None, tm, HEAD_DIM), lambda i: (which, rope_row(i), 0))
    resident = dict(pipeline_mode=pl.Buffered(1))
    return pl.pallas_call(
        functools.partial(_in_odd_kernel, q_width=q_width),
        grid=(nt // tm,),
        in_specs=_common_in_specs(tm, d, seg) + [
            pl.BlockSpec((d, n_qk), lambda i: (0, 0), **resident),
            pl.BlockSpec((kv_w, d), lambda i: (0, 0), **resident),
            tab_spec(0), tab_spec(0), tab_spec(1), tab_spec(1)],
        out_specs=[pl.BlockSpec((tm, n_qk), lambda i: (i, 0)),
                   pl.BlockSpec((kv_w, tm), lambda i: (0, i))],
        out_shape=[jax.ShapeDtypeStruct((nt, n_qk), BF16),
                   jax.ShapeDtypeStruct((kv_w, nt), BF16)],
        scratch_shapes=[pltpu.VMEM((tm, d), BF16)],
        compiler_params=_params(("parallel",)),
        name="in_proj_odd",
    )(xs, g.reshape(1, d), mods, mods, w_qk, w_vt, ctab, stab, ctab, stab)


def _na_kernel(q_ref, k_ref, vt_ref, qc_ref, kc_ref, vtc_ref, bias_ref, o_ref, oc_ref, *, n_rows, blocks_per_step):
    c = pl.program_id(2)
    bq = NA_BLOCK_ROWS * GRID_W
    span = NA_SPAN_ROWS * GRID_W
    n_blocks = n_rows // NA_BLOCK_ROWS
    dn = (((1,), (1,)), ((), ()))
    kc = kc_ref[...]
    vtc = vtc_ref[...]

    def span_start(j):
        blk = c * blocks_per_step + j
        kr_base = jnp.clip(blk * NA_BLOCK_ROWS - NA_WIN_ROWS // 2, 0, n_rows - NA_SPAN_ROWS)
        return blk, pl.multiple_of(kr_base * GRID_W, NA_BLOCK_ROWS * GRID_W)

    def scores(j):
        blk, start = span_start(j)
        kind = jnp.where(blk == 0, 0, jnp.where(blk == n_blocks - 1, 2, 1))
        q = q_ref[j * bq:(j + 1) * bq, :]
        s_nb = lax.dot_general(k_ref[pl.ds(start, span), :], q, dn, preferred_element_type=F32) + bias_ref[kind]
        s_cx = lax.dot_general(kc, q, dn, preferred_element_type=F32)
        return s_nb, s_cx

    def finish(j, s_nb, s_cx):
        _, start = span_start(j)
        m = jnp.maximum(s_nb.max(axis=0, keepdims=True), s_cx.max(axis=0, keepdims=True))
        p_nb = jnp.exp2(s_nb - m)
        p_cx = jnp.exp2(s_cx - m)
        l = p_nb.sum(axis=0, keepdims=True) + p_cx.sum(axis=0, keepdims=True)
        o = (jnp.dot(vt_ref[:, pl.ds(start, span)], p_nb.astype(BF16), preferred_element_type=F32)
             + jnp.dot(vtc, p_cx.astype(BF16), preferred_element_type=F32))
        o_ref[j * bq:(j + 1) * bq, :] = (o / l).T.astype(BF16)

    cur = scores(0)
    for j in range(blocks_per_step):
        nxt = scores(j + 1) if j + 1 < blocks_per_step else None
        finish(j, *cur)
        cur = nxt

    @pl.when(c == pl.num_programs(2) - 1)
    def _():
        s = lax.dot_general(kc, qc_ref[...], dn, preferred_element_type=F32)
        p = jnp.exp2(s - s.max(axis=0, keepdims=True))
        l = p.sum(axis=0, keepdims=True)
        o = jnp.dot(vtc, p.astype(BF16), preferred_element_type=F32)
        oc_ref[...] = (o / l).T.astype(BF16)


def _na_bias_table(rpb, n_rows):
    n_heads = rpb.shape[0]
    qc = jnp.arange(GRID_W)[:, None]
    kc = jnp.arange(GRID_W)[None, :]
    kc0 = jnp.clip(qc - NA_WIN_COLS // 2, 0, GRID_W - NA_WIN_COLS)
    valid = (kc >= kc0) & (kc < kc0 + NA_WIN_COLS)
    off_c = kc - qc + (NA_WIN_COLS - 1)
    col = jnp.zeros((n_heads, 2 * NA_WIN_ROWS - 1, GRID_W, GRID_W), F32)
    for o in range(2 * NA_WIN_COLS - 1):
        col = col + jnp.where((off_c == o)[None, None], rpb[:, :, o, None, None].astype(F32), 0.0)
    col = jnp.where(valid[None, None], col * LOG2E, MASK_VALUE).transpose(0, 1, 3, 2)
    tables = []
    for r0, kr_base in ((0, 0), (NA_BLOCK_ROWS, 0), (n_rows - NA_BLOCK_ROWS, n_rows - NA_SPAN_ROWS)):
        per_row = []
        for t in range(NA_BLOCK_ROWS):
            q_row = r0 + t
            win0 = min(max(q_row - NA_WIN_ROWS // 2, 0), n_rows - NA_WIN_ROWS)
            u_lo = win0 - kr_base
            off_lo = win0 - q_row + (NA_WIN_ROWS - 1)
            assert 0 <= u_lo <= NA_SPAN_ROWS - NA_WIN_ROWS
            per_row.append(jnp.pad(col[:, off_lo:off_lo + NA_WIN_ROWS],
                                   ((0, 0), (u_lo, NA_SPAN_ROWS - NA_WIN_ROWS - u_lo), (0, 0), (0, 0)),
                                   constant_values=MASK_VALUE))
        tb = jnp.stack(per_row, axis=3)
        tables.append(tb.reshape(n_heads, NA_SPAN_ROWS * GRID_W, NA_BLOCK_ROWS * GRID_W))
    return jnp.stack(tables)


def _na_attention(qk, vt, bias_tbl, n_batch, seq, ctx_len):
    n_rows = seq // GRID_W
    bq = NA_BLOCK_ROWS * GRID_W
    span = NA_SPAN_ROWS * GRID_W
    step_rows = min(2048, seq)
    assert ctx_len == bq and n_rows % NA_BLOCK_ROWS == 0 and n_rows >= NA_SPAN_ROWS + NA_BLOCK_ROWS
    steps = seq // step_rows
    ctx_blk0 = n_batch * seq // ctx_len
    h_k = NA_HEADS
    return pl.pallas_call(
        functools.partial(_na_kernel, n_rows=n_rows, blocks_per_step=step_rows // bq),
        grid=(n_batch, NA_HEADS, steps),
        in_specs=[pl.BlockSpec((step_rows, HEAD_DIM), lambda b, h, c: (b * steps + c, h)),
                  pl.BlockSpec((seq, HEAD_DIM), lambda b, h, c: (b, h_k + h)),
                  pl.BlockSpec((HEAD_DIM, seq), lambda b, h, c: (h, b)),
                  pl.BlockSpec((ctx_len, HEAD_DIM), lambda b, h, c: (ctx_blk0 + b, h)),
                  pl.BlockSpec((ctx_len, HEAD_DIM), lambda b, h, c: (ctx_blk0 + b, h_k + h)),
                  pl.BlockSpec((HEAD_DIM, ctx_len), lambda b, h, c: (h, ctx_blk0 + b)),
                  pl.BlockSpec((3, None, span, bq), lambda b, h, c: (0, h, 0, 0))],
        out_specs=[pl.BlockSpec((step_rows, HEAD_DIM), lambda b, h, c: (b * steps + c, h)),
                   pl.BlockSpec((ctx_len, HEAD_DIM), lambda b, h, c: (b, h))],
        out_shape=[jax.ShapeDtypeStruct((n_batch * seq, NA_WIDTH), BF16),
                   jax.ShapeDtypeStruct((n_batch * ctx_len, NA_WIDTH), BF16)],
        compiler_params=_params(("parallel", "parallel", "arbitrary")),
        name="neighbourhood_attention",
    )(qk, qk, vt, qk, qk, vt, bias_tbl)


def _sg_kernel(uz_ref, ws_ref, bs_ref, lg_ref, lb_ref, o_ref, *, n_chunks):
    for g in range(SG_GROUPS):
        w = ws_ref[g].astype(BF16)
        for n in range(n_chunks):
            rows = slice(n * SG_CHUNK, (n + 1) * SG_CHUNK)
            u = uz_ref[rows, g * SG_DIM:(g + 1) * SG_DIM]
            z = uz_ref[rows, SG_WIDTH + g * SG_DIM:SG_WIDTH + (g + 1) * SG_DIM]
            zz = _gelu_tanh(z)
            mu = jnp.mean(zz, axis=-1, keepdims=True)
            xc = zz - mu
            var = jnp.mean(xc * xc, axis=-1, keepdims=True)
            zn = xc * lax.rsqrt(var + LN_EPS) * lg_ref[g] + lb_ref[g]
            mixed = jnp.dot(w, zn.astype(BF16), preferred_element_type=F32) + bs_ref[g]
            o_ref[rows, g * SG_DIM:(g + 1) * SG_DIM] = (_gelu_tanh(u) * mixed).astype(BF16)


def _spatial_gating(uz, w_s, b_s, ln_g, ln_b, tm):
    nt = uz.shape[0]
    full3 = lambda i: (0, 0, 0)
    return pl.pallas_call(
        functools.partial(_sg_kernel, n_chunks=tm // SG_CHUNK),
        grid=(nt // tm,),
        in_specs=[pl.BlockSpec((tm, 2 * SG_WIDTH), lambda i: (i, 0)),
                  pl.BlockSpec((SG_GROUPS, SG_CHUNK, SG_CHUNK), full3),
                  pl.BlockSpec((SG_GROUPS, SG_CHUNK, 1), full3),
                  pl.BlockSpec((SG_GROUPS, 1, SG_DIM), full3),
                  pl.BlockSpec((SG_GROUPS, 1, SG_DIM), full3)],
        out_specs=pl.BlockSpec((tm, SG_WIDTH), lambda i: (i, 0)),
        out_shape=jax.ShapeDtypeStruct((nt, SG_WIDTH), BF16),
        compiler_params=_params(("parallel",)),
        name="spatial_gating",
    )(uz, w_s, b_s.reshape(SG_GROUPS, SG_CHUNK, 1), ln_g.reshape(SG_GROUPS, 1, SG_DIM),
      ln_b.reshape(SG_GROUPS, 1, SG_DIM))


def _flash_kernel(q_ref, k_ref, vt_ref, kc_ref, vtc_ref, o_ref, q_s, s_a, s_b, s_c, acc_s, m_s,
                  *, tq, tk, seq, group, n_lat_q):
    qi = pl.program_id(2)
    for g in range(group):
        q_s[g * tq:(g + 1) * tq, :] = q_ref[:, g * HEAD_DIM:(g + 1) * HEAD_DIM]
    m_rows = group * tq
    n_blk = m_rows // MXU_N
    n_tiles = seq // tk
    dn = (((1,), (1,)), ((), ()))
    blk_cols = lambda n: slice(n * MXU_N, (n + 1) * MXU_N)

    def scores(k, s_ref, n):
        s_ref[:, blk_cols(n)] = lax.dot_general(k, q_s[blk_cols(n), :], dn, preferred_element_type=F32)

    def with_ones_row(vt):
        ones = jnp.where(lax.broadcasted_iota(jnp.int32, (BF16_ROWS, vt.shape[1]), 0) == 0, 1.0, 0.0)
        return jnp.concatenate([vt, ones.astype(BF16)], axis=0)

    def consume(s_ref, vt_aug, n):
        cols = blk_cols(n)
        s = s_ref[:, cols]
        m_old = m_s[:, cols]
        m_new = jnp.maximum(m_old, s.max(axis=0, keepdims=True))
        alpha = jnp.exp2(m_old - m_new)
        p = jnp.exp2(s - m_new).astype(BF16)
        acc_s[:, cols] = alpha * acc_s[:, cols] + jnp.dot(vt_aug, p, preferred_element_type=F32)
        m_s[:, cols] = m_new

    def scores_and_consume(k_next, s_next, s_cur, vt_cur):
        vt_aug = with_ones_row(vt_cur)
        for n in range(n_blk):
            scores(k_next, s_next, n)
            consume(s_cur, vt_aug, n)

    k_tile = lambda t: k_ref[pl.ds(pl.multiple_of(t * tk, tk), tk), :]
    vt_tile = lambda t: vt_ref[:, pl.ds(pl.multiple_of(t * tk, tk), tk)]

    m_s[...] = jnp.full(m_s.shape, MASK_VALUE, F32)
    acc_s[...] = jnp.zeros(acc_s.shape, F32)

    @pl.when(qi < n_lat_q)
    def _():
        for n in range(n_blk):
            scores(k_tile(0), s_a, n)

        def pair(u, carry):
            t0 = 2 * u
            scores_and_consume(k_tile(t0 + 1), s_b, s_a, vt_tile(t0))
            scores_and_consume(k_tile(t0 + 2), s_a, s_b, vt_tile(t0 + 1))
            return carry

        lax.fori_loop(0, n_tiles // 2 - 1, pair, 0)
        scores_and_consume(k_tile(n_tiles - 1), s_b, s_a, vt_tile(n_tiles - 2))
        scores_and_consume(kc_ref[...], s_c, s_b, vt_tile(n_tiles - 1))
        vtc_aug = with_ones_row(vtc_ref[...])
        for n in range(n_blk):
            consume(s_c, vtc_aug, n)

    @pl.when(qi == n_lat_q)
    def _():
        vtc_aug = with_ones_row(vtc_ref[...])
        for n in range(n_blk):
            scores(kc_ref[...], s_c, n)
            consume(s_c, vtc_aug, n)

    o = acc_s[:HEAD_DIM, :] / acc_s[HEAD_DIM:HEAD_DIM + 1, :]
    for g in range(group):
        o_ref[:, g * HEAD_DIM:(g + 1) * HEAD_DIM] = o[:, g * tq:(g + 1) * tq].T.astype(BF16)


def _flash_attention(qk, vt, n_batch, seq, ctx_len, n_q_heads):
    nt = qk.shape[0]
    group = n_q_heads // GQA_KV_HEADS
    tq = ctx_len
    tk = min(FLASH_KEY_TILE, seq // 2)
    assert seq % (2 * tk) == 0 and (group * tq) % MXU_N == 0
    n_lat_q = seq // tq
    ctx_blk0 = n_batch * seq // tq
    hk0 = n_q_heads
    gw = group * HEAD_DIM
    m_rows = group * tq

    def q_idx(b, kh, qi):
        return (jnp.where(qi < n_lat_q, b * n_lat_q + qi, ctx_blk0 + b), kh)

    return pl.pallas_call(
        functools.partial(_flash_kernel, tq=tq, tk=tk, seq=seq, group=group, n_lat_q=n_lat_q),
        grid=(n_batch, GQA_KV_HEADS, n_lat_q + 1),
        in_specs=[pl.BlockSpec((tq, gw), q_idx),
                  pl.BlockSpec((seq, HEAD_DIM), lambda b, kh, qi: (b, hk0 + kh)),
                  pl.BlockSpec((HEAD_DIM, seq), lambda b, kh, qi: (kh, b)),
                  pl.BlockSpec((tq, HEAD_DIM), lambda b, kh, qi: (ctx_blk0 + b, hk0 + kh)),
                  pl.BlockSpec((HEAD_DIM, tq), lambda b, kh, qi: (kh, ctx_blk0 + b))],
        out_specs=pl.BlockSpec((tq, gw), q_idx),
        out_shape=jax.ShapeDtypeStruct((nt, n_q_heads * HEAD_DIM), BF16),
        scratch_shapes=[pltpu.VMEM((m_rows, HEAD_DIM), BF16),
                        pltpu.VMEM((tk, m_rows), F32), pltpu.VMEM((tk, m_rows), F32),
                        pltpu.VMEM((tq, m_rows), F32),
                        pltpu.VMEM((HEAD_DIM + BF16_ROWS, m_rows), F32),
                        pltpu.VMEM((1, m_rows), F32)],
        compiler_params=_params(("parallel", "parallel", "arbitrary")),
        name="gqa_flash_attention",
    )(qk, qk, vt, qk, vt)


def _out_kernel(*refs, split, n_lat_tiles):
    n_lhs = len(split) + sum(split)
    lhs_refs = list(refs[:n_lhs])
    (w_ref, x_ref, ga_ref, g2_ref, sc_ref, sh_ref, wr_ref, br_ref, xo_ref, tok_ref, lg_ref) = refs[n_lhs:]
    is_latent_tile = pl.program_id(0) < n_lat_tiles
    acc = None
    k0 = 0
    for two in split:
        a = lhs_refs.pop(0)[...]
        if two:
            a = jnp.where(is_latent_tile, a, lhs_refs.pop(0)[...])
        kp = a.shape[1]
        part = jnp.dot(a, w_ref[k0:k0 + kp, :], preferred_element_type=F32)
        acc = part if acc is None else acc + part
        k0 += kp
    xn = x_ref[...] + ga_ref[...] * acc
    xo_ref[...] = xn
    tok = _rms_modulate(xn, g2_ref[...], sc_ref[...], sh_ref[...])
    tok_ref[...] = _pack_bf16_pairs(tok)
    lg_ref[...] = lax.dot_general(wr_ref[...], tok.astype(BF16), (((1,), (1,)), ((), ())),
                                  preferred_element_type=F32) + br_ref[...]


def _out_proj(lhs, w, xs, mods, g2, w_router, b_router, seg, tm, n_lat_tiles):
    nt, d = xs.shape
    k = w.shape[0]
    modspec = lambda which: pl.BlockSpec((None, None, 1, d), lambda i: (seg(i), which, 0, 0))
    lhs_args, lhs_specs = [], []
    for part in lhs:
        if isinstance(part, tuple):
            lat, cx = part
            lhs_args += [lat, cx]
            lhs_specs += [pl.BlockSpec((tm, lat.shape[1]), lambda i: (jnp.minimum(i, n_lat_tiles - 1), 0)),
                          pl.BlockSpec((tm, cx.shape[1]), lambda i: (jnp.maximum(i - n_lat_tiles, 0), 0))]
        else:
            lhs_args.append(part)
            lhs_specs.append(pl.BlockSpec((tm, part.shape[1]), lambda i: (i, 0)))
    return pl.pallas_call(
        functools.partial(_out_kernel, split=tuple(isinstance(p, tuple) for p in lhs), n_lat_tiles=n_lat_tiles),
        grid=(nt // tm,),
        in_specs=lhs_specs + [
            pl.BlockSpec((k, d), lambda i: (0, 0)),
            pl.BlockSpec((tm, d), lambda i: (i, 0)),
            modspec(2),
            pl.BlockSpec((1, d), lambda i: (0, 0)),
            modspec(4),
            modspec(3),
            pl.BlockSpec((ROUTER_W, d), lambda i: (0, 0)),
            pl.BlockSpec((ROUTER_W, 1), lambda i: (0, 0))],
        out_specs=[pl.BlockSpec((tm, d), lambda i: (i, 0)),
                   pl.BlockSpec((tm, d // 2), lambda i: (i, 0)),
                   pl.BlockSpec((ROUTER_W, tm), lambda i: (0, i))],
        out_shape=[jax.ShapeDtypeStruct((nt, d), F32),
                   jax.ShapeDtypeStruct((nt, d // 2), jnp.uint32),
                   jax.ShapeDtypeStruct((ROUTER_W, nt), F32)],
        compiler_params=_params(("parallel",)),
        name="out_proj_residual_router",
    )(*lhs_args, w, xs, mods, g2.reshape(1, d), mods, mods, w_router, b_router)


def _expert_kernel(blk_e_ref, n_used_ref, xs_ref, w1_ref, w3_ref, w2_ref, o_ref):
    b = pl.program_id(0)

    @pl.when(b < n_used_ref[0])
    def _():
        x = _unpack_bf16_pairs(xs_ref[...]).astype(BF16)
        h1 = jnp.dot(x, w1_ref[...].astype(BF16), preferred_element_type=F32)
        h3 = jnp.dot(x, w3_ref[...].astype(BF16), preferred_element_type=F32)
        a = (h1 * _sigmoid(h1) * h3).astype(BF16)
        o_ref[...] = _pack_bf16_pairs(jnp.dot(a, w2_ref[...].astype(BF16), preferred_element_type=F32))

    @pl.when(b >= n_used_ref[0])
    def _():
        o_ref[...] = jnp.zeros_like(o_ref)


def _experts(xs, blk_e, n_used, w1, w3, w2, layer):
    p, dw = xs.shape
    d, hid = w1.shape[-2:]
    nb = p // MOE_ROWS
    grid_spec = pltpu.PrefetchScalarGridSpec(
        num_scalar_prefetch=2,
        grid=(nb,),
        in_specs=[pl.BlockSpec((MOE_ROWS, dw), lambda b, be, nu: (jnp.minimum(b, nu[0] - 1), 0)),
                  pl.BlockSpec((None, None, d, hid), lambda b, be, nu: (layer, be[b], 0, 0)),
                  pl.BlockSpec((None, None, d, hid), lambda b, be, nu: (layer, be[b], 0, 0)),
                  pl.BlockSpec((None, None, hid, d), lambda b, be, nu: (layer, be[b], 0, 0))],
        out_specs=pl.BlockSpec((MOE_ROWS, dw), lambda b, be, nu: (b, 0)),
    )
    return pl.pallas_call(
        _expert_kernel,
        grid_spec=grid_spec,
        out_shape=jax.ShapeDtypeStruct((p, dw), jnp.uint32),
        compiler_params=_params(("arbitrary",)),
        name="moe_experts",
    )(blk_e, n_used, xs, w1, w3, w2)


def _first_index(vals, target):
    idx = jnp.full(target.shape, len(vals) - 1, jnp.int32)
    for i in range(len(vals) - 2, -1, -1):
        idx = jnp.where(vals[i] == target, i, idx)
    return idx


def _router_kernel(lg_ref, gate_ref, pos_ref, blk_ref, nused_ref, e_s, rank_s, *, n_tiles):
    tw = ROUTE_TILE
    tri = jnp.where(lax.broadcasted_iota(jnp.int32, (tw, tw), 0) < lax.broadcasted_iota(jnp.int32, (tw, tw), 1),
                    1.0, 0.0).astype(BF16)
    eid = lax.broadcasted_iota(jnp.int32, (MOE_EXPERTS, tw), 0)
    epg = MOE_EXPERTS_PER_GROUP

    def pass1(c, run):
        sl = pl.ds(pl.multiple_of(c * tw, tw), tw)
        g = [lg_ref[i:i + 1, sl] for i in range(MOE_GROUPS)]
        gm = functools.reduce(jnp.maximum, g)
        gidx = _first_index(g, gm)
        gval = 1.0 / functools.reduce(lambda a, b: a + b, [jnp.exp(gi - gm) for gi in g])
        le = []
        for e in range(epg):
            sel = lg_ref[MOE_GROUPS + (MOE_GROUPS - 1) * epg + e:MOE_GROUPS + (MOE_GROUPS - 1) * epg + e + 1, sl]
            for gg in range(MOE_GROUPS - 2, -1, -1):
                sel = jnp.where(gidx == gg, lg_ref[MOE_GROUPS + gg * epg + e:MOE_GROUPS + gg * epg + e + 1, sl], sel)
            le.append(sel)
        m1 = functools.reduce(jnp.maximum, le)
        i1 = _first_index(le, m1)
        le2 = [jnp.where(i1 == e, -jnp.inf, le[e]) for e in range(epg)]
        m2 = functools.reduce(jnp.maximum, le2)
        i2 = _first_index(le2, m2)
        t = jnp.exp(m2 - m1)
        inv = 1.0 / (1.0 + t)
        gate_ref[0:1, sl] = inv * gval
        gate_ref[1:2, sl] = t * inv * gval
        e0 = gidx * epg + i1
        e1 = gidx * epg + i2
        oh0 = eid == e0
        oh1 = eid == e1
        oh0f = jnp.where(oh0, 1.0, 0.0)
        oh1f = jnp.where(oh1, 1.0, 0.0)
        pre0 = jnp.dot(oh0f.astype(BF16), tri, preferred_element_type=F32)
        pre1 = jnp.dot(oh1f.astype(BF16), tri, preferred_element_type=F32)
        c0 = oh0f.sum(axis=1, keepdims=True)
        c1 = oh1f.sum(axis=1, keepdims=True)
        rank_s[0:1, sl] = jnp.where(oh0, run + pre0, 0.0).sum(axis=0, keepdims=True)
        rank_s[1:2, sl] = jnp.where(oh1, run + c0 + pre1, 0.0).sum(axis=0, keepdims=True)
        e_s[0:1, sl] = e0
        e_s[1:2, sl] = e1
        return run + c0 + c1

    counts = lax.fori_loop(0, n_tiles, pass1, jnp.zeros((MOE_EXPERTS, 1), F32))
    blocks = jnp.floor((counts + (MOE_ROWS - 1)) * (1.0 / MOE_ROWS))
    lincl = jnp.where(lax.broadcasted_iota(jnp.int32, (MOE_EXPERTS, MOE_EXPERTS), 1)
                      <= lax.broadcasted_iota(jnp.int32, (MOE_EXPERTS, MOE_EXPERTS), 0), 1.0, 0.0).astype(BF16)
    end_blocks = jnp.dot(lincl, jnp.broadcast_to(blocks, (MOE_EXPERTS, LANES)).astype(BF16),
                         preferred_element_type=F32)[:, 0:1]
    start_rows = (end_blocks - blocks) * MOE_ROWS

    def pass2(c, carry):
        sl = pl.ds(pl.multiple_of(c * tw, tw), tw)
        for k in range(MOE_TOP_K):
            base = jnp.where(eid == e_s[k:k + 1, sl], start_rows, 0.0).sum(axis=0, keepdims=True)
            pos_ref[k:k + 1, sl] = (base + rank_s[k:k + 1, sl]).astype(jnp.int32)
        return carry

    lax.fori_loop(0, n_tiles, pass2, 0)
    bl = lax.broadcasted_iota(jnp.int32, (MOE_EXPERTS, PLAN_LANES), 1).astype(F32)
    blk = jnp.where(end_blocks <= bl, 1.0, 0.0).sum(axis=0, keepdims=True)
    blk_ref[...] = jnp.minimum(blk, MOE_EXPERTS - 1.0).astype(jnp.int32)
    nused_ref[...] = jnp.broadcast_to(end_blocks[MOE_EXPERTS - 1:MOE_EXPERTS, :], (1, LANES)).astype(jnp.int32)


def _router(lg_t):
    n = lg_t.shape[1]
    assert n % ROUTE_TILE == 0
    return pl.pallas_call(
        functools.partial(_router_kernel, n_tiles=n // ROUTE_TILE),
        out_shape=[jax.ShapeDtypeStruct((MOE_TOP_K, n), F32),
                   jax.ShapeDtypeStruct((MOE_TOP_K, n), jnp.int32),
                   jax.ShapeDtypeStruct((1, PLAN_LANES), jnp.int32),
                   jax.ShapeDtypeStruct((1, LANES), jnp.int32)],
        scratch_shapes=[pltpu.VMEM((MOE_TOP_K, n), jnp.int32), pltpu.VMEM((MOE_TOP_K, n), F32)],
        compiler_params=pltpu.CompilerParams(vmem_limit_bytes=VMEM_LIMIT),
        name="moe_router_plan",
    )(lg_t)


def _dispatch_kernel(p0_ref, p1_ref, tok_ref, init_ref, xs_ref, sem, *, tm):
    del init_ref
    base = pl.program_id(0) * tm

    def copy(r, p_ref):
        return pltpu.make_async_copy(tok_ref.at[pl.ds(r, 1)], xs_ref.at[pl.ds(p_ref[base + r], 1)], sem)

    def start(r, carry):
        copy(r, p0_ref).start()
        copy(r, p1_ref).start()
        return carry

    def wait(r, carry):
        copy(r, p0_ref).wait()
        copy(r, p1_ref).wait()
        return carry

    lax.fori_loop(0, tm, start, 0, unroll=8)
    lax.fori_loop(0, tm, wait, 0, unroll=8)


def _dispatch(tok, pos0, pos1, init, tm):
    nt, d = tok.shape
    n_slots = init.shape[0]
    grid_spec = pltpu.PrefetchScalarGridSpec(
        num_scalar_prefetch=2,
        grid=(nt // tm,),
        in_specs=[pl.BlockSpec((tm, d), lambda i, p0, p1: (i, 0)),
                  pl.BlockSpec(memory_space=pl.ANY)],
        out_specs=pl.BlockSpec(memory_space=pl.ANY),
        scratch_shapes=[pltpu.SemaphoreType.DMA(())],
    )
    return pl.pallas_call(
        functools.partial(_dispatch_kernel, tm=tm),
        grid_spec=grid_spec,
        out_shape=jax.ShapeDtypeStruct((n_slots, d), tok.dtype),
        input_output_aliases={3: 0},
        compiler_params=_params(("arbitrary",)),
        name="moe_dispatch",
    )(pos0, pos1, tok, init)


def _combine_kernel(p0_ref, p1_ref, x_ref, yb_ref, gt_ref, ga_ref, *rest, tm, final):
    if final:
        gf_ref, o_ref, y0_s, y1_s, sem = rest
    else:
        o_ref, y0_s, y1_s, sem = rest
    i = pl.program_id(0)
    slot = i % 2

    def copy(tile, buf, r, p_ref, y_s):
        return pltpu.make_async_copy(yb_ref.at[pl.ds(p_ref[tile * tm + r], 1)], y_s.at[buf, pl.ds(r, 1)],
                                     sem.at[buf])

    def start_tile(tile, buf):
        def body(r, carry):
            copy(tile, buf, r, p0_ref, y0_s).start()
            copy(tile, buf, r, p1_ref, y1_s).start()
            return carry
        lax.fori_loop(0, tm, body, 0, unroll=8)

    def wait_tile(tile, buf):
        def body(r, carry):
            copy(tile, buf, r, p0_ref, y0_s).wait()
            copy(tile, buf, r, p1_ref, y1_s).wait()
            return carry
        lax.fori_loop(0, tm, body, 0, unroll=8)

    @pl.when(i == 0)
    def _():
        start_tile(i, slot)

    @pl.when(i + 1 < pl.num_programs(0))
    def _():
        start_tile(i + 1, 1 - slot)

    wait_tile(i, slot)
    gt = gt_ref[...]
    f = gt[:, 0:1] * _unpack_bf16_pairs(y0_s[slot]) + gt[:, 1:2] * _unpack_bf16_pairs(y1_s[slot])
    xn = x_ref[...] + ga_ref[...] * f
    if final:
        r = lax.rsqrt(jnp.mean(xn * xn, axis=-1, keepdims=True) + RMS_EPS)
        xn = xn * r * gf_ref[...]
    o_ref[...] = xn


def _combine(xs, yb, pos0, pos1, gate, mods, seg, tm, final_g=None, n_out_rows=None):
    nt, d = xs.shape
    rows = nt if n_out_rows is None else n_out_rows
    row_blk = lambda i, p0, p1: (i, 0)
    in_specs = [pl.BlockSpec((tm, d), row_blk),
                pl.BlockSpec(memory_space=pl.ANY),
                pl.BlockSpec((tm, MOE_TOP_K), row_blk),
                pl.BlockSpec((None, None, 1, d), lambda i, p0, p1: (seg(i), 5, 0, 0))]
    args = [xs, yb, gate, mods]
    if final_g is not None:
        in_specs.append(pl.BlockSpec((1, d), lambda i, p0, p1: (0, 0)))
        args.append(final_g.reshape(1, d))
    grid_spec = pltpu.PrefetchScalarGridSpec(
        num_scalar_prefetch=2,
        grid=(rows // tm,),
        in_specs=in_specs,
        out_specs=pl.BlockSpec((tm, d), row_blk),
        scratch_shapes=[pltpu.VMEM((2, tm, d // 2), jnp.uint32), pltpu.VMEM((2, tm, d // 2), jnp.uint32),
                        pltpu.SemaphoreType.DMA((2,))],
    )
    return pl.pallas_call(
        functools.partial(_combine_kernel, tm=tm, final=final_g is not None),
        grid_spec=grid_spec,
        out_shape=jax.ShapeDtypeStruct((rows, d), F32),
        compiler_params=_params(("arbitrary",)),
        name="moe_combine_residual",
    )(pos0, pos1, *args)


def _rope_tables(seq, pad_rows):
    t = jnp.arange(seq, dtype=jnp.int32)
    row = (t // GRID_W).astype(F32)
    col = (t % GRID_W).astype(F32)
    inv_freq = ROPE_THETA ** (-jnp.arange(ROPE_AXIS_DIM // 2, dtype=F32) * 2.0 / ROPE_AXIS_DIM)
    ang = jnp.concatenate([row[:, None] * inv_freq, col[:, None] * inv_freq], axis=-1)
    cos, sin = jnp.cos(ang), jnp.sin(ang)
    cos = jnp.concatenate([cos, jnp.ones((pad_rows, HEAD_DIM // 2), F32)], axis=0)
    sin = jnp.concatenate([sin, jnp.zeros((pad_rows, HEAD_DIM // 2), F32)], axis=0)
    return cos, sin


def _split_pairs(w, n_heads):
    d = w.shape[0]
    return w.reshape(d, n_heads, HEAD_DIM // 2, 2).transpose(0, 1, 3, 2).reshape(d, n_heads * HEAD_DIM)


def _gain_rope_tables(cos, sin, q_gain, k_gain):
    def tables(g):
        ge, go = g[0::2][None, :], g[1::2][None, :]
        return (jnp.concatenate([ge * cos, go * cos], axis=1),
                jnp.concatenate([-go * sin, ge * sin], axis=1))
    cq, sq = tables(q_gain)
    ck, sk = tables(k_gain)
    return jnp.stack([cq, ck]), jnp.stack([sq, sk])


def kernel(x, c, ctx, c_ctx, mod_w, mod_b, norm_mix_g, norm_ffn_g, norm_final_g, na_sg_w_in, na_sg_w_out, na_rpb, sg_w_s, sg_b_s, sg_ln_g, sg_ln_b, gqa_w_in, gqa_w_out, gqa_q_gain, gqa_k_gain, moe_w_group, moe_b_group, moe_w_expert, moe_b_expert, moe_w1, moe_w3, moe_w2):
    n_batch, seq, d = x.shape
    ctx_len = ctx.shape[1]
    depth = mod_w.shape[0]
    n_lat = n_batch * seq
    tm = min(ROW_TILE, ctx_len * n_batch)
    assert seq % tm == 0 and (n_batch * ctx_len) % tm == 0 and seq % ctx_len == 0
    assert ctx_len % GRID_W == 0 and n_batch + 1 <= 8
    tiles_per_batch = seq // tm
    seg = _seg_fn(tiles_per_batch, n_batch)
    n_q_heads = d // HEAD_DIM
    n_tok = n_lat + n_batch * ctx_len
    n_blocks = -(-n_tok * MOE_TOP_K // MOE_ROWS) + MOE_EXPERTS
    assert n_blocks <= PLAN_LANES and n_tok % ROUTE_TILE == 0

    xs = jnp.concatenate([x.reshape(n_lat, d), ctx.reshape(n_batch * ctx_len, d)], axis=0)
    cond = jnp.zeros((8, d), F32).at[:n_batch].set(c).at[n_batch].set(c_ctx)
    mod_all = _modulation(cond, mod_w, mod_b)
    cos, sin = _rope_tables(seq, tm)

    q_w = n_q_heads * HEAD_DIM
    kv_w = GQA_KV_HEADS * HEAD_DIM
    out = None
    slots = jnp.zeros((n_blocks * MOE_ROWS, d // 2), jnp.uint32)
    for layer in range(depth):
        last = layer == depth - 1
        mods = mod_all[layer, :n_batch + 1].reshape(n_batch + 1, 6, 1, d)
        i = layer // 2
        if layer % 2 == 0:
            w_in = na_sg_w_in[i]
            w_vt = w_in[:, 2 * NA_WIDTH:3 * NA_WIDTH].T.astype(BF16)
            qk, vt, uz = _in_proj_even(xs, norm_mix_g[layer], mods, w_in.astype(BF16), w_vt, seg, tm)
            att_lat, att_ctx = _na_attention(qk, vt, _na_bias_table(na_rpb[i], seq // GRID_W), n_batch, seq, ctx_len)
            gat = _spatial_gating(uz, sg_w_s[i], sg_b_s[i], sg_ln_g[i], sg_ln_b[i], tm)
            lhs, w_out = [(att_lat, att_ctx), gat], na_sg_w_out[i].astype(BF16)
        else:
            ctab, stab = _gain_rope_tables(cos, sin, gqa_q_gain[i] * (ATTN_SCALE * LOG2E), gqa_k_gain[i])
            w_qk = _split_pairs(gqa_w_in[i][:, :q_w + kv_w], n_q_heads + GQA_KV_HEADS).astype(BF16)
            w_vt = gqa_w_in[i][:, q_w + kv_w:].T.astype(BF16)
            qk, vt = _in_proj_odd(xs, norm_mix_g[layer], mods, w_qk, w_vt, ctab, stab,
                                  seg, tm, tiles_per_batch, n_lat // tm, q_w)
            att = _flash_attention(qk, vt, n_batch, seq, ctx_len, n_q_heads)
            lhs, w_out = [att], gqa_w_out[i].astype(BF16)

        pad = ROUTER_W - MOE_GROUPS - MOE_EXPERTS
        w_router = jnp.concatenate([moe_w_group[layer], moe_w_expert[layer], jnp.zeros((d, pad), F32)],
                                   axis=1).T.astype(BF16)
        b_router = jnp.concatenate([moe_b_group[layer], moe_b_expert[layer], jnp.zeros((pad,), F32)])[:, None]
        xs, tok, lg_t = _out_proj(lhs, w_out, xs, mods, norm_ffn_g[layer], w_router, b_router, seg, tm,
                                  n_lat // tm)

        gate, pos, blk, nused = _router(lg_t)
        pos0, pos1 = pos[0], pos[1]
        slots = _dispatch(tok, pos0, pos1, slots, tm)
        yb = _experts(slots, blk[0, :n_blocks], nused[0, :1], moe_w1, moe_w3, moe_w2, layer)
        if last:
            out = _combine(xs, yb, pos0, pos1, gate.T, mods, seg, tm, final_g=norm_final_g, n_out_rows=n_lat)
        else:
            xs = _combine(xs, yb, pos0, pos1, gate.T, mods, seg, tm)
    return out.reshape(n_batch, seq, d)
```

```python
import functools

import jax
import jax.numpy as jnp
from jax import lax
from jax.experimental import pallas as pl
from jax.experimental.pallas import tpu as pltpu

F32 = jnp.float32
BF16 = jnp.bfloat16

GRID_W = 64
HEAD_DIM = 128
NA_HEADS = 8
NA_WIDTH = NA_HEADS * HEAD_DIM
NA_WIN_ROWS = 8
NA_WIN_COLS = 16
NA_BLOCK_ROWS = 4
NA_SPAN_ROWS = 12
SG_GROUPS = 8
SG_DIM = 128
SG_WIDTH = SG_GROUPS * SG_DIM
SG_CHUNK = 128
GQA_KV_HEADS = 4
ROPE_THETA = 10000.0
ROPE_AXIS_DIM = HEAD_DIM // 2
MOE_GROUPS = 4
MOE_EXPERTS_PER_GROUP = 8
MOE_EXPERTS = MOE_GROUPS * MOE_EXPERTS_PER_GROUP
MOE_TOP_K = 2
RMS_EPS = 1e-6
LN_EPS = 1e-5

LOG2E = 1.4426950408889634
MASK_VALUE = -1e30
ATTN_SCALE = HEAD_DIM ** -0.5
LANES = 128
MXU_N = 256
BF16_ROWS = 16
ROUTER_W = LANES
ROW_TILE = 512
MOE_ROWS = 512
FLASH_KEY_TILE = 1024
ROUTE_TILE = 512
PLAN_LANES = 256
VMEM_LIMIT = 48 * 1024 * 1024
VMEM_LIMIT_BIG = 56 * 1024 * 1024


def _params(sem):
    return pltpu.CompilerParams(dimension_semantics=sem, vmem_limit_bytes=VMEM_LIMIT)


def _sigmoid(x):
    return 1.0 / (1.0 + jnp.exp(-x))


def _gelu_tanh(x):
    cdf = 0.5 * (1.0 + jnp.tanh(0.7978845608028654 * (x + 0.044715 * (x * x * x))))
    return x * cdf


def _pack_bf16_pairs(x):
    c = x.shape[1] // 2
    bits = pltpu.bitcast(x.astype(jnp.bfloat16).astype(F32), jnp.uint32)
    return (bits[:, :c] >> 16) | bits[:, c:]


def _unpack_bf16_pairs(w):
    lo = pltpu.bitcast(w << 16, F32)
    hi = pltpu.bitcast(w & jnp.uint32(0xFFFF0000), F32)
    return jnp.concatenate([lo, hi], axis=1)


def _rms_modulate(x, g, sc, sh):
    r = lax.rsqrt(jnp.mean(x * x, axis=-1, keepdims=True) + RMS_EPS)
    return (x * r * g) * (1.0 + sc) + sh


def _mod_kernel(c_ref, w_ref, b_ref, o_ref):
    c = c_ref[...]
    cs = (c * _sigmoid(c)).astype(BF16)
    o_ref[0] = jnp.dot(cs, w_ref[0].astype(BF16), preferred_element_type=F32) + b_ref[0]


def _modulation(cond, mod_w, mod_b):
    depth, d, n = mod_w.shape
    tn = 1024
    return pl.pallas_call(
        _mod_kernel,
        grid=(depth, n // tn),
        in_specs=[pl.BlockSpec((8, d), lambda l, j: (0, 0)),
                  pl.BlockSpec((1, d, tn), lambda l, j: (l, 0, j)),
                  pl.BlockSpec((1, 1, tn), lambda l, j: (l, 0, j))],
        out_specs=pl.BlockSpec((1, 8, tn), lambda l, j: (l, 0, j)),
        out_shape=jax.ShapeDtypeStruct((depth, 8, n), F32),
        compiler_params=_params(("parallel", "parallel")),
        name="adaln_modulation",
    )(cond, mod_w, mod_b.reshape(depth, 1, n))


def _seg_fn(tiles_per_batch, n_batch):
    return lambda i: jnp.minimum(i // tiles_per_batch, n_batch)


def _in_prologue(x_ref, g_ref, sc_ref, sh_ref, hx_ref):
    hx_ref[...] = _rms_modulate(x_ref[...], g_ref[...], sc_ref[...], sh_ref[...]).astype(BF16)


def _in_even_kernel(x_ref, g_ref, sc_ref, sh_ref, w_ref, wvt_ref, qk_ref, vt_ref, uz_ref, hx_ref, *, tn, q_width):
    _in_prologue(x_ref, g_ref, sc_ref, sh_ref, hx_ref)
    n_qk = qk_ref.shape[1]
    uz0 = n_qk + vt_ref.shape[0]
    for j in range(n_qk // tn):
        cols = slice(j * tn, (j + 1) * tn)
        acc = jnp.dot(hx_ref[...], w_ref[:, cols], preferred_element_type=F32)
        if j * tn < q_width:
            acc = acc * (ATTN_SCALE * LOG2E)
        qk_ref[:, cols] = acc.astype(BF16)
    for j in range(uz_ref.shape[1] // tn):
        uz_ref[:, j * tn:(j + 1) * tn] = jnp.dot(hx_ref[...], w_ref[:, uz0 + j * tn:uz0 + (j + 1) * tn],
                                                 preferred_element_type=F32)
    for j in range(vt_ref.shape[0] // tn):
        vt_ref[j * tn:(j + 1) * tn, :] = lax.dot_general(
            wvt_ref[j * tn:(j + 1) * tn, :], hx_ref[...], (((1,), (1,)), ((), ())),
            preferred_element_type=F32).astype(BF16)


def _in_odd_kernel(x_ref, g_ref, sc_ref, sh_ref, w_ref, wvt_ref, cq_ref, sq_ref, ck_ref, sk_ref,
                   qk_ref, vt_ref, hx_ref, *, q_width):
    _in_prologue(x_ref, g_ref, sc_ref, sh_ref, hx_ref)
    for hp in range(qk_ref.shape[1] // MXU_N):
        c_ref, s_ref = (cq_ref, sq_ref) if hp * MXU_N < q_width else (ck_ref, sk_ref)
        y2 = jnp.dot(hx_ref[...], w_ref[:, hp * MXU_N:(hp + 1) * MXU_N], preferred_element_type=F32)
        for h in range(MXU_N // HEAD_DIM):
            y = y2[:, h * HEAD_DIM:(h + 1) * HEAD_DIM]
            cols = slice(hp * MXU_N + h * HEAD_DIM, hp * MXU_N + (h + 1) * HEAD_DIM)
            r = lax.rsqrt(jnp.mean(y * y, axis=-1, keepdims=True) + RMS_EPS)
            sw = pltpu.roll(y, HEAD_DIM // 2, 1)
            qk_ref[:, cols] = ((y * c_ref[...] + sw * s_ref[...]) * r).astype(BF16)
    vt_ref[...] = lax.dot_general(wvt_ref[...], hx_ref[...], (((1,), (1,)), ((), ())),
                                  preferred_element_type=F32).astype(BF16)


def _common_in_specs(tm, d, seg):
    return [pl.BlockSpec((tm, d), lambda i, *_: (i, 0)),
            pl.BlockSpec((1, d), lambda i, *_: (0, 0)),
            pl.BlockSpec((None, None, 1, d), lambda i, *_: (seg(i), 1, 0, 0)),
            pl.BlockSpec((None, None, 1, d), lambda i, *_: (seg(i), 0, 0, 0))]


def _in_proj_even(xs, g, mods, w, w_vt, seg, tm):
    nt, d = xs.shape
    v_w = w_vt.shape[0]
    n = w.shape[1] - v_w
    n_qk = 2 * NA_WIDTH
    resident = dict(pipeline_mode=pl.Buffered(1))
    return pl.pallas_call(
        functools.partial(_in_even_kernel, tn=512, q_width=NA_WIDTH),
        grid=(nt // tm,),
        in_specs=_common_in_specs(tm, d, seg) + [
            pl.BlockSpec((d, n + v_w), lambda i: (0, 0), **resident),
            pl.BlockSpec((v_w, d), lambda i: (0, 0), **resident)],
        out_specs=[pl.BlockSpec((tm, n_qk), lambda i: (i, 0)),
                   pl.BlockSpec((v_w, tm), lambda i: (0, i)),
                   pl.BlockSpec((tm, n - n_qk), lambda i: (i, 0))],
        out_shape=[jax.ShapeDtypeStruct((nt, n_qk), BF16),
                   jax.ShapeDtypeStruct((v_w, nt), BF16),
                   jax.ShapeDtypeStruct((nt, n - n_qk), F32)],
        scratch_shapes=[pltpu.VMEM((tm, d), BF16)],
        compiler_params=pltpu.CompilerParams(dimension_semantics=("parallel",),
                                             vmem_limit_bytes=VMEM_LIMIT_BIG),
        name="in_proj_even",
    )(xs, g.reshape(1, d), mods, mods, w, w_vt)


def _in_proj_odd(xs, g, mods, w_qk, w_vt, ctab, stab, seg, tm, tiles_per_batch, n_lat_tiles, q_width):
    nt, d = xs.shape
    n_qk = w_qk.shape[1]
    kv_w = w_vt.shape[0]
    rope_row = lambda i: jnp.where(i < n_lat_tiles, i % tiles_per_batch, tiles_per_batch)
    tab_spec = lambda which: pl.BlockSpec((None, tm, HEAD_DIM), lambda i: (which, rope_row(i), 0))
    resident = dict(pipeline_mode=pl.Buffered(1))
    return pl.pallas_call(
        functools.partial(_in_odd_kernel, q_width=q_width),
        grid=(nt // tm,),
        in_specs=_common_in_specs(tm, d, seg) + [
            pl.BlockSpec((d, n_qk), lambda i: (0, 0), **resident),
            pl.BlockSpec((kv_w, d), lambda i: (0, 0), **resident),
            tab_spec(0), tab_spec(0), tab_spec(1), tab_spec(1)],
        out_specs=[pl.BlockSpec((tm, n_qk), lambda i: (i, 0)),
                   pl.BlockSpec((kv_w, tm), lambda i: (0, i))],
        out_shape=[jax.ShapeDtypeStruct((nt, n_qk), BF16),
                   jax.ShapeDtypeStruct((kv_w, nt), BF16)],
        scratch_shapes=[pltpu.VMEM((tm, d), BF16)],
        compiler_params=_params(("parallel",)),
        name="in_proj_odd",
    )(xs, g.reshape(1, d), mods, mods, w_qk, w_vt, ctab, stab, ctab, stab)


def _na_kernel(q_ref, k_ref, vt_ref, qc_ref, kc_ref, vtc_ref, bias_ref, o_ref, oc_ref, *, n_rows, blocks_per_step):
    c = pl.program_id(2)
    bq = NA_BLOCK_ROWS * GRID_W
    span = NA_SPAN_ROWS * GRID_W
    n_blocks = n_rows // NA_BLOCK_ROWS
    dn = (((1,), (1,)), ((), ()))
    kc = kc_ref[...]
    vtc = vtc_ref[...]

    def span_start(j):
        blk = c * blocks_per_step + j
        kr_base = jnp.clip(blk * NA_BLOCK_ROWS - NA_WIN_ROWS // 2, 0, n_rows - NA_SPAN_ROWS)
        return blk, pl.multiple_of(kr_base * GRID_W, NA_BLOCK_ROWS * GRID_W)

    def scores(j):
        blk, start = span_start(j)
        kind = jnp.where(blk == 0, 0, jnp.where(blk == n_blocks - 1, 2, 1))
        q = q_ref[j * bq:(j + 1) * bq, :]
        s_nb = lax.dot_general(k_ref[pl.ds(start, span), :], q, dn, preferred_element_type=F32) + bias_ref[kind]
        s_cx = lax.dot_general(kc, q, dn, preferred_element_type=F32)
        return s_nb, s_cx

    def finish(j, s_nb, s_cx):
        _, start = span_start(j)
        m = jnp.maximum(s_nb.max(axis=0, keepdims=True), s_cx.max(axis=0, keepdims=True))
        p_nb = jnp.exp2(s_nb - m)
        p_cx = jnp.exp2(s_cx - m)
        l = p_nb.sum(axis=0, keepdims=True) + p_cx.sum(axis=0, keepdims=True)
        o = (jnp.dot(vt_ref[:, pl.ds(start, span)], p_nb.astype(BF16), preferred_element_type=F32)
             + jnp.dot(vtc, p_cx.astype(BF16), preferred_element_type=F32))
        o_ref[j * bq:(j + 1) * bq, :] = (o / l).T.astype(BF16)

    cur = scores(0)
    for j in range(blocks_per_step):
        nxt = scores(j + 1) if j + 1 < blocks_per_step else None
        finish(j, *cur)
        cur = nxt

    @pl.when(c == pl.num_programs(2) - 1)
    def _():
        s = lax.dot_general(kc, qc_ref[...], dn, preferred_element_type=F32)
        p = jnp.exp2(s - s.max(axis=0, keepdims=True))
        l = p.sum(axis=0, keepdims=True)
        o = jnp.dot(vtc, p.astype(BF16), preferred_element_type=F32)
        oc_ref[...] = (o / l).T.astype(BF16)


def _na_bias_table(rpb, n_rows):
    n_heads = rpb.shape[0]
    qc = jnp.arange(GRID_W)[:, None]
    kc = jnp.arange(GRID_W)[None, :]
    kc0 = jnp.clip(qc - NA_WIN_COLS // 2, 0, GRID_W - NA_WIN_COLS)
    valid = (kc >= kc0) & (kc < kc0 + NA_WIN_COLS)
    off_c = kc - qc + (NA_WIN_COLS - 1)
    col = jnp.zeros((n_heads, 2 * NA_WIN_ROWS - 1, GRID_W, GRID_W), F32)
    for o in range(2 * NA_WIN_COLS - 1):
        col = col + jnp.where((off_c == o)[None, None], rpb[:, :, o, None, None].astype(F32), 0.0)
    col = jnp.where(valid[None, None], col * LOG2E, MASK_VALUE).transpose(0, 1, 3, 2)
    tables = []
    for r0, kr_base in ((0, 0), (NA_BLOCK_ROWS, 0), (n_rows - NA_BLOCK_ROWS, n_rows - NA_SPAN_ROWS)):
        per_row = []
        for t in range(NA_BLOCK_ROWS):
            q_row = r0 + t
            win0 = min(max(q_row - NA_WIN_ROWS // 2, 0), n_rows - NA_WIN_ROWS)
            u_lo = win0 - kr_base
            off_lo = win0 - q_row + (NA_WIN_ROWS - 1)
            assert 0 <= u_lo <= NA_SPAN_ROWS - NA_WIN_ROWS
            per_row.append(jnp.pad(col[:, off_lo:off_lo + NA_WIN_ROWS],
                                   ((0, 0), (u_lo, NA_SPAN_ROWS - NA_WIN_ROWS - u_lo), (0, 0), (0, 0)),
                                   constant_values=MASK_VALUE))
        tb = jnp.stack(per_row, axis=3)
        tables.append(tb.reshape(n_heads, NA_SPAN_ROWS * GRID_W, NA_BLOCK_ROWS * GRID_W))
    return jnp.stack(tables)


def _na_attention(qk, vt, bias_tbl, n_batch, seq, ctx_len):
    n_rows = seq // GRID_W
    bq = NA_BLOCK_ROWS * GRID_W
    span = NA_SPAN_ROWS * GRID_W
    step_rows = min(4096, seq)
    assert ctx_len == bq and n_rows % NA_BLOCK_ROWS == 0 and n_rows >= NA_SPAN_ROWS + NA_BLOCK_ROWS
    steps = seq // step_rows
    ctx_blk0 = n_batch * seq // ctx_len
    h_k = NA_HEADS
    return pl.pallas_call(
        functools.partial(_na_kernel, n_rows=n_rows, blocks_per_step=step_rows // bq),
        grid=(n_batch, NA_HEADS, steps),
        in_specs=[pl.BlockSpec((step_rows, HEAD_DIM), lambda b, h, c: (b * steps + c, h)),
                  pl.BlockSpec((seq, HEAD_DIM), lambda b, h, c: (b, h_k + h)),
                  pl.BlockSpec((HEAD_DIM, seq), lambda b, h, c: (h, b)),
                  pl.BlockSpec((ctx_len, HEAD_DIM), lambda b, h, c: (ctx_blk0 + b, h)),
                  pl.BlockSpec((ctx_len, HEAD_DIM), lambda b, h, c: (ctx_blk0 + b, h_k + h)),
                  pl.BlockSpec((HEAD_DIM, ctx_len), lambda b, h, c: (h, ctx_blk0 + b)),
                  pl.BlockSpec((3, None, span, bq), lambda b, h, c: (0, h, 0, 0))],
        out_specs=[pl.BlockSpec((step_rows, HEAD_DIM), lambda b, h, c: (b * steps + c, h)),
                   pl.BlockSpec((ctx_len, HEAD_DIM), lambda b, h, c: (b, h))],
        out_shape=[jax.ShapeDtypeStruct((n_batch * seq, NA_WIDTH), BF16),
                   jax.ShapeDtypeStruct((n_batch * ctx_len, NA_WIDTH), BF16)],
        compiler_params=_params(("parallel", "parallel", "arbitrary")),
        name="neighbourhood_attention",
    )(qk, qk, vt, qk, qk, vt, bias_tbl)


def _sg_kernel(uz_ref, ws_ref, bs_ref, lg_ref, lb_ref, o_ref, *, n_chunks):
    for g in range(SG_GROUPS):
        w = ws_ref[g].astype(BF16)
        for n in range(n_chunks):
            rows = slice(n * SG_CHUNK, (n + 1) * SG_CHUNK)
            u = uz_ref[rows, g * SG_DIM:(g + 1) * SG_DIM]
            z = uz_ref[rows, SG_WIDTH + g * SG_DIM:SG_WIDTH + (g + 1) * SG_DIM]
            zz = _gelu_tanh(z)
            mu = jnp.mean(zz, axis=-1, keepdims=True)
            xc = zz - mu
            var = jnp.mean(xc * xc, axis=-1, keepdims=True)
            zn = xc * lax.rsqrt(var + LN_EPS) * lg_ref[g] + lb_ref[g]
            mixed = jnp.dot(w, zn.astype(BF16), preferred_element_type=F32) + bs_ref[g]
            o_ref[rows, g * SG_DIM:(g + 1) * SG_DIM] = (_gelu_tanh(u) * mixed).astype(BF16)


def _spatial_gating(uz, w_s, b_s, ln_g, ln_b, tm):
    nt = uz.shape[0]
    full3 = lambda i: (0, 0, 0)
    return pl.pallas_call(
        functools.partial(_sg_kernel, n_chunks=tm // SG_CHUNK),
        grid=(nt // tm,),
        in_specs=[pl.BlockSpec((tm, 2 * SG_WIDTH), lambda i: (i, 0)),
                  pl.BlockSpec((SG_GROUPS, SG_CHUNK, SG_CHUNK), full3),
                  pl.BlockSpec((SG_GROUPS, SG_CHUNK, 1), full3),
                  pl.BlockSpec((SG_GROUPS, 1, SG_DIM), full3),
                  pl.BlockSpec((SG_GROUPS, 1, SG_DIM), full3)],
        out_specs=pl.BlockSpec((tm, SG_WIDTH), lambda i: (i, 0)),
        out_shape=jax.ShapeDtypeStruct((nt, SG_WIDTH), BF16),
        compiler_params=_params(("parallel",)),
        name="spatial_gating",
    )(uz, w_s, b_s.reshape(SG_GROUPS, SG_CHUNK, 1), ln_g.reshape(SG_GROUPS, 1, SG_DIM),
      ln_b.reshape(SG_GROUPS, 1, SG_DIM))


def _flash_kernel(q_ref, k_ref, vt_ref, kc_ref, vtc_ref, o_ref, q_s, s_a, s_b, s_c, acc_s, m_s,
                  *, tq, tk, seq, group, n_lat_q):
    qi = pl.program_id(2)
    for g in range(group):
        q_s[g * tq:(g + 1) * tq, :] = q_ref[:, g * HEAD_DIM:(g + 1) * HEAD_DIM]
    m_rows = group * tq
    n_blk = m_rows // MXU_N
    n_tiles = seq // tk
    dn = (((1,), (1,)), ((), ()))
    blk_cols = lambda n: slice(n * MXU_N, (n + 1) * MXU_N)

    def scores(k, s_ref, n):
        s_ref[:, blk_cols(n)] = lax.dot_general(k, q_s[blk_cols(n), :], dn, preferred_element_type=F32)

    def with_ones_row(vt):
        ones = jnp.where(lax.broadcasted_iota(jnp.int32, (BF16_ROWS, vt.shape[1]), 0) == 0, 1.0, 0.0)
        return jnp.concatenate([vt, ones.astype(BF16)], axis=0)

    def consume(s_ref, vt_aug, n):
        cols = blk_cols(n)
        s = s_ref[:, cols]
        m_old = m_s[:, cols]
        m_new = jnp.maximum(m_old, s.max(axis=0, keepdims=True))
        alpha = jnp.exp2(m_old - m_new)
        p = jnp.exp2(s - m_new).astype(BF16)
        acc_s[:, cols] = alpha * acc_s[:, cols] + jnp.dot(vt_aug, p, preferred_element_type=F32)
        m_s[:, cols] = m_new

    def scores_and_consume(k_next, s_next, s_cur, vt_cur):
        vt_aug = with_ones_row(vt_cur)
        for n in range(n_blk):
            scores(k_next, s_next, n)
            consume(s_cur, vt_aug, n)

    k_tile = lambda t: k_ref[pl.ds(pl.multiple_of(t * tk, tk), tk), :]
    vt_tile = lambda t: vt_ref[:, pl.ds(pl.multiple_of(t * tk, tk), tk)]

    m_s[...] = jnp.full(m_s.shape, MASK_VALUE, F32)
    acc_s[...] = jnp.zeros(acc_s.shape, F32)

    @pl.when(qi < n_lat_q)
    def _():
        for n in range(n_blk):
            scores(k_tile(0), s_a, n)

        def pair(u, carry):
            t0 = 2 * u
            scores_and_consume(k_tile(t0 + 1), s_b, s_a, vt_tile(t0))
            scores_and_consume(k_tile(t0 + 2), s_a, s_b, vt_tile(t0 + 1))
            return carry

        lax.fori_loop(0, n_tiles // 2 - 1, pair, 0)
        scores_and_consume(k_tile(n_tiles - 1), s_b, s_a, vt_tile(n_tiles - 2))
        scores_and_consume(kc_ref[...], s_c, s_b, vt_tile(n_tiles - 1))
        vtc_aug = with_ones_row(vtc_ref[...])
        for n in range(n_blk):
            consume(s_c, vtc_aug, n)

    @pl.when(qi == n_lat_q)
    def _():
        vtc_aug = with_ones_row(vtc_ref[...])
        for n in range(n_blk):
            scores(kc_ref[...], s_c, n)
            consume(s_c, vtc_aug, n)

    o = acc_s[:HEAD_DIM, :] / acc_s[HEAD_DIM:HEAD_DIM + 1, :]
    for g in range(group):
        o_ref[:, g * HEAD_DIM:(g + 1) * HEAD_DIM] = o[:, g * tq:(g + 1) * tq].T.astype(BF16)


def _flash_attention(qk, vt, n_batch, seq, ctx_len, n_q_heads):
    nt = qk.shape[0]
    group = n_q_heads // GQA_KV_HEADS
    tq = ctx_len
    tk = min(FLASH_KEY_TILE, seq // 2)
    assert seq % (2 * tk) == 0 and (group * tq) % MXU_N == 0
    n_lat_q = seq // tq
    ctx_blk0 = n_batch * seq // tq
    hk0 = n_q_heads
    gw = group * HEAD_DIM
    m_rows = group * tq

    def q_idx(b, kh, qi):
        return (jnp.where(qi < n_lat_q, b * n_lat_q + qi, ctx_blk0 + b), kh)

    return pl.pallas_call(
        functools.partial(_flash_kernel, tq=tq, tk=tk, seq=seq, group=group, n_lat_q=n_lat_q),
        grid=(n_batch, GQA_KV_HEADS, n_lat_q + 1),
        in_specs=[pl.BlockSpec((tq, gw), q_idx),
                  pl.BlockSpec((seq, HEAD_DIM), lambda b, kh, qi: (b, hk0 + kh)),
                  pl.BlockSpec((HEAD_DIM, seq), lambda b, kh, qi: (kh, b)),
                  pl.BlockSpec((tq, HEAD_DIM), lambda b, kh, qi: (ctx_blk0 + b, hk0 + kh)),
                  pl.BlockSpec((HEAD_DIM, tq), lambda b, kh, qi: (kh, ctx_blk0 + b))],
        out_specs=pl.BlockSpec((tq, gw), q_idx),
        out_shape=jax.ShapeDtypeStruct((nt, n_q_heads * HEAD_DIM), BF16),
        scratch_shapes=[pltpu.VMEM((m_rows, HEAD_DIM), BF16),
                        pltpu.VMEM((tk, m_rows), F32), pltpu.VMEM((tk, m_rows), F32),
                        pltpu.VMEM((tq, m_rows), F32),
                        pltpu.VMEM((HEAD_DIM + BF16_ROWS, m_rows), F32),
                        pltpu.VMEM((1, m_rows), F32)],
        compiler_params=_params(("parallel", "parallel", "arbitrary")),
        name="gqa_flash_attention",
    )(qk, qk, vt, qk, vt)


def _out_kernel(*refs, split, n_lat_tiles):
    n_lhs = len(split) + sum(split)
    lhs_refs = list(refs[:n_lhs])
    (w_ref, x_ref, ga_ref, g2_ref, sc_ref, sh_ref, wr_ref, br_ref, xo_ref, tok_ref, lg_ref) = refs[n_lhs:]
    is_latent_tile = pl.program_id(0) < n_lat_tiles
    acc = None
    k0 = 0
    for two in split:
        a = lhs_refs.pop(0)[...]
        if two:
            a = jnp.where(is_latent_tile, a, lhs_refs.pop(0)[...])
        kp = a.shape[1]
        part = jnp.dot(a, w_ref[k0:k0 + kp, :], preferred_element_type=F32)
        acc = part if acc is None else acc + part
        k0 += kp
    xn = x_ref[...] + ga_ref[...] * acc
    xo_ref[...] = xn
    tok = _rms_modulate(xn, g2_ref[...], sc_ref[...], sh_ref[...])
    tok_ref[...] = _pack_bf16_pairs(tok)
    lg_ref[...] = lax.dot_general(wr_ref[...], tok.astype(BF16), (((1,), (1,)), ((), ())),
                                  preferred_element_type=F32) + br_ref[...]


def _out_proj(lhs, w, xs, mods, g2, w_router, b_router, seg, tm, n_lat_tiles):
    nt, d = xs.shape
    k = w.shape[0]
    modspec = lambda which: pl.BlockSpec((None, None, 1, d), lambda i: (seg(i), which, 0, 0))
    lhs_args, lhs_specs = [], []
    for part in lhs:
        if isinstance(part, tuple):
            lat, cx = part
            lhs_args += [lat, cx]
            lhs_specs += [pl.BlockSpec((tm, lat.shape[1]), lambda i: (jnp.minimum(i, n_lat_tiles - 1), 0)),
                          pl.BlockSpec((tm, cx.shape[1]), lambda i: (jnp.maximum(i - n_lat_tiles, 0), 0))]
        else:
            lhs_args.append(part)
            lhs_specs.append(pl.BlockSpec((tm, part.shape[1]), lambda i: (i, 0)))
    return pl.pallas_call(
        functools.partial(_out_kernel, split=tuple(isinstance(p, tuple) for p in lhs), n_lat_tiles=n_lat_tiles),
        grid=(nt // tm,),
        in_specs=lhs_specs + [
            pl.BlockSpec((k, d), lambda i: (0, 0)),
            pl.BlockSpec((tm, d), lambda i: (i, 0)),
            modspec(2),
            pl.BlockSpec((1, d), lambda i: (0, 0)),
            modspec(4),
            modspec(3),
            pl.BlockSpec((ROUTER_W, d), lambda i: (0, 0)),
            pl.BlockSpec((ROUTER_W, 1), lambda i: (0, 0))],
        out_specs=[pl.BlockSpec((tm, d), lambda i: (i, 0)),
                   pl.BlockSpec((tm, d // 2), lambda i: (i, 0)),
                   pl.BlockSpec((ROUTER_W, tm), lambda i: (0, i))],
        out_shape=[jax.ShapeDtypeStruct((nt, d), F32),
                   jax.ShapeDtypeStruct((nt, d // 2), jnp.uint32),
                   jax.ShapeDtypeStruct((ROUTER_W, nt), F32)],
        compiler_params=_params(("parallel",)),
        name="out_proj_residual_router",
    )(*lhs_args, w, xs, mods, g2.reshape(1, d), mods, mods, w_router, b_router)


def _expert_kernel(blk_e_ref, n_used_ref, xs_ref, w1_ref, w3_ref, w2_ref, o_ref):
    b = pl.program_id(0)

    @pl.when(b < n_used_ref[0])
    def _():
        x = _unpack_bf16_pairs(xs_ref[...]).astype(BF16)
        h1 = jnp.dot(x, w1_ref[...].astype(BF16), preferred_element_type=F32)
        h3 = jnp.dot(x, w3_ref[...].astype(BF16), preferred_element_type=F32)
        a = (h1 * _sigmoid(h1) * h3).astype(BF16)
        o_ref[...] = _pack_bf16_pairs(jnp.dot(a, w2_ref[...].astype(BF16), preferred_element_type=F32))

    @pl.when(b >= n_used_ref[0])
    def _():
        o_ref[...] = jnp.zeros_like(o_ref)


def _experts(xs, blk_e, n_used, w1, w3, w2, layer):
    p, dw = xs.shape
    d, hid = w1.shape[-2:]
    nb = p // MOE_ROWS
    grid_spec = pltpu.PrefetchScalarGridSpec(
        num_scalar_prefetch=2,
        grid=(nb,),
        in_specs=[pl.BlockSpec((MOE_ROWS, dw), lambda b, be, nu: (jnp.minimum(b, nu[0] - 1), 0)),
                  pl.BlockSpec((None, None, d, hid), lambda b, be, nu: (layer, be[b], 0, 0)),
                  pl.BlockSpec((None, None, d, hid), lambda b, be, nu: (layer, be[b], 0, 0)),
                  pl.BlockSpec((None, None, hid, d), lambda b, be, nu: (layer, be[b], 0, 0))],
        out_specs=pl.BlockSpec((MOE_ROWS, dw), lambda b, be, nu: (b, 0)),
    )
    return pl.pallas_call(
        _expert_kernel,
        grid_spec=grid_spec,
        out_shape=jax.ShapeDtypeStruct((p, dw), jnp.uint32),
        compiler_params=_params(("arbitrary",)),
        name="moe_experts",
    )(blk_e, n_used, xs, w1, w3, w2)


def _first_index(vals, target):
    idx = jnp.full(target.shape, len(vals) - 1, jnp.int32)
    for i in range(len(vals) - 2, -1, -1):
        idx = jnp.where(vals[i] == target, i, idx)
    return idx


def _router_kernel(lg_ref, gate_ref, pos_ref, blk_ref, nused_ref, e_s, rank_s, *, n_tiles):
    tw = ROUTE_TILE
    tri = jnp.where(lax.broadcasted_iota(jnp.int32, (tw, tw), 0) < lax.broadcasted_iota(jnp.int32, (tw, tw), 1),
                    1.0, 0.0).astype(BF16)
    eid = lax.broadcasted_iota(jnp.int32, (MOE_EXPERTS, tw), 0)
    epg = MOE_EXPERTS_PER_GROUP

    def pass1(c, run):
        sl = pl.ds(pl.multiple_of(c * tw, tw), tw)
        g = [lg_ref[i:i + 1, sl] for i in range(MOE_GROUPS)]
        gm = functools.reduce(jnp.maximum, g)
        gidx = _first_index(g, gm)
        gval = 1.0 / functools.reduce(lambda a, b: a + b, [jnp.exp(gi - gm) for gi in g])
        le = []
        for e in range(epg):
            sel = lg_ref[MOE_GROUPS + (MOE_GROUPS - 1) * epg + e:MOE_GROUPS + (MOE_GROUPS - 1) * epg + e + 1, sl]
            for gg in range(MOE_GROUPS - 2, -1, -1):
                sel = jnp.where(gidx == gg, lg_ref[MOE_GROUPS + gg * epg + e:MOE_GROUPS + gg * epg + e + 1, sl], sel)
            le.append(sel)
        m1 = functools.reduce(jnp.maximum, le)
        i1 = _first_index(le, m1)
        le2 = [jnp.where(i1 == e, -jnp.inf, le[e]) for e in range(epg)]
        m2 = functools.reduce(jnp.maximum, le2)
        i2 = _first_index(le2, m2)
        t = jnp.exp(m2 - m1)
        inv = 1.0 / (1.0 + t)
        gate_ref[0:1, sl] = inv * gval
        gate_ref[1:2, sl] = t * inv * gval
        e0 = gidx * epg + i1
        e1 = gidx * epg + i2
        oh0 = eid == e0
        oh1 = eid == e1
        oh0f = jnp.where(oh0, 1.0, 0.0)
        oh1f = jnp.where(oh1, 1.0, 0.0)
        pre0 = jnp.dot(oh0f.astype(BF16), tri, preferred_element_type=F32)
        pre1 = jnp.dot(oh1f.astype(BF16), tri, preferred_element_type=F32)
        c0 = oh0f.sum(axis=1, keepdims=True)
        c1 = oh1f.sum(axis=1, keepdims=True)
        rank_s[0:1, sl] = jnp.where(oh0, run + pre0, 0.0).sum(axis=0, keepdims=True)
        rank_s[1:2, sl] = jnp.where(oh1, run + c0 + pre1, 0.0).sum(axis=0, keepdims=True)
        e_s[0:1, sl] = e0
        e_s[1:2, sl] = e1
        return run + c0 + c1

    counts = lax.fori_loop(0, n_tiles, pass1, jnp.zeros((MOE_EXPERTS, 1), F32))
    blocks = jnp.floor((counts + (MOE_ROWS - 1)) * (1.0 / MOE_ROWS))
    lincl = jnp.where(lax.broadcasted_iota(jnp.int32, (MOE_EXPERTS, MOE_EXPERTS), 1)
                      <= lax.broadcasted_iota(jnp.int32, (MOE_EXPERTS, MOE_EXPERTS), 0), 1.0, 0.0).astype(BF16)
    end_blocks = jnp.dot(lincl, jnp.broadcast_to(blocks, (MOE_EXPERTS, LANES)).astype(BF16),
                         preferred_element_type=F32)[:, 0:1]
    start_rows = (end_blocks - blocks) * MOE_ROWS

    def pass2(c, carry):
        sl = pl.ds(pl.multiple_of(c * tw, tw), tw)
        for k in range(MOE_TOP_K):
            base = jnp.where(eid == e_s[k:k + 1, sl], start_rows, 0.0).sum(axis=0, keepdims=True)
            pos_ref[k:k + 1, sl] = (base + rank_s[k:k + 1, sl]).astype(jnp.int32)
        return carry

    lax.fori_loop(0, n_tiles, pass2, 0)
    bl = lax.broadcasted_iota(jnp.int32, (MOE_EXPERTS, PLAN_LANES), 1).astype(F32)
    blk = jnp.where(end_blocks <= bl, 1.0, 0.0).sum(axis=0, keepdims=True)
    blk_ref[...] = jnp.minimum(blk, MOE_EXPERTS - 1.0).astype(jnp.int32)
    nused_ref[...] = jnp.broadcast_to(end_blocks[MOE_EXPERTS - 1:MOE_EXPERTS, :], (1, LANES)).astype(jnp.int32)


def _router(lg_t):
    n = lg_t.shape[1]
    assert n % ROUTE_TILE == 0
    return pl.pallas_call(
        functools.partial(_router_kernel, n_tiles=n // ROUTE_TILE),
        out_shape=[jax.ShapeDtypeStruct((MOE_TOP_K, n), F32),
                   jax.ShapeDtypeStruct((MOE_TOP_K, n), jnp.int32),
                   jax.ShapeDtypeStruct((1, PLAN_LANES), jnp.int32),
                   jax.ShapeDtypeStruct((1, LANES), jnp.int32)],
        scratch_shapes=[pltpu.VMEM((MOE_TOP_K, n), jnp.int32), pltpu.VMEM((MOE_TOP_K, n), F32)],
        compiler_params=pltpu.CompilerParams(vmem_limit_bytes=VMEM_LIMIT),
        name="moe_router_plan",
    )(lg_t)


def _dispatch_kernel(p0_ref, p1_ref, tok_ref, init_ref, xs_ref, sem, *, tm):
    del init_ref
    base = pl.program_id(0) * tm

    def copy(r, p_ref):
        return pltpu.make_async_copy(tok_ref.at[pl.ds(r, 1)], xs_ref.at[pl.ds(p_ref[base + r], 1)], sem)

    def start(r, carry):
        copy(r, p0_ref).start()
        copy(r, p1_ref).start()
        return carry

    def wait(r, carry):
        copy(r, p0_ref).wait()
        copy(r, p1_ref).wait()
        return carry

    lax.fori_loop(0, tm, start, 0, unroll=8)
    lax.fori_loop(0, tm, wait, 0, unroll=8)


def _dispatch(tok, pos0, pos1, init, tm):
    nt, d = tok.shape
    n_slots = init.shape[0]
    grid_spec = pltpu.PrefetchScalarGridSpec(
        num_scalar_prefetch=2,
        grid=(nt // tm,),
        in_specs=[pl.BlockSpec((tm, d), lambda i, p0, p1: (i, 0)),
                  pl.BlockSpec(memory_space=pl.ANY)],
        out_specs=pl.BlockSpec(memory_space=pl.ANY),
        scratch_shapes=[pltpu.SemaphoreType.DMA(())],
    )
    return pl.pallas_call(
        functools.partial(_dispatch_kernel, tm=tm),
        grid_spec=grid_spec,
        out_shape=jax.ShapeDtypeStruct((n_slots, d), tok.dtype),
        input_output_aliases={3: 0},
        compiler_params=_params(("arbitrary",)),
        name="moe_dispatch",
    )(pos0, pos1, tok, init)


def _combine_kernel(p0_ref, p1_ref, x_ref, yb_ref, gt_ref, ga_ref, *rest, tm, final):
    if final:
        gf_ref, o_ref, y0_s, y1_s, sem = rest
    else:
        o_ref, y0_s, y1_s, sem = rest
    i = pl.program_id(0)
    slot = i % 2

    def copy(tile, buf, r, p_ref, y_s):
        return pltpu.make_async_copy(yb_ref.at[pl.ds(p_ref[tile * tm + r], 1)], y_s.at[buf, pl.ds(r, 1)],
                                     sem.at[buf])

    def start_tile(tile, buf):
        def body(r, carry):
            copy(tile, buf, r, p0_ref, y0_s).start()
            copy(tile, buf, r, p1_ref, y1_s).start()
            return carry
        lax.fori_loop(0, tm, body, 0, unroll=8)

    def wait_tile(tile, buf):
        def body(r, carry):
            copy(tile, buf, r, p0_ref, y0_s).wait()
            copy(tile, buf, r, p1_ref, y1_s).wait()
            return carry
        lax.fori_loop(0, tm, body, 0, unroll=8)

    @pl.when(i == 0)
    def _():
        start_tile(i, slot)

    @pl.when(i + 1 < pl.num_programs(0))
    def _():
        start_tile(i + 1, 1 - slot)

    wait_tile(i, slot)
    gt = gt_ref[...]
    f = gt[:, 0:1] * _unpack_bf16_pairs(y0_s[slot]) + gt[:, 1:2] * _unpack_bf16_pairs(y1_s[slot])
    xn = x_ref[...] + ga_ref[...] * f
    if final:
        r = lax.rsqrt(jnp.mean(xn * xn, axis=-1, keepdims=True) + RMS_EPS)
        xn = xn * r * gf_ref[...]
    o_ref[...] = xn


def _combine(xs, yb, pos0, pos1, gate, mods, seg, tm, final_g=None, n_out_rows=None):
    nt, d = xs.shape
    rows = nt if n_out_rows is None else n_out_rows
    row_blk = lambda i, p0, p1: (i, 0)
    in_specs = [pl.BlockSpec((tm, d), row_blk),
                pl.BlockSpec(memory_space=pl.ANY),
                pl.BlockSpec((tm, MOE_TOP_K), row_blk),
                pl.BlockSpec((None, None, 1, d), lambda i, p0, p1: (seg(i), 5, 0, 0))]
    args = [xs, yb, gate, mods]
    if final_g is not None:
        in_specs.append(pl.BlockSpec((1, d), lambda i, p0, p1: (0, 0)))
        args.append(final_g.reshape(1, d))
    grid_spec = pltpu.PrefetchScalarGridSpec(
        num_scalar_prefetch=2,
        grid=(rows // tm,),
        in_specs=in_specs,
        out_specs=pl.BlockSpec((tm, d), row_blk),
        scratch_shapes=[pltpu.VMEM((2, tm, d // 2), jnp.uint32), pltpu.VMEM((2, tm, d // 2), jnp.uint32),
                        pltpu.SemaphoreType.DMA((2,))],
    )
    return pl.pallas_call(
        functools.partial(_combine_kernel, tm=tm, final=final_g is not None),
        grid_spec=grid_spec,
        out_shape=jax.ShapeDtypeStruct((rows, d), F32),
        compiler_params=_params(("arbitrary",)),
        name="moe_combine_residual",
    )(pos0, pos1, *args)


def _rope_tables(seq, pad_rows):
    t = jnp.arange(seq, dtype=jnp.int32)
    row = (t // GRID_W).astype(F32)
    col = (t % GRID_W).astype(F32)
    inv_freq = ROPE_THETA ** (-jnp.arange(ROPE_AXIS_DIM // 2, dtype=F32) * 2.0 / ROPE_AXIS_DIM)
    ang = jnp.concatenate([row[:, None] * inv_freq, col[:, None] * inv_freq], axis=-1)
    cos, sin = jnp.cos(ang), jnp.sin(ang)
    cos = jnp.concatenate([cos, jnp.ones((pad_rows, HEAD_DIM // 2), F32)], axis=0)
    sin = jnp.concatenate([sin, jnp.zeros((pad_rows, HEAD_DIM // 2), F32)], axis=0)
    return cos, sin


def _split_pairs(w, n_heads):
    d = w.shape[0]
    return w.reshape(d, n_heads, HEAD_DIM // 2, 2).transpose(0, 1, 3, 2).reshape(d, n_heads * HEAD_DIM)


def _gain_rope_tables(cos, sin, q_gain, k_gain):
    def tables(g):
        ge, go = g[0::2][None, :], g[1::2][None, :]
        return (jnp.concatenate([ge * cos, go * cos], axis=1),
                jnp.concatenate([-go * sin, ge * sin], axis=1))
    cq, sq = tables(q_gain)
    ck, sk = tables(k_gain)
    return jnp.stack([cq, ck]), jnp.stack([sq, sk])


def kernel(x, c, ctx, c_ctx, mod_w, mod_b, norm_mix_g, norm_ffn_g, norm_final_g, na_sg_w_in, na_sg_w_out, na_rpb, sg_w_s, sg_b_s, sg_ln_g, sg_ln_b, gqa_w_in, gqa_w_out, gqa_q_gain, gqa_k_gain, moe_w_group, moe_b_group, moe_w_expert, moe_b_expert, moe_w1, moe_w3, moe_w2):
    n_batch, seq, d = x.shape
    ctx_len = ctx.shape[1]
    depth = mod_w.shape[0]
    n_lat = n_batch * seq
    tm = min(ROW_TILE, ctx_len * n_batch)
    assert seq % tm == 0 and (n_batch * ctx_len) % tm == 0 and seq % ctx_len == 0
    assert ctx_len % GRID_W == 0 and n_batch + 1 <= 8
    tiles_per_batch = seq // tm
    seg = _seg_fn(tiles_per_batch, n_batch)
    n_q_heads = d // HEAD_DIM
    n_tok = n_lat + n_batch * ctx_len
    n_blocks = -(-n_tok * MOE_TOP_K // MOE_ROWS) + MOE_EXPERTS
    assert n_blocks <= PLAN_LANES and n_tok % ROUTE_TILE == 0

    xs = jnp.concatenate([x.reshape(n_lat, d), ctx.reshape(n_batch * ctx_len, d)], axis=0)
    cond = jnp.zeros((8, d), F32).at[:n_batch].set(c).at[n_batch].set(c_ctx)
    mod_all = _modulation(cond, mod_w, mod_b)
    cos, sin = _rope_tables(seq, tm)

    q_w = n_q_heads * HEAD_DIM
    kv_w = GQA_KV_HEADS * HEAD_DIM
    out = None
    slots = jnp.zeros((n_blocks * MOE_ROWS, d // 2), jnp.uint32)
    for layer in range(depth):
        last = layer == depth - 1
        mods = mod_all[layer, :n_batch + 1].reshape(n_batch + 1, 6, 1, d)
        i = layer // 2
        if layer % 2 == 0:
            w_in = na_sg_w_in[i]
            w_vt = w_in[:, 2 * NA_WIDTH:3 * NA_WIDTH].T.astype(BF16)
            qk, vt, uz = _in_proj_even(xs, norm_mix_g[layer], mods, w_in.astype(BF16), w_vt, seg, tm)
            att_lat, att_ctx = _na_attention(qk, vt, _na_bias_table(na_rpb[i], seq // GRID_W), n_batch, seq, ctx_len)
            gat = _spatial_gating(uz, sg_w_s[i], sg_b_s[i], sg_ln_g[i], sg_ln_b[i], tm)
            lhs, w_out = [(att_lat, att_ctx), gat], na_sg_w_out[i].astype(BF16)
        else:
            ctab, stab = _gain_rope_tables(cos, sin, gqa_q_gain[i] * (ATTN_SCALE * LOG2E), gqa_k_gain[i])
            w_qk = _split_pairs(gqa_w_in[i][:, :q_w + kv_w], n_q_heads + GQA_KV_HEADS).astype(BF16)
            w_vt = gqa_w_in[i][:, q_w + kv_w:].T.astype(BF16)
            qk, vt = _in_proj_odd(xs, norm_mix_g[layer], mods, w_qk, w_vt, ctab, stab,
                                  seg, tm, tiles_per_batch, n_lat // tm, q_w)
            att = _flash_attention(qk, vt, n_batch, seq, ctx_len, n_q_heads)
            lhs, w_out = [att], gqa_w_out[i].astype(BF16)

        pad = ROUTER_W - MOE_GROUPS - MOE_EXPERTS
        w_router = jnp.concatenate([moe_w_group[layer], moe_w_expert[layer], jnp.zeros((d, pad), F32)],
                                   axis=1).T.astype(BF16)
        b_router = jnp.concatenate([moe_b_group[layer], moe_b_expert[layer], jnp.zeros((pad,), F32)])[:, None]
        xs, tok, lg_t = _out_proj(lhs, w_out, xs, mods, norm_ffn_g[layer], w_router, b_router, seg, tm,
                                  n_lat // tm)

        gate, pos, blk, nused = _router(lg_t)
        pos0, pos1 = pos[0], pos[1]
        slots = _dispatch(tok, pos0, pos1, slots, tm)
        yb = _experts(slots, blk[0, :n_blocks], nused[0, :1], moe_w1, moe_w3, moe_w2, layer)
        if last:
            out = _combine(xs, yb, pos0, pos1, gate.T, mods, seg, tm, final_g=norm_final_g, n_out_rows=n_lat)
        else:
            xs = _combine(xs, yb, pos0, pos1, gate.T, mods, seg, tm)
    return out.reshape(n_batch, seq, d)
```
